```python
import math
import jax, jax.numpy as jnp
from jax import lax
import numpy as np

D_MODEL = 1024
BATCH = 8
SEQ = 2048
DEPTH = 1
DEC_BATCH = 128
DEC_SEQ = 1
PAST_LEN = 16384
PAGE_SIZE = 128

S5_WIDTH = D_MODEL // 2
S5_GROUP = 16
S5_GROUPS = S5_WIDTH // S5_GROUP
S5_STATE = 64
DT_MIN = 1e-3
DT_MAX = 1e-1
RWKV_HEAD = 64
RWKV_WIDTH = D_MODEL // 2
RWKV_HEADS = RWKV_WIDTH // RWKV_HEAD
W_LORA = 64
A_LORA = 64
G_LORA = 128
N_SHIFT = 3 * RWKV_WIDTH + W_LORA + A_LORA + G_LORA
RWKV_SPLITS = (RWKV_WIDTH, 2 * RWKV_WIDTH, 3 * RWKV_WIDTH, 3 * RWKV_WIDTH + W_LORA, 3 * RWKV_WIDTH + W_LORA + A_LORA)
N_IN = S5_WIDTH + N_SHIFT + 2 * D_MODEL
D_FF = 128 * ((8 * D_MODEL // 3 + 127) // 128)
N_MOD = 9
NORM_EPS = 1e-6
GN_EPS = 64e-5

kernel_name = 'hybrid_s5_rwkv7_macaron_adaln_step'


def _rmsnorm(x, g):
    xf = x.astype(jnp.float32)
    xf = xf * lax.rsqrt(jnp.mean(xf * xf, axis=-1, keepdims=True) + NORM_EPS)
    return (xf * g).astype(x.dtype)


def _modulate(n, shift, scale):
    return n * (1 + scale) + shift


def _swiglu(x, w1, w3, w2):
    return (jax.nn.silu(x @ w1) * (x @ w3)) @ w2


def _s5_branch(u, h_re0, h_im0, p):
    f32 = jnp.float32
    bn, length, _ = u.shape
    uf = u.astype(f32)
    ug = uf.reshape(bn, length, S5_GROUPS, S5_GROUP)
    dt = jnp.exp(p['s5_log_dt'].astype(f32))[:, None]
    lam_re = p['s5_lam_re'].astype(f32)
    lam_im = p['s5_lam_im'].astype(f32)
    mag = jnp.exp(lam_re * dt)
    ang = lam_im * dt
    lb_re = mag * jnp.cos(ang)
    lb_im = mag * jnp.sin(ang)
    den = lam_re * lam_re + lam_im * lam_im
    num_re = lb_re - 1
    k_re = (num_re * lam_re + lb_im * lam_im) / den
    k_im = (lb_im * lam_re - num_re * lam_im) / den
    b_re = p['s5_b_re'].astype(f32)
    b_im = p['s5_b_im'].astype(f32)
    bb_re = k_re[..., None] * b_re - k_im[..., None] * b_im
    bb_im = k_re[..., None] * b_im + k_im[..., None] * b_re
    bu_re = jnp.einsum('blgc,gnc->blgn', ug, bb_re)
    bu_im = jnp.einsum('blgc,gnc->blgn', ug, bb_im)
    h_re0 = h_re0.astype(f32)
    h_im0 = h_im0.astype(f32)
    bu_re = bu_re.at[:, 0].add(lb_re * h_re0 - lb_im * h_im0)
    bu_im = bu_im.at[:, 0].add(lb_re * h_im0 + lb_im * h_re0)
    a_re = jnp.broadcast_to(lb_re, bu_re.shape)
    a_im = jnp.broadcast_to(lb_im, bu_im.shape)

    def combine(e1, e2):
        a1r, a1i, b1r, b1i = e1
        a2r, a2i, b2r, b2i = e2
        return (a1r * a2r - a1i * a2i,
                a1r * a2i + a1i * a2r,
                a2r * b1r - a2i * b1i + b2r,
                a2r * b1i + a2i * b1r + b2i)

    _, _, xs_re, xs_im = lax.associative_scan(combine, (a_re, a_im, bu_re, bu_im), axis=1)
    y = (jnp.einsum('blgn,gcn->blgc', xs_re, p['s5_c_re'].astype(f32))
         - jnp.einsum('blgn,gcn->blgc', xs_im, p['s5_c_im'].astype(f32)))
    y = y.reshape(bn, length, S5_WIDTH) + p['s5_d'] * uf
    z = jax.nn.gelu(y)
    out = (z @ p['s5_glu_v']) * jax.nn.sigmoid(z @ p['s5_glu_g'])
    return out.astype(u.dtype), xs_re[:, -1], xs_im[:, -1]


def _wkv7_step(s, inp):
    r, dec, k, v, a, b = inp
    sa = jnp.einsum('bhij,bhj->bhi', s, a)
    s = s * dec[:, :, None, :] + sa[..., None] * b[:, :, None, :] + v[..., None] * k[:, :, None, :]
    return s, jnp.einsum('bhij,bhj->bhi', s, r)


def _rwkv_branch(mixed, wkv0, p):
    f32 = jnp.float32
    bn, length, _ = mixed.shape
    mf = mixed.astype(f32)
    r, k, v, wd, ad, gd = jnp.split(mf, RWKV_SPLITS, axis=-1)
    w = -jax.nn.softplus(-(p['rwkv_w0'] + jnp.tanh(wd) @ p['rwkv_w2'])) - 0.5
    decay = jnp.exp(-jnp.exp(w))
    a = jax.nn.sigmoid(p['rwkv_a0'] + ad @ p['rwkv_a2'])
    g = jax.nn.sigmoid(gd) @ p['rwkv_g2']
    hs = (bn, length, RWKV_HEADS, RWKV_HEAD)
    kk = (k * p['rwkv_k_k']).reshape(hs)
    kk = kk / jnp.maximum(jnp.sqrt(jnp.sum(kk * kk, axis=-1, keepdims=True)), 1e-12)
    k = k * (1 + (a - 1) * p['rwkv_k_a'])
    r_h = r.reshape(hs)
    k_h = k.reshape(hs)
    v_h = v.reshape(hs)
    d_h = decay.reshape(hs)
    a_h = a.reshape(hs)
    xs = tuple(jnp.moveaxis(t, 1, 0) for t in (r_h, d_h, k_h, v_h, -kk, kk * a_h))
    s_fin, ys = lax.scan(_wkv7_step, wkv0.astype(f32), xs)
    y = jnp.moveaxis(ys, 0, 1)
    mean = jnp.mean(y, axis=-1, keepdims=True)
    var = jnp.mean(jnp.square(y - mean), axis=-1, keepdims=True)
    yn = ((y - mean) * lax.rsqrt(var + GN_EPS)).reshape(bn, length, RWKV_WIDTH)
    yn = yn * p['rwkv_ln_w'] + p['rwkv_ln_b']
    bonus = (jnp.sum(r_h * k_h * p['rwkv_r_k'], axis=-1, keepdims=True) * v_h).reshape(bn, length, RWKV_WIDTH)
    o = (yn + bonus) * g
    return (o @ p['rwkv_proj']).astype(mixed.dtype), s_fin


def _layer(x, c, s5_re0, s5_im0, wkv0, shift0, p):
    mod = (jax.nn.silu(c) @ p['w_ada'] + p['b_ada']).reshape(c.shape[0], N_MOD, 1, D_MODEL)
    sh1, sc1, gt1, sh2, sc2, gt2, sh3, sc3, gt3 = [mod[:, i] for i in range(N_MOD)]
    h = x
    n = _modulate(_rmsnorm(h, p['g_ffn1']), sh1, sc1)
    h = h + 0.5 * gt1 * _swiglu(n, p['ffn1_w1'], p['ffn1_w3'], p['ffn1_w2'])
    u = _modulate(_rmsnorm(h, p['g_mix']), sh2, sc2)
    proj = u @ p['w_in']
    s5_in = proj[..., :S5_WIDTH]
    cur = proj[..., S5_WIDTH:S5_WIDTH + N_SHIFT]
    gates = proj[..., S5_WIDTH + N_SHIFT:]
    prev = jnp.concatenate([shift0[:, None, :].astype(cur.dtype), cur[:, :-1]], axis=1)
    mixed = cur + p['mu_shift'] * (prev - cur)
    y_s5, s5_re1, s5_im1 = _s5_branch(s5_in, s5_re0, s5_im0, p)
    y_rwkv, wkv1 = _rwkv_branch(mixed, wkv0, p)
    m = jax.nn.sigmoid(gates[..., :D_MODEL]) * y_s5 + jax.nn.sigmoid(gates[..., D_MODEL:]) * y_rwkv
    h = h + gt2 * (m @ p['w_out'])
    n = _modulate(_rmsnorm(h, p['g_ffn2']), sh3, sc3)
    h = h + 0.5 * gt3 * _swiglu(n, p['ffn2_w1'], p['ffn2_w3'], p['ffn2_w2'])
    y = _rmsnorm(h, p['g_final']).astype(x.dtype)
    sd = s5_re0.dtype
    return (y, s5_re1.astype(sd), s5_im1.astype(sd), wkv1.astype(wkv0.dtype),
            cur[:, -1].astype(shift0.dtype))


def setup_inputs(seed: int = 0) -> dict:
    key = jax.random.key(seed)
    ks = iter(jax.random.split(key, 64))
    f32 = jnp.float32

    def nrm(shape, s):
        return jax.random.normal(next(ks), shape, f32) * s

    def uni(shape, lo, hi):
        return jax.random.uniform(next(ks), shape, f32, lo, hi)

    d = D_MODEL
    g, n = S5_GROUPS, S5_STATE
    return {
        'x_prompt': nrm((BATCH, SEQ, d), 1.0),
        'x_sample': nrm((DEC_BATCH, DEC_SEQ, d), 1.0),
        'c_prompt': nrm((BATCH, d), 1.0),
        'c_sample': nrm((DEC_BATCH, d), 1.0),
        'state_s5_re': nrm((DEC_BATCH, g, n), 0.3),
        'state_s5_im': nrm((DEC_BATCH, g, n), 0.3),
        'state_wkv': nrm((DEC_BATCH, RWKV_HEADS, RWKV_HEAD, RWKV_HEAD), 0.3),
        'state_shift': nrm((DEC_BATCH, N_SHIFT), 1.0),
        'w_ada': nrm((d, N_MOD * d), 0.3 * d ** -0.5),
        'b_ada': nrm((N_MOD * d,), 0.01),
        'g_ffn1': 1.0 + nrm((d,), 0.02),
        'g_mix': 1.0 + nrm((d,), 0.02),
        'g_ffn2': 1.0 + nrm((d,), 0.02),
        'g_final': 1.0 + nrm((d,), 0.02),
        'ffn1_w1': nrm((d, D_FF), d ** -0.5),
        'ffn1_w3': nrm((d, D_FF), d ** -0.5),
        'ffn1_w2': nrm((D_FF, d), D_FF ** -0.5),
        'ffn2_w1': nrm((d, D_FF), d ** -0.5),
        'ffn2_w3': nrm((d, D_FF), d ** -0.5),
        'ffn2_w2': nrm((D_FF, d), D_FF ** -0.5),
        'w_in': nrm((d, N_IN), d ** -0.5),
        'mu_shift': uni((N_SHIFT,), 0.0, 1.0),
        's5_lam_re': -0.5 + nrm((g, n), 0.005),
        's5_lam_im': jnp.pi * jnp.arange(n, dtype=f32)[None, :] + nrm((g, n), 0.01),
        's5_log_dt': uni((g,), math.log(DT_MIN), math.log(DT_MAX)),
        's5_b_re': nrm((g, n, S5_GROUP), (2 * S5_GROUP) ** -0.5),
        's5_b_im': nrm((g, n, S5_GROUP), (2 * S5_GROUP) ** -0.5),
        's5_c_re': nrm((g, S5_GROUP, n), n ** -0.5),
        's5_c_im': nrm((g, S5_GROUP, n), n ** -0.5),
        's5_d': nrm((S5_WIDTH,), 1.0),
        's5_glu_v': nrm((S5_WIDTH, d), S5_WIDTH ** -0.5),
        's5_glu_g': nrm((S5_WIDTH, d), S5_WIDTH ** -0.5),
        'rwkv_w0': uni((RWKV_WIDTH,), -6.0, 1.0),
        'rwkv_w2': nrm((W_LORA, RWKV_WIDTH), 0.1 * W_LORA ** -0.5),
        'rwkv_a0': nrm((RWKV_WIDTH,), 0.1),
        'rwkv_a2': nrm((A_LORA, RWKV_WIDTH), 0.1 * A_LORA ** -0.5),
        'rwkv_g2': nrm((G_LORA, RWKV_WIDTH), G_LORA ** -0.5),
        'rwkv_k_k': 0.85 + nrm((RWKV_WIDTH,), 0.02),
        'rwkv_k_a': 1.0 + nrm((RWKV_WIDTH,), 0.02),
        'rwkv_r_k': nrm((RWKV_HEADS, RWKV_HEAD), 0.1),
        'rwkv_ln_w': 1.0 + nrm((RWKV_WIDTH,), 0.02),
        'rwkv_ln_b': nrm((RWKV_WIDTH,), 0.01),
        'rwkv_proj': nrm((RWKV_WIDTH, d), RWKV_WIDTH ** -0.5),
        'w_out': nrm((d, d), d ** -0.5),
    }


def reference(x_prompt, x_sample, c_prompt, c_sample, state_s5_re, state_s5_im, state_wkv, state_shift,
              w_ada, b_ada, g_ffn1, g_mix, g_ffn2, g_final,
              ffn1_w1, ffn1_w3, ffn1_w2, ffn2_w1, ffn2_w3, ffn2_w2,
              w_in, mu_shift,
              s5_lam_re, s5_lam_im, s5_log_dt, s5_b_re, s5_b_im, s5_c_re, s5_c_im, s5_d, s5_glu_v, s5_glu_g,
              rwkv_w0, rwkv_w2, rwkv_a0, rwkv_a2, rwkv_g2, rwkv_k_k, rwkv_k_a, rwkv_r_k,
              rwkv_ln_w, rwkv_ln_b, rwkv_proj, w_out):
    p = dict(w_ada=w_ada, b_ada=b_ada, g_ffn1=g_ffn1, g_mix=g_mix, g_ffn2=g_ffn2, g_final=g_final,
             ffn1_w1=ffn1_w1, ffn1_w3=ffn1_w3, ffn1_w2=ffn1_w2,
             ffn2_w1=ffn2_w1, ffn2_w3=ffn2_w3, ffn2_w2=ffn2_w2,
             w_in=w_in, mu_shift=mu_shift,
             s5_lam_re=s5_lam_re, s5_lam_im=s5_lam_im, s5_log_dt=s5_log_dt,
             s5_b_re=s5_b_re, s5_b_im=s5_b_im, s5_c_re=s5_c_re, s5_c_im=s5_c_im,
             s5_d=s5_d, s5_glu_v=s5_glu_v, s5_glu_g=s5_glu_g,
             rwkv_w0=rwkv_w0, rwkv_w2=rwkv_w2, rwkv_a0=rwkv_a0, rwkv_a2=rwkv_a2, rwkv_g2=rwkv_g2,
             rwkv_k_k=rwkv_k_k, rwkv_k_a=rwkv_k_a, rwkv_r_k=rwkv_r_k,
             rwkv_ln_w=rwkv_ln_w, rwkv_ln_b=rwkv_ln_b, rwkv_proj=rwkv_proj, w_out=w_out)
    bp = x_prompt.shape[0]
    z_s5 = jnp.zeros((bp, S5_GROUPS, S5_STATE), state_s5_re.dtype)
    z_wkv = jnp.zeros((bp, RWKV_HEADS, RWKV_HEAD, RWKV_HEAD), state_wkv.dtype)
    z_shift = jnp.zeros((bp, N_SHIFT), state_shift.dtype)
    y_prompt, s5_re_p, s5_im_p, wkv_p, shift_p = _layer(x_prompt, c_prompt, z_s5, z_s5, z_wkv, z_shift, p)
    y_sample, s5_re_s, s5_im_s, wkv_s, shift_s = _layer(x_sample, c_sample, state_s5_re, state_s5_im,
                                                        state_wkv, state_shift, p)
    return (y_prompt, y_sample, s5_re_p, s5_im_p, wkv_p, shift_p, s5_re_s, s5_im_s, wkv_s, shift_s)
```

```python
import functools
import math

import jax
import jax.numpy as jnp
from jax import lax
from jax.experimental import pallas as pl
from jax.experimental.pallas import tpu as pltpu

F32 = jnp.float32
BF16 = jnp.bfloat16

NORM_EPS = 1e-6
GN_EPS = 64e-5
N_MOD = 9

V7X_LANES = 128
V7X_SUBLANES = 8
V7X_MXU_DIM = 256
V7X_VMEM_BYTES = 64 * 1024 * 1024
VMEM_CAP_BYTES = V7X_VMEM_BYTES - 8 * 1024 * 1024

WKV_CHUNK = 64


def _cparams(semantics, vmem_bytes):
    return pltpu.CompilerParams(
        dimension_semantics=semantics,
        vmem_limit_bytes=int(min(vmem_bytes, VMEM_CAP_BYTES)),
    )


def _const_spec(shape):
    nd = len(shape)
    return pl.BlockSpec(shape, lambda *_: (0,) * nd, pipeline_mode=pl.Buffered(1))


def _dot(a, b):
    return jnp.dot(a, b, preferred_element_type=F32)


def _dot_nt(a, b):
    return lax.dot_general(a, b, (((1,), (1,)), ((), ())), preferred_element_type=F32)


def _dot_tn(a, b):
    return lax.dot_general(a, b, (((0,), (0,)), ((), ())), preferred_element_type=F32)


def _split3(x):
    hi = x.astype(BF16)
    r1 = x - hi.astype(F32)
    mid = r1.astype(BF16)
    lo = (r1 - mid.astype(F32)).astype(BF16)
    return hi, mid, lo


def _rms(x, g):
    ms = jnp.mean(x * x, axis=-1, keepdims=True)
    return x * lax.rsqrt(ms + NORM_EPS) * g


def _softplus(x):
    return jnp.maximum(x, 0.0) + jnp.log1p(jnp.exp(-jnp.abs(x)))


def _mod_kernel(c_ref, w_ref, b_ref, o_ref):
    c = c_ref[...]
    s = (c * jax.nn.sigmoid(c)).astype(BF16)
    o_ref[...] = _dot(s, w_ref[...]) + b_ref[...]


def _mod_call(c, w_ada, b_ada):
    rows, d = c.shape
    n = w_ada.shape[1]
    tn = d
    return pl.pallas_call(
        _mod_kernel,
        out_shape=jax.ShapeDtypeStruct((rows, n), F32),
        grid=(n // tn,),
        in_specs=[
            pl.BlockSpec((rows, d), lambda j: (0, 0)),
            pl.BlockSpec((d, tn), lambda j: (0, j)),
            pl.BlockSpec((1, tn), lambda j: (0, j)),
        ],
        out_specs=pl.BlockSpec((rows, tn), lambda j: (0, j)),
        compiler_params=_cparams(("parallel",), 32 * 2**20),
        name="adaln_mod",
    )(c, w_ada, b_ada.reshape(1, n))


def _ffn_core(x, sh, sc, gt, g, w1_ref, w3_ref, w2_ref, n_chunks):
    u = (_rms(x, g) * (1.0 + sc) + sh).astype(BF16)
    ck = w1_ref.shape[1] // n_chunks
    acc = None
    for c in range(n_chunks):
        a = _dot(u, w1_ref[:, c * ck:(c + 1) * ck])
        b = _dot(u, w3_ref[:, c * ck:(c + 1) * ck])
        hm = (a * jax.nn.sigmoid(a) * b).astype(BF16)
        part = _dot(hm, w2_ref[c * ck:(c + 1) * ck, :])
        acc = part if acc is None else acc + part
    return x + (0.5 * gt) * acc


def _ffn_kernel(x_ref, sh_ref, sc_ref, gt_ref, g_ref, w1_ref, w3_ref, w2_ref, o_ref, *, n_chunks):
    o_ref[...] = _ffn_core(x_ref[...], sh_ref[...], sc_ref[...], gt_ref[...], g_ref[...],
                           w1_ref, w3_ref, w2_ref, n_chunks)


def _mod_spec(mod, tiles_per_group):
    _, mrows, d = mod.shape
    return pl.BlockSpec((None, mrows, d), lambda i: (i // tiles_per_group, 0, 0))


def _ffn_chunks(d_ff):
    return 2 if d_ff % (2 * V7X_LANES) == 0 else 1


def _ffn_call(x2, sh, sc, gt, g, w1, w3, w2, *, tm, tiles_per_group):
    rows, d = x2.shape
    d_ff = w1.shape[1]
    kern = functools.partial(_ffn_kernel, n_chunks=_ffn_chunks(d_ff))
    weights = 3 * d * d_ff * 2
    tiles = 4 * tm * d * 4 + 4 * tm * d_ff * 4
    return pl.pallas_call(
        kern,
        out_shape=jax.ShapeDtypeStruct((rows, d), F32),
        grid=(rows // tm,),
        in_specs=[
            pl.BlockSpec((tm, d), lambda i: (i, 0)),
            _mod_spec(sh, tiles_per_group), _mod_spec(sc, tiles_per_group), _mod_spec(gt, tiles_per_group),
            _const_spec((1, d)),
            _const_spec(w1.shape), _const_spec(w3.shape), _const_spec(w2.shape),
        ],
        out_specs=pl.BlockSpec((tm, d), lambda i: (i, 0)),
        compiler_params=_cparams(("parallel",), weights + tiles + 8 * 2**20),
        name="ffn1",
    )(x2, sh, sc, gt, g.reshape(1, d), w1, w3, w2)


def _proj_kernel(h_ref, sh_ref, sc_ref, g_ref, win_ref, s5_ref, cur_ref, gates_ref):
    u = (_rms(h_ref[...], g_ref[...]) * (1.0 + sc_ref[...]) + sh_ref[...]).astype(BF16)
    n0 = s5_ref.shape[1]
    n1 = n0 + cur_ref.shape[1]
    s5_ref[...] = _dot(u, win_ref[:, :n0])
    cur_ref[...] = _dot(u, win_ref[:, n0:n1])
    gates_ref[...] = _dot(u, win_ref[:, n1:])


def _proj_call(h, sh, sc, g, w_in, widths, *, tm, tiles_per_group):
    rows, d = h.shape
    n_in = w_in.shape[1]
    return pl.pallas_call(
        _proj_kernel,
        out_shape=tuple(jax.ShapeDtypeStruct((rows, w), F32) for w in widths),
        grid=(rows // tm,),
        in_specs=[
            pl.BlockSpec((tm, d), lambda i: (i, 0)),
            _mod_spec(sh, tiles_per_group), _mod_spec(sc, tiles_per_group),
            _const_spec((1, d)),
            _const_spec(w_in.shape),
        ],
        out_specs=tuple(pl.BlockSpec((tm, w), lambda i: (i, 0)) for w in widths),
        compiler_params=_cparams(("parallel",), d * n_in * 2 + 3 * tm * (d + n_in) * 4 + 8 * 2**20),
        name="mix_proj",
    )(h, sh, sc, g.reshape(1, d), w_in)


def _s5_prep_kernel(lre_ref, lim_ref, ldt_ref, bre_ref, bim_ref, are_ref, aim_ref, bbre_ref, bbim_ref):
    lre = lre_ref[...]
    lim = lim_ref[...]
    dt = jnp.exp(ldt_ref[...])
    mag = jnp.exp(lre * dt)
    ang = lim * dt
    a_re = mag * jnp.cos(ang)
    a_im = mag * jnp.sin(ang)
    den = lre * lre + lim * lim
    num_re = a_re - 1.0
    k_re = (num_re * lre + a_im * lim) / den
    k_im = (a_im * lre - num_re * lim) / den
    are_ref[...] = a_re
    aim_ref[...] = a_im
    b_re = bre_ref[...]
    b_im = bim_ref[...]
    bbre_ref[...] = k_re * b_re - k_im * b_im
    bbim_ref[...] = k_re * b_im + k_im * b_re


def _s5_prep_call(lam_re, lam_im, log_dt, b_re_t, b_im_t):
    g, n = lam_re.shape
    c = b_re_t.shape[1]
    return pl.pallas_call(
        _s5_prep_kernel,
        out_shape=(jax.ShapeDtypeStruct((g, 1, n), F32), jax.ShapeDtypeStruct((g, 1, n), F32),
                   jax.ShapeDtypeStruct((g, c, n), F32), jax.ShapeDtypeStruct((g, c, n), F32)),
        name="s5_discretise",
    )(lam_re.reshape(g, 1, n), lam_im.reshape(g, 1, n), log_dt.reshape(g, 1, 1), b_re_t, b_im_t)


def _s5_input_matmul(ub, wb_ref, bure_ref, buim_ref):
    nk, kin, two_s = wb_ref.shape
    s = two_s // 2
    for k in range(nk):
        res = _dot(ub[:, k * kin:(k + 1) * kin], wb_ref[k])
        bure_ref[:, k * s:(k + 1) * s] = res[:, :s]
        buim_ref[:, k * s:(k + 1) * s] = res[:, s:]


def _s5_output(xre, xim, u, wcre_ref, wcim_ref, d_row):
    nk, s, kout = wcre_ref.shape
    xre_b = xre.astype(BF16)
    xim_b = xim.astype(BF16)
    ys = []
    for k in range(nk):
        ys.append(_dot(xre_b[:, k * s:(k + 1) * s], wcre_ref[k]) + _dot(xim_b[:, k * s:(k + 1) * s], wcim_ref[k]))
    y = jnp.concatenate(ys, axis=-1) + d_row * u
    return jax.nn.gelu(y)


def _s5_glu(zb, gv_ref, gg_ref):
    return _dot(zb, gv_ref[...]) * jax.nn.sigmoid(_dot(zb, gg_ref[...]))


def _s5_seq_kernel(u_ref, h0re_ref, h0im_ref, are_ref, aim_ref, wb_ref, wcre_ref, wcim_ref, d_ref,
                   gv_ref, gg_ref, perm_ref, permt_ref,
                   o_ref, hre_ref, him_ref,
                   bure_scr, buim_scr, xre_scr, xim_scr, *, lane_chunk):
    nb, ts, din = u_ref.shape
    rows = nb * ts

    @pl.when(pl.program_id(0) == 0)
    def _():
        hre_ref[...] = h0re_ref[...]
        him_ref[...] = h0im_ref[...]

    u_bm = u_ref[...].reshape(rows, din)
    hi, mid, lo = _split3(u_bm)
    perm = perm_ref[...]
    u_hi_tm = _dot(perm, hi)
    u_tm = u_hi_tm + _dot(perm, mid) + _dot(perm, lo)
    _s5_input_matmul(u_hi_tm.astype(BF16), wb_ref, bure_scr, buim_scr)

    ns = are_ref.shape[1]
    for lc in range(ns // lane_chunk):
        sl = slice(lc * lane_chunk, (lc + 1) * lane_chunk)
        ar = are_ref[:, sl]
        ai = aim_ref[:, sl]
        sr = hre_ref[:, sl]
        si = him_ref[:, sl]
        for t in range(ts):
            rs = slice(t * nb, (t + 1) * nb)
            nr = ar * sr - ai * si + bure_scr[rs, sl]
            ni = ar * si + ai * sr + buim_scr[rs, sl]
            xre_scr[rs, sl] = nr
            xim_scr[rs, sl] = ni
            sr, si = nr, ni
        hre_ref[:, sl] = sr
        him_ref[:, sl] = si

    z_tm = _s5_output(xre_scr[...], xim_scr[...], u_tm, wcre_ref, wcim_ref, d_ref[...])
    z_bm = _dot(permt_ref[...], z_tm.astype(BF16)).astype(BF16)
    o_ref[...] = _s5_glu(z_bm, gv_ref, gg_ref).reshape(nb, ts, o_ref.shape[2])


def _s5_rows_kernel(u_ref, h0re_ref, h0im_ref, are_ref, aim_ref, wb_ref, wcre_ref, wcim_ref, d_ref,
                    gv_ref, gg_ref, o_ref, hre_ref, him_ref, bure_scr, buim_scr):
    u = u_ref[...]
    _s5_input_matmul(u.astype(BF16), wb_ref, bure_scr, buim_scr)
    ar = are_ref[...]
    ai = aim_ref[...]
    sr = h0re_ref[...]
    si = h0im_ref[...]
    nr = ar * sr - ai * si + bure_scr[...]
    ni = ar * si + ai * sr + buim_scr[...]
    hre_ref[...] = nr
    him_ref[...] = ni
    z = _s5_output(nr, ni, u, wcre_ref, wcim_ref, d_ref[...])
    o_ref[...] = _s5_glu(z.astype(BF16), gv_ref, gg_ref)


def _s5_weight_specs(ws):
    return [_const_spec(w.shape) for w in ws]


def _s5_seq_call(s5_in3, h0re, h0im, a_re, a_im, sw, *, ts):
    nb, length, din = s5_in3.shape
    ns = a_re.shape[1]
    d = sw["glu_v"].shape[1]
    rows = nb * ts
    idx = jnp.arange(rows)
    src = (idx % nb) * ts + idx // nb
    perm = (src[:, None] == jnp.arange(rows)[None, :]).astype(BF16)
    a_re_b = jnp.broadcast_to(a_re, (nb, ns))
    a_im_b = jnp.broadcast_to(a_im, (nb, ns))
    ws = [sw["wb"], sw["wc_re"], sw["wc_im"], sw["d"], sw["glu_v"], sw["glu_g"], perm, perm.T]
    kern = functools.partial(_s5_seq_kernel, lane_chunk=4 * V7X_LANES)
    return pl.pallas_call(
        kern,
        out_shape=(jax.ShapeDtypeStruct((nb, length, d), F32),
                   jax.ShapeDtypeStruct((nb, ns), F32), jax.ShapeDtypeStruct((nb, ns), F32)),
        grid=(length // ts,),
        in_specs=[pl.BlockSpec((nb, ts, din), lambda c: (0, c, 0)),
                  _const_spec((nb, ns)), _const_spec((nb, ns)), _const_spec((nb, ns)), _const_spec((nb, ns))]
                 + _s5_weight_specs(ws),
        out_specs=(pl.BlockSpec((nb, ts, d), lambda c: (0, c, 0)),
                   pl.BlockSpec((nb, ns), lambda c: (0, 0)), pl.BlockSpec((nb, ns), lambda c: (0, 0))),
        scratch_shapes=[pltpu.VMEM((rows, ns), F32) for _ in range(4)],
        compiler_params=_cparams(("arbitrary",), 40 * 2**20),
        name="s5_seq",
    )(s5_in3, h0re, h0im, a_re_b, a_im_b, *ws)


def _s5_rows_call(s5_in, h0re, h0im, a_re, a_im, sw):
    rows, din = s5_in.shape
    ns = a_re.shape[1]
    d = sw["glu_v"].shape[1]
    ws = [sw["wb"], sw["wc_re"], sw["wc_im"], sw["d"], sw["glu_v"], sw["glu_g"]]
    full = lambda shape: pl.BlockSpec(shape, lambda i: (0,) * len(shape))
    return pl.pallas_call(
        _s5_rows_kernel,
        out_shape=(jax.ShapeDtypeStruct((rows, d), F32),
                   jax.ShapeDtypeStruct((rows, ns), F32), jax.ShapeDtypeStruct((rows, ns), F32)),
        grid=(1,),
        in_specs=[full((rows, din)), full((rows, ns)), full((rows, ns)), full((1, ns)), full((1, ns))]
                 + [full(w.shape) for w in ws],
        out_specs=(full((rows, d)), full((rows, ns)), full((rows, ns))),
        scratch_shapes=[pltpu.VMEM((rows, ns), F32) for _ in range(2)],
        compiler_params=_cparams(("arbitrary",), 40 * 2**20),
        name="s5_rows",
    )(s5_in, h0re, h0im, a_re, a_im, *ws)


def _headsum(x, ones_bd):
    hi = x.astype(BF16)
    lo = (x - hi.astype(F32)).astype(BF16)
    return _dot(hi, ones_bd) + _dot(lo, ones_bd)


def _rwkv_pre(cur, prev, p, ones_bd, width):
    mixed = cur + p["mu"] * (prev - cur)
    r = mixed[:, 0:width]
    k = mixed[:, width:2 * width]
    v = mixed[:, 2 * width:3 * width]
    wa = mixed[:, 3 * width:3 * width + p["w2p"].shape[0]]
    gd = mixed[:, 3 * width + p["w2p"].shape[0]:]
    lw = _dot(jnp.tanh(wa).astype(BF16), p["w2p"])
    la = _dot(wa.astype(BF16), p["a2p"])
    w = -_softplus(-(p["w0"] + lw)) - 0.5
    ld = -jnp.exp(w)
    a = jax.nn.sigmoid(p["a0"] + la)
    g = _dot(jax.nn.sigmoid(gd).astype(BF16), p["g2"])
    kk = k * p["k_k"]
    norm = jnp.sqrt(_headsum(kk * kk, ones_bd))
    kk = kk / jnp.maximum(norm, 1e-12)
    k2 = k * (1.0 + (a - 1.0) * p["k_a"])
    return r, ld, k2, v, -kk, kk * a, g


def _rwkv_post(y, r, k2, v, g, p, ones_bd, proj, head):
    inv = 1.0 / head
    mean = _headsum(y, ones_bd) * inv
    dlt = y - mean
    var = _headsum(dlt * dlt, ones_bd) * inv
    yn = dlt * lax.rsqrt(var + GN_EPS) * p["ln_w"] + p["ln_b"]
    bonus = _headsum(r * k2 * p["r_k"], ones_bd) * v
    o = ((yn + bonus) * g).astype(BF16)
    return _dot(o, proj)


def _solve_unit_lower(l_mat, w, size):
    x = w
    pw = l_mat
    steps = int(math.log2(size))
    for i in range(steps):
        x = x + _dot(pw.astype(BF16), x.astype(BF16))
        if i + 1 < steps:
            pwb = pw.astype(BF16)
            pw = _dot(pwb, pwb)
    return x


def _wkv_chunk(r, ld, k, v, a, b, s_ref, y_ref, row0, tril, strict, incl, head):
    c, width = r.shape
    hi, mid, lo = _split3(ld)
    cum = _dot(tril, hi) + _dot(tril, mid) + _dot(tril, lo)
    g_inc = jnp.exp(cum)
    g_inv = jnp.exp(-cum)
    a_t = a * jnp.exp(cum - ld)
    r_t = r * g_inc
    b_t = b * g_inv
    k_t = k * g_inv
    cum_end = cum[c - 1:c, :]
    g_end = jnp.exp(cum_end)
    g_rel = jnp.exp(cum_end - cum)
    b_e = b * g_rel
    k_e = k * g_rel
    for h in range(width // head):
        sl = slice(h * head, (h + 1) * head)
        ar = jnp.concatenate([a_t[:, sl], r_t[:, sl]], axis=0).astype(BF16)
        bk = jnp.concatenate([b_t[:, sl], k_t[:, sl]], axis=0).astype(BF16)
        sc = _dot_nt(ar, bk)
        s0 = s_ref[h]
        am = _dot_nt(ar, s0.astype(BF16))
        vh = v[:, sl].astype(BF16)
        l_ab = jnp.where(strict, sc[:c, :c], 0.0)
        l_ak = jnp.where(strict, sc[:c, c:], 0.0)
        u = _solve_unit_lower(l_ab, am[:c] + _dot(l_ak.astype(BF16), vh), c)
        ub = u.astype(BF16)
        m_rb = jnp.where(incl, sc[c:, :c], 0.0).astype(BF16)
        m_rk = jnp.where(incl, sc[c:, c:], 0.0).astype(BF16)
        y_ref[row0:row0 + c, sl] = am[c:] + _dot(m_rb, ub) + _dot(m_rk, vh)
        s_ref[h] = (s0 * g_end[:, sl] + _dot_tn(ub, b_e[:, sl].astype(BF16))
                    + _dot_tn(vh, k_e[:, sl].astype(BF16)))


_RWKV_PARAM_NAMES = ("mu", "w0", "w2p", "a0", "a2p", "g2", "k_k", "k_a", "r_k", "ln_w", "ln_b")


def _rwkv_seq_kernel(cur_ref, shift0_ref, s0_ref, *rest, chunk, head):
    np_ = len(_RWKV_PARAM_NAMES)
    p = {n: ref[...] for n, ref in zip(_RWKV_PARAM_NAMES, rest[:np_])}
    proj_ref, ones_ref, tril_ref = rest[np_:np_ + 3]
    out_ref, st_ref = rest[np_ + 3:np_ + 5]
    carry_scr, y_scr = rest[np_ + 5:]

    @pl.when(pl.program_id(1) == 0)
    def _():
        carry_scr[...] = shift0_ref[...]
        st_ref[...] = s0_ref[...]

    cur = cur_ref[...]
    rows, _ = cur.shape
    width = y_scr.shape[1]
    row = lax.broadcasted_iota(jnp.int32, cur.shape, 0)
    prev = jnp.where(row == 0, carry_scr[...], pltpu.roll(cur, 1, axis=0))
    carry_scr[...] = cur[rows - 1:rows, :]

    ones_bd = ones_ref[...]
    r, ld, k2, v, a, b, g = _rwkv_pre(cur, prev, p, ones_bd, width)

    ri = lax.broadcasted_iota(jnp.int32, (chunk, chunk), 0)
    ci = lax.broadcasted_iota(jnp.int32, (chunk, chunk), 1)
    strict = ri > ci
    incl = ri >= ci
    tril = tril_ref[...]
    for s in range(rows // chunk):
        rs = slice(s * chunk, (s + 1) * chunk)
        _wkv_chunk(r[rs], ld[rs], k2[rs], v[rs], a[rs], b[rs], st_ref, y_scr, s * chunk,
                   tril, strict, incl, head)

    out_ref[...] = _rwkv_post(y_scr[...], r, k2, v, g, p, ones_bd, proj_ref[...], head)


def _rwkv_rows_pre_kernel(cur_ref, prev_ref, *rest, width):
    np_ = len(_RWKV_PARAM_NAMES)
    p = {n: ref[...] for n, ref in zip(_RWKV_PARAM_NAMES, rest[:np_])}
    ones_ref = rest[np_]
    outs = rest[np_ + 1:]
    vals = _rwkv_pre(cur_ref[...], prev_ref[...], p, ones_ref[...], width)
    for o_ref, val in zip(outs, vals):
        o_ref[...] = val


def _wkv_step_kernel(r_ref, ld_ref, k_ref, v_ref, a_ref, b_ref, s_ref, y_ref, so_ref, *, head):
    nb, nh = s_ref.shape[0], s_ref.shape[1]
    eye = (lax.broadcasted_iota(jnp.int32, (head, head), 0)
           == lax.broadcasted_iota(jnp.int32, (head, head), 1)).astype(F32)
    dec = jnp.exp(ld_ref[...])
    for i in range(nb):
        for h in range(nh):
            sl = slice(h * head, (h + 1) * head)
            s = s_ref[i, h]
            sa = jnp.sum(s * a_ref[i:i + 1, sl], axis=-1, keepdims=True)
            v_col = jnp.sum(eye * v_ref[i:i + 1, sl], axis=-1, keepdims=True)
            s2 = s * dec[i:i + 1, sl] + sa * b_ref[i:i + 1, sl] + v_col * k_ref[i:i + 1, sl]
            y_col = jnp.sum(s2 * r_ref[i:i + 1, sl], axis=-1, keepdims=True)
            y_ref[i:i + 1, sl] = jnp.sum(eye * y_col, axis=0, keepdims=True)
            so_ref[i, h] = s2


def _rwkv_rows_post_kernel(y_ref, r_ref, k_ref, v_ref, g_ref, *rest, head):
    np_ = len(_RWKV_PARAM_NAMES)
    p = {n: ref[...] for n, ref in zip(_RWKV_PARAM_NAMES, rest[:np_])}
    proj_ref, ones_ref, o_ref = rest[np_:]
    o_ref[...] = _rwkv_post(y_ref[...], r_ref[...], k_ref[...], v_ref[...], g_ref[...], p,
                            ones_ref[...], proj_ref[...], head)


def _rwkv_param_list(rw):
    return [rw[n] for n in _RWKV_PARAM_NAMES]


def _rwkv_seq_call(cur, shift0, s0, rw, *, nb, length, tr):
    n_shift = cur.shape[1]
    nh, head = s0.shape[1], s0.shape[2]
    width = nh * head
    d = rw["proj"].shape[1]
    params = _rwkv_param_list(rw)
    chunk = min(WKV_CHUNK, tr)
    tril = (jnp.arange(chunk)[:, None] >= jnp.arange(chunk)[None, :]).astype(BF16)
    consts = params + [rw["proj"], rw["ones_bd"], tril]
    tiles = length // tr
    kern = functools.partial(_rwkv_seq_kernel, chunk=chunk, head=head)
    return pl.pallas_call(
        kern,
        out_shape=(jax.ShapeDtypeStruct((nb * length, d), F32),
                   jax.ShapeDtypeStruct((nb, nh, head, head), F32)),
        grid=(nb, tiles),
        in_specs=[pl.BlockSpec((tr, n_shift), lambda b, i: (b * tiles + i, 0)),
                  pl.BlockSpec((None, 1, n_shift), lambda b, i: (b, 0, 0)),
                  pl.BlockSpec((None, nh, head, head), lambda b, i: (b, 0, 0, 0))]
                 + [pl.BlockSpec(c.shape, lambda b, i, nd=c.ndim: (0,) * nd) for c in consts],
        out_specs=(pl.BlockSpec((tr, d), lambda b, i: (b * tiles + i, 0)),
                   pl.BlockSpec((None, nh, head, head), lambda b, i: (b, 0, 0, 0))),
        scratch_shapes=[pltpu.VMEM((1, n_shift), F32), pltpu.VMEM((tr, width), F32)],
        compiler_params=_cparams(("arbitrary", "arbitrary"), 40 * 2**20),
        name="rwkv_seq",
    )(cur, shift0.reshape(nb, 1, n_shift), s0, *consts)


def _rwkv_rows_call(cur, prev, s0, rw, *, seq_block):
    rows, n_shift = cur.shape
    nh, head = s0.shape[1], s0.shape[2]
    width = nh * head
    d = rw["proj"].shape[1]
    params = _rwkv_param_list(rw)
    full = lambda shape: pl.BlockSpec(shape, lambda i: (0,) * len(shape))
    vec = jax.ShapeDtypeStruct((rows, width), F32)

    pre_consts = params + [rw["ones_bd"]]
    r, ld, k2, v, a, b, g = pl.pallas_call(
        functools.partial(_rwkv_rows_pre_kernel, width=width),
        out_shape=(vec,) * 7,
        grid=(1,),
        in_specs=[full(cur.shape), full(prev.shape)] + [full(c.shape) for c in pre_consts],
        out_specs=(full((rows, width)),) * 7,
        compiler_params=_cparams(("arbitrary",), 32 * 2**20),
        name="rwkv_rows_pre",
    )(cur, prev, *pre_consts)

    blk = pl.BlockSpec((seq_block, width), lambda i: (i, 0))
    sblk = pl.BlockSpec((seq_block, nh, head, head), lambda i: (i, 0, 0, 0))
    y, s1 = pl.pallas_call(
        functools.partial(_wkv_step_kernel, head=head),
        out_shape=(vec, jax.ShapeDtypeStruct(s0.shape, F32)),
        grid=(rows // seq_block,),
        in_specs=[blk] * 6 + [sblk],
        out_specs=(blk, sblk),
        compiler_params=_cparams(("parallel",), 32 * 2**20),
        name="wkv_step",
    )(r, ld, k2, v, a, b, s0)

    post_consts = params + [rw["proj"], rw["ones_bd"]]
    out = pl.pallas_call(
        functools.partial(_rwkv_rows_post_kernel, head=head),
        out_shape=jax.ShapeDtypeStruct((rows, d), F32),
        grid=(1,),
        in_specs=[full((rows, width))] * 5 + [full(c.shape) for c in post_consts],
        out_specs=full((rows, d)),
        compiler_params=_cparams(("arbitrary",), 32 * 2**20),
        name="rwkv_rows_post",
    )(y, r, k2, v, g, *post_consts)
    return out, s1


def _merge_kernel(h_ref, gates_ref, ys5_ref, yrw_ref, gt2_ref, wout_ref,
                  sh_ref, sc_ref, gt_ref, g_ref, w1_ref, w3_ref, w2_ref, gfin_ref, o_ref, *, n_chunks):
    d = h_ref.shape[1]
    gates = gates_ref[...]
    m = jax.nn.sigmoid(gates[:, :d]) * ys5_ref[...] + jax.nn.sigmoid(gates[:, d:]) * yrw_ref[...]
    h2 = h_ref[...] + gt2_ref[...] * _dot(m.astype(BF16), wout_ref[...])
    h3 = _ffn_core(h2, sh_ref[...], sc_ref[...], gt_ref[...], g_ref[...], w1_ref, w3_ref, w2_ref, n_chunks)
    o_ref[...] = _rms(h3, gfin_ref[...])


def _merge_call(h, gates, ys5, yrw, gt2, w_out, sh, sc, gt, g, w1, w3, w2, g_fin, *, tm, tiles_per_group):
    rows, d = h.shape
    d_ff = w1.shape[1]
    kern = functools.partial(_merge_kernel, n_chunks=_ffn_chunks(d_ff))
    row_spec = lambda w: pl.BlockSpec((tm, w), lambda i: (i, 0))
    weights = (3 * d * d_ff + d * d) * 2
    tiles = 2 * tm * 6 * d * 4 + 4 * tm * d_ff * 4 + 6 * tm * d * 4
    return pl.pallas_call(
        kern,
        out_shape=jax.ShapeDtypeStruct((rows, d), F32),
        grid=(rows // tm,),
        in_specs=[row_spec(d), row_spec(2 * d), row_spec(d), row_spec(d),
                  _mod_spec(gt2, tiles_per_group), _const_spec(w_out.shape),
                  _mod_spec(sh, tiles_per_group), _mod_spec(sc, tiles_per_group), _mod_spec(gt, tiles_per_group),
                  _const_spec((1, d)), _const_spec(w1.shape), _const_spec(w3.shape), _const_spec(w2.shape),
                  _const_spec((1, d))],
        out_specs=row_spec(d),
        compiler_params=_cparams(("parallel",), weights + tiles + 8 * 2**20),
        name="merge_ffn2",
    )(h, gates, ys5, yrw, gt2, w_out, sh, sc, gt, g.reshape(1, d), w1, w3, w2, g_fin.reshape(1, d))


def _pick_tile(n, target):
    t = min(n, target)
    while n % t:
        t -= V7X_SUBLANES
    return t


def _block_diag_rows(blk):
    nk, g, r, c = blk.shape
    eye = jnp.eye(g, dtype=blk.dtype)
    return (blk[:, :, :, None, :] * eye[None, :, None, :, None]).reshape(nk, g * r, g * c)


def _s5_weights(a_shape, bb_re, bb_im, c_re, c_im, s5_d, glu_v, glu_g):
    g, c, n = bb_re.shape
    gpb = max(1, min(g, V7X_MXU_DIM // c))
    nk = g // gpb
    wb = jnp.concatenate([_block_diag_rows(bb_re.reshape(nk, gpb, c, n)),
                          _block_diag_rows(bb_im.reshape(nk, gpb, c, n))], axis=-1).astype(BF16)
    to_out = lambda w: _block_diag_rows(jnp.swapaxes(w, 1, 2).reshape(nk, gpb, n, c)).astype(BF16)
    return dict(wb=wb, wc_re=to_out(c_re), wc_im=to_out(-c_im), d=s5_d.reshape(1, g * c),
                glu_v=glu_v.astype(BF16), glu_g=glu_g.astype(BF16))


def _layer(x, c_mod, s5_re0, s5_im0, wkv0, shift0, w, *, sequential):
    nb, length, d = x.shape
    rows = nb * length
    g, n = s5_re0.shape[1], s5_re0.shape[2]
    ns = g * n
    n_shift = shift0.shape[1]
    s5_width = w["s5"]["d"].shape[1]
    widths = (s5_width, n_shift, 2 * d)

    if sequential:
        tm = _pick_tile(length, 512)
        tpg = length // tm
        mods = [c_mod[:, i].reshape(nb, 1, d) for i in range(N_MOD)]
    else:
        tm = _pick_tile(rows, 512)
        tpg = 1
        mods = [c_mod[:, i].reshape(rows // tm, tm, d) for i in range(N_MOD)]
    sh1, sc1, gt1, sh2, sc2, gt2, sh3, sc3, gt3 = mods

    x2 = x.reshape(rows, d)
    h1 = _ffn_call(x2, sh1, sc1, gt1, w["g_ffn1"], *w["ffn1"], tm=tm, tiles_per_group=tpg)
    s5_in, cur, gates = _proj_call(h1, sh2, sc2, w["g_mix"], w["w_in"], widths, tm=tm, tiles_per_group=tpg)

    h0re = s5_re0.reshape(nb, ns)
    h0im = s5_im0.reshape(nb, ns)
    if sequential:
        ts = V7X_MXU_DIM // nb
        y_s5, hre, him = _s5_seq_call(s5_in.reshape(nb, length, s5_width), h0re, h0im,
                                      w["a_re"], w["a_im"], w["s5"], ts=ts)
        y_s5 = y_s5.reshape(rows, d)
        tr = _pick_tile(length, 2 * WKV_CHUNK)
        y_rw, wkv1 = _rwkv_seq_call(cur, shift0, wkv0, w["rwkv"], nb=nb, length=length, tr=tr)
    else:
        y_s5, hre, him = _s5_rows_call(s5_in, h0re, h0im, w["a_re"], w["a_im"], w["s5"])
        y_rw, wkv1 = _rwkv_rows_call(cur, shift0, wkv0, w["rwkv"], seq_block=V7X_SUBLANES)

    tm2 = _pick_tile(tm, 256)
    tpg2 = tpg * (tm // tm2) if sequential else 1
    if not sequential:
        mods2 = [c_mod[:, i].reshape(rows // tm2, tm2, d) for i in (5, 6, 7, 8)]
    else:
        mods2 = [gt2, sh3, sc3, gt3]
    y = _merge_call(h1, gates, y_s5, y_rw, mods2[0], w["w_out"], mods2[1], mods2[2], mods2[3],
                    w["g_ffn2"], *w["ffn2"], w["g_final"], tm=tm2, tiles_per_group=tpg2)

    shift1 = cur.reshape(nb, length, n_shift)[:, -1]
    return (y.reshape(nb, length, d), hre.reshape(nb, g, n), him.reshape(nb, g, n), wkv1, shift1)


def kernel(x_prompt, x_sample, c_prompt, c_sample, state_s5_re, state_s5_im, state_wkv, state_shift, w_ada, b_ada, g_ffn1, g_mix, g_ffn2, g_final, ffn1_w1, ffn1_w3, ffn1_w2, ffn2_w1, ffn2_w3, ffn2_w2, w_in, mu_shift, s5_lam_re, s5_lam_im, s5_log_dt, s5_b_re, s5_b_im, s5_c_re, s5_c_im, s5_d, s5_glu_v, s5_glu_g, rwkv_w0, rwkv_w2, rwkv_a0, rwkv_a2, rwkv_g2, rwkv_k_k, rwkv_k_a, rwkv_r_k, rwkv_ln_w, rwkv_ln_b, rwkv_proj, w_out):
    bp, _, d = x_prompt.shape
    bs = x_sample.shape[0]
    g, n = s5_lam_re.shape
    nh, head = rwkv_r_k.shape
    width = nh * head
    n_shift = mu_shift.shape[0]
    bf = lambda t: t.astype(BF16)
    row = lambda t: t.reshape(1, -1)

    a_re, a_im, bb_re, bb_im = _s5_prep_call(s5_lam_re, s5_lam_im, s5_log_dt,
                                             jnp.swapaxes(s5_b_re, 1, 2), jnp.swapaxes(s5_b_im, 1, 2))
    w_lora = rwkv_w2.shape[0]
    a_lora = rwkv_a2.shape[0]
    w2p = jnp.concatenate([rwkv_w2, jnp.zeros((a_lora, width), F32)], axis=0)
    a2p = jnp.concatenate([jnp.zeros((w_lora, width), F32), rwkv_a2], axis=0)
    rw = dict(mu=row(mu_shift), w0=row(rwkv_w0), w2p=bf(w2p), a0=row(rwkv_a0), a2p=bf(a2p), g2=bf(rwkv_g2),
              k_k=row(rwkv_k_k), k_a=row(rwkv_k_a), r_k=row(rwkv_r_k), ln_w=row(rwkv_ln_w), ln_b=row(rwkv_ln_b),
              proj=bf(rwkv_proj),
              ones_bd=jnp.kron(jnp.eye(nh, dtype=F32), jnp.ones((head, head), F32)).astype(BF16))
    w = dict(
        g_ffn1=g_ffn1, g_mix=g_mix, g_ffn2=g_ffn2, g_final=g_final,
        ffn1=(bf(ffn1_w1), bf(ffn1_w3), bf(ffn1_w2)), ffn2=(bf(ffn2_w1), bf(ffn2_w3), bf(ffn2_w2)),
        w_in=bf(w_in), w_out=bf(w_out),
        a_re=a_re.reshape(1, g * n), a_im=a_im.reshape(1, g * n),
        s5=_s5_weights(None, bb_re, bb_im, s5_c_re, s5_c_im, s5_d, s5_glu_v, s5_glu_g),
        rwkv=rw,
    )

    mod = _mod_call(jnp.concatenate([c_prompt, c_sample], axis=0), bf(w_ada), b_ada)
    mod_p = mod[:bp].reshape(bp, N_MOD, d)
    mod_s = mod[bp:].reshape(bs, N_MOD, d)

    z_s5 = jnp.zeros((bp, g, n), state_s5_re.dtype)
    z_wkv = jnp.zeros((bp, nh, head, head), state_wkv.dtype)
    z_shift = jnp.zeros((bp, n_shift), state_shift.dtype)
    y_p, s5re_p, s5im_p, wkv_p, shift_p = _layer(x_prompt, mod_p, z_s5, z_s5, z_wkv, z_shift, w, sequential=True)
    y_s, s5re_s, s5im_s, wkv_s, shift_s = _layer(x_sample, mod_s, state_s5_re, state_s5_im, state_wkv,
                                                 state_shift, w, sequential=False)
    return (y_p, y_s, s5re_p, s5im_p, wkv_p, shift_p, s5re_s, s5im_s, wkv_s, shift_s)
```

```python
import functools
import math

import jax
import jax.numpy as jnp
from jax import lax
from jax.experimental import pallas as pl
from jax.experimental.pallas import tpu as pltpu

F32 = jnp.float32
BF16 = jnp.bfloat16

NORM_EPS = 1e-6
GN_EPS = 64e-5
N_MOD = 9

V7X_LANES = 128
V7X_SUBLANES = 8
V7X_MXU_DIM = 256
V7X_VMEM_BYTES = 64 * 1024 * 1024
VMEM_CAP_BYTES = V7X_VMEM_BYTES - 8 * 1024 * 1024

WKV_CHUNK = 64


def _cparams(semantics, vmem_bytes):
    return pltpu.CompilerParams(
        dimension_semantics=semantics,
        vmem_limit_bytes=int(min(vmem_bytes, VMEM_CAP_BYTES)),
    )


def _const_spec(shape):
    nd = len(shape)
    return pl.BlockSpec(shape, lambda *_: (0,) * nd, pipeline_mode=pl.Buffered(1))


def _dot(a, b):
    return jnp.dot(a, b, preferred_element_type=F32)


def _dot_nt(a, b):
    return lax.dot_general(a, b, (((1,), (1,)), ((), ())), preferred_element_type=F32)


def _dot_tn(a, b):
    return lax.dot_general(a, b, (((0,), (0,)), ((), ())), preferred_element_type=F32)


def _split3(x):
    hi = x.astype(BF16)
    r1 = x - hi.astype(F32)
    mid = r1.astype(BF16)
    lo = (r1 - mid.astype(F32)).astype(BF16)
    return hi, mid, lo


def _rms(x, g):
    ms = jnp.mean(x * x, axis=-1, keepdims=True)
    return x * lax.rsqrt(ms + NORM_EPS) * g


def _softplus(x):
    return jnp.maximum(x, 0.0) + jnp.log1p(jnp.exp(-jnp.abs(x)))


def _mod_kernel(c_ref, w_ref, b_ref, o_ref):
    c = c_ref[...]
    s = (c * jax.nn.sigmoid(c)).astype(BF16)
    o_ref[...] = _dot(s, w_ref[...]) + b_ref[...]


def _mod_call(c, w_ada, b_ada):
    rows, d = c.shape
    n = w_ada.shape[1]
    tn = d
    return pl.pallas_call(
        _mod_kernel,
        out_shape=jax.ShapeDtypeStruct((rows, n), F32),
        grid=(n // tn,),
        in_specs=[
            pl.BlockSpec((rows, d), lambda j: (0, 0)),
            pl.BlockSpec((d, tn), lambda j: (0, j)),
            pl.BlockSpec((1, tn), lambda j: (0, j)),
        ],
        out_specs=pl.BlockSpec((rows, tn), lambda j: (0, j)),
        compiler_params=_cparams(("parallel",), 32 * 2**20),
        name="adaln_mod",
    )(c, w_ada, b_ada.reshape(1, n))


def _ffn_core(x, sh, sc, gt, g, w1_ref, w3_ref, w2_ref, n_chunks):
    u = (_rms(x, g) * (1.0 + sc) + sh).astype(BF16)
    ck = w1_ref.shape[1] // n_chunks
    acc = None
    for c in range(n_chunks):
        a = _dot(u, w1_ref[:, c * ck:(c + 1) * ck])
        b = _dot(u, w3_ref[:, c * ck:(c + 1) * ck])
        hm = (a * jax.nn.sigmoid(a) * b).astype(BF16)
        part = _dot(hm, w2_ref[c * ck:(c + 1) * ck, :])
        acc = part if acc is None else acc + part
    return x + (0.5 * gt) * acc


def _ffn_kernel(x_ref, sh_ref, sc_ref, gt_ref, g_ref, w1_ref, w3_ref, w2_ref, o_ref, *, n_chunks):
    o_ref[...] = _ffn_core(x_ref[...], sh_ref[...], sc_ref[...], gt_ref[...], g_ref[...],
                           w1_ref, w3_ref, w2_ref, n_chunks)


def _mod_spec(mod, tiles_per_group):
    _, mrows, d = mod.shape
    return pl.BlockSpec((None, mrows, d), lambda i: (i // tiles_per_group, 0, 0))


def _ffn_chunks(d_ff):
    return 2 if d_ff % (2 * V7X_LANES) == 0 else 1


def _ffn_call(x2, sh, sc, gt, g, w1, w3, w2, *, tm, tiles_per_group):
    rows, d = x2.shape
    d_ff = w1.shape[1]
    kern = functools.partial(_ffn_kernel, n_chunks=_ffn_chunks(d_ff))
    weights = 3 * d * d_ff * 2
    tiles = 4 * tm * d * 4 + 4 * tm * d_ff * 4
    return pl.pallas_call(
        kern,
        out_shape=jax.ShapeDtypeStruct((rows, d), F32),
        grid=(rows // tm,),
        in_specs=[
            pl.BlockSpec((tm, d), lambda i: (i, 0)),
            _mod_spec(sh, tiles_per_group), _mod_spec(sc, tiles_per_group), _mod_spec(gt, tiles_per_group),
            _const_spec((1, d)),
            _const_spec(w1.shape), _const_spec(w3.shape), _const_spec(w2.shape),
        ],
        out_specs=pl.BlockSpec((tm, d), lambda i: (i, 0)),
        compiler_params=_cparams(("parallel",), weights + tiles + 8 * 2**20),
        name="ffn1",
    )(x2, sh, sc, gt, g.reshape(1, d), w1, w3, w2)


def _proj_kernel(h_ref, sh_ref, sc_ref, g_ref, win_ref, s5_ref, cur_ref, gates_ref):
    u = (_rms(h_ref[...], g_ref[...]) * (1.0 + sc_ref[...]) + sh_ref[...]).astype(BF16)
    n0 = s5_ref.shape[1]
    n1 = n0 + cur_ref.shape[1]
    s5_ref[...] = _dot(u, win_ref[:, :n0])
    cur_ref[...] = _dot(u, win_ref[:, n0:n1])
    gates_ref[...] = _dot(u, win_ref[:, n1:])


def _proj_call(h, sh, sc, g, w_in, widths, *, tm, tiles_per_group):
    rows, d = h.shape
    n_in = w_in.shape[1]
    return pl.pallas_call(
        _proj_kernel,
        out_shape=tuple(jax.ShapeDtypeStruct((rows, w), F32) for w in widths),
        grid=(rows // tm,),
        in_specs=[
            pl.BlockSpec((tm, d), lambda i: (i, 0)),
            _mod_spec(sh, tiles_per_group), _mod_spec(sc, tiles_per_group),
            _const_spec((1, d)),
            _const_spec(w_in.shape),
        ],
        out_specs=tuple(pl.BlockSpec((tm, w), lambda i: (i, 0)) for w in widths),
        compiler_params=_cparams(("parallel",), d * n_in * 2 + 3 * tm * (d + n_in) * 4 + 8 * 2**20),
        name="mix_proj",
    )(h, sh, sc, g.reshape(1, d), w_in)


def _s5_prep_kernel(lre_ref, lim_ref, ldt_ref, bre_ref, bim_ref, are_ref, aim_ref, bbre_ref, bbim_ref):
    lre = lre_ref[...]
    lim = lim_ref[...]
    dt = jnp.exp(ldt_ref[...])
    mag = jnp.exp(lre * dt)
    ang = lim * dt
    a_re = mag * jnp.cos(ang)
    a_im = mag * jnp.sin(ang)
    den = lre * lre + lim * lim
    num_re = a_re - 1.0
    k_re = (num_re * lre + a_im * lim) / den
    k_im = (a_im * lre - num_re * lim) / den
    are_ref[...] = a_re
    aim_ref[...] = a_im
    b_re = bre_ref[...]
    b_im = bim_ref[...]
    bbre_ref[...] = k_re * b_re - k_im * b_im
    bbim_ref[...] = k_re * b_im + k_im * b_re


def _s5_prep_call(lam_re, lam_im, log_dt, b_re_t, b_im_t):
    g, n = lam_re.shape
    c = b_re_t.shape[1]
    return pl.pallas_call(
        _s5_prep_kernel,
        out_shape=(jax.ShapeDtypeStruct((g, 1, n), F32), jax.ShapeDtypeStruct((g, 1, n), F32),
                   jax.ShapeDtypeStruct((g, c, n), F32), jax.ShapeDtypeStruct((g, c, n), F32)),
        name="s5_discretise",
    )(lam_re.reshape(g, 1, n), lam_im.reshape(g, 1, n), log_dt.reshape(g, 1, 1), b_re_t, b_im_t)


def _s5_input_matmul(ub, wb_ref, bure_ref, buim_ref):
    nk, kin, two_s = wb_ref.shape
    s = two_s // 2
    for k in range(nk):
        res = _dot(ub[:, k * kin:(k + 1) * kin], wb_ref[k])
        bure_ref[:, k * s:(k + 1) * s] = res[:, :s]
        buim_ref[:, k * s:(k + 1) * s] = res[:, s:]


def _s5_output(xre, xim, u, wcre_ref, wcim_ref, d_row):
    nk, s, kout = wcre_ref.shape
    xre_b = xre.astype(BF16)
    xim_b = xim.astype(BF16)
    ys = []
    for k in range(nk):
        ys.append(_dot(xre_b[:, k * s:(k + 1) * s], wcre_ref[k]) + _dot(xim_b[:, k * s:(k + 1) * s], wcim_ref[k]))
    y = jnp.concatenate(ys, axis=-1) + d_row * u
    return jax.nn.gelu(y)


def _s5_glu(zb, gv_ref, gg_ref):
    return _dot(zb, gv_ref[...]) * jax.nn.sigmoid(_dot(zb, gg_ref[...]))


def _s5_seq_kernel(u_ref, h0re_ref, h0im_ref, are_ref, aim_ref, wb_ref, wcre_ref, wcim_ref, d_ref,
                   gv_ref, gg_ref, perm_ref, permt_ref,
                   o_ref, hre_ref, him_ref,
                   bure_scr, buim_scr, xre_scr, xim_scr, *, lane_chunk):
    nb, ts, din = u_ref.shape
    rows = nb * ts

    @pl.when(pl.program_id(0) == 0)
    def _():
        hre_ref[...] = h0re_ref[...]
        him_ref[...] = h0im_ref[...]

    u_bm = u_ref[...].reshape(rows, din)
    hi, mid, lo = _split3(u_bm)
    perm = perm_ref[...]
    u_hi_tm = _dot(perm, hi)
    u_tm = u_hi_tm + _dot(perm, mid) + _dot(perm, lo)
    _s5_input_matmul(u_hi_tm.astype(BF16), wb_ref, bure_scr, buim_scr)

    ns = are_ref.shape[1]
    for lc in range(ns // lane_chunk):
        sl = slice(lc * lane_chunk, (lc + 1) * lane_chunk)
        ar = are_ref[:, sl]
        ai = aim_ref[:, sl]
        sr = hre_ref[:, sl]
        si = him_ref[:, sl]
        for t in range(ts):
            rs = slice(t * nb, (t + 1) * nb)
            nr = ar * sr - ai * si + bure_scr[rs, sl]
            ni = ar * si + ai * sr + buim_scr[rs, sl]
            xre_scr[rs, sl] = nr
            xim_scr[rs, sl] = ni
            sr, si = nr, ni
        hre_ref[:, sl] = sr
        him_ref[:, sl] = si

    z_tm = _s5_output(xre_scr[...], xim_scr[...], u_tm, wcre_ref, wcim_ref, d_ref[...])
    z_bm = _dot(permt_ref[...], z_tm.astype(BF16)).astype(BF16)
    o_ref[...] = _s5_glu(z_bm, gv_ref, gg_ref).reshape(nb, ts, o_ref.shape[2])


def _s5_rows_kernel(u_ref, h0re_ref, h0im_ref, are_ref, aim_ref, wb_ref, wcre_ref, wcim_ref, d_ref,
                    gv_ref, gg_ref, o_ref, hre_ref, him_ref, bure_scr, buim_scr):
    u = u_ref[...]
    _s5_input_matmul(u.astype(BF16), wb_ref, bure_scr, buim_scr)
    ar = are_ref[...]
    ai = aim_ref[...]
    sr = h0re_ref[...]
    si = h0im_ref[...]
    nr = ar * sr - ai * si + bure_scr[...]
    ni = ar * si + ai * sr + buim_scr[...]
    hre_ref[...] = nr
    him_ref[...] = ni
    z = _s5_output(nr, ni, u, wcre_ref, wcim_ref, d_ref[...])
    o_ref[...] = _s5_glu(z.astype(BF16), gv_ref, gg_ref)


def _s5_weight_specs(ws):
    return [_const_spec(w.shape) for w in ws]


def _s5_seq_call(s5_in3, h0re, h0im, a_re, a_im, sw, *, ts):
    nb, length, din = s5_in3.shape
    ns = a_re.shape[1]
    d = sw["glu_v"].shape[1]
    rows = nb * ts
    idx = jnp.arange(rows)
    src = (idx % nb) * ts + idx // nb
    perm = (src[:, None] == jnp.arange(rows)[None, :]).astype(BF16)
    a_re_b = jnp.broadcast_to(a_re, (nb, ns))
    a_im_b = jnp.broadcast_to(a_im, (nb, ns))
    ws = [sw["wb"], sw["wc_re"], sw["wc_im"], sw["d"], sw["glu_v"], sw["glu_g"], perm, perm.T]
    kern = functools.partial(_s5_seq_kernel, lane_chunk=4 * V7X_LANES)
    return pl.pallas_call(
        kern,
        out_shape=(jax.ShapeDtypeStruct((nb, length, d), F32),
                   jax.ShapeDtypeStruct((nb, ns), F32), jax.ShapeDtypeStruct((nb, ns), F32)),
        grid=(length // ts,),
        in_specs=[pl.BlockSpec((nb, ts, din), lambda c: (0, c, 0)),
                  _const_spec((nb, ns)), _const_spec((nb, ns)), _const_spec((nb, ns)), _const_spec((nb, ns))]
                 + _s5_weight_specs(ws),
        out_specs=(pl.BlockSpec((nb, ts, d), lambda c: (0, c, 0)),
                   pl.BlockSpec((nb, ns), lambda c: (0, 0)), pl.BlockSpec((nb, ns), lambda c: (0, 0))),
        scratch_shapes=[pltpu.VMEM((rows, ns), F32) for _ in range(4)],
        compiler_params=_cparams(("arbitrary",), 40 * 2**20),
        name="s5_seq",
    )(s5_in3, h0re, h0im, a_re_b, a_im_b, *ws)


def _s5_rows_call(s5_in, h0re, h0im, a_re, a_im, sw):
    rows, din = s5_in.shape
    ns = a_re.shape[1]
    d = sw["glu_v"].shape[1]
    ws = [sw["wb"], sw["wc_re"], sw["wc_im"], sw["d"], sw["glu_v"], sw["glu_g"]]
    full = lambda shape: pl.BlockSpec(shape, lambda i: (0,) * len(shape))
    return pl.pallas_call(
        _s5_rows_kernel,
        out_shape=(jax.ShapeDtypeStruct((rows, d), F32),
                   jax.ShapeDtypeStruct((rows, ns), F32), jax.ShapeDtypeStruct((rows, ns), F32)),
        grid=(1,),
        in_specs=[full((rows, din)), full((rows, ns)), full((rows, ns)), full((1, ns)), full((1, ns))]
                 + [full(w.shape) for w in ws],
        out_specs=(full((rows, d)), full((rows, ns)), full((rows, ns))),
        scratch_shapes=[pltpu.VMEM((rows, ns), F32) for _ in range(2)],
        compiler_params=_cparams(("arbitrary",), 40 * 2**20),
        name="s5_rows",
    )(s5_in, h0re, h0im, a_re, a_im, *ws)


def _headsum(x, ones_bd):
    hi = x.astype(BF16)
    lo = (x - hi.astype(F32)).astype(BF16)
    return _dot(hi, ones_bd) + _dot(lo, ones_bd)


def _rwkv_pre(cur, prev, p, ones_bd, width):
    mixed = cur + p["mu"] * (prev - cur)
    r = mixed[:, 0:width]
    k = mixed[:, width:2 * width]
    v = mixed[:, 2 * width:3 * width]
    wa = mixed[:, 3 * width:3 * width + p["w2p"].shape[0]]
    gd = mixed[:, 3 * width + p["w2p"].shape[0]:]
    lw = _dot(jnp.tanh(wa).astype(BF16), p["w2p"])
    la = _dot(wa.astype(BF16), p["a2p"])
    w = -_softplus(-(p["w0"] + lw)) - 0.5
    ld = -jnp.exp(w)
    a = jax.nn.sigmoid(p["a0"] + la)
    g = _dot(jax.nn.sigmoid(gd).astype(BF16), p["g2"])
    kk = k * p["k_k"]
    norm = jnp.sqrt(_headsum(kk * kk, ones_bd))
    kk = kk / jnp.maximum(norm, 1e-12)
    k2 = k * (1.0 + (a - 1.0) * p["k_a"])
    return r, ld, k2, v, -kk, kk * a, g


def _rwkv_post(y, r, k2, v, g, p, ones_bd, proj, head):
    inv = 1.0 / head
    mean = _headsum(y, ones_bd) * inv
    dlt = y - mean
    var = _headsum(dlt * dlt, ones_bd) * inv
    yn = dlt * lax.rsqrt(var + GN_EPS) * p["ln_w"] + p["ln_b"]
    bonus = _headsum(r * k2 * p["r_k"], ones_bd) * v
    o = ((yn + bonus) * g).astype(BF16)
    return _dot(o, proj)


def _wkv_block(r, ld, k, v, a, b, s_ref, y_ref, tril, head, chunk):
    rows, width = r.shape
    nsub = rows // chunk
    nh = width // head
    c = chunk
    pairs = [(s, h) for s in range(nsub) for h in range(nh)]
    ri = lax.broadcasted_iota(jnp.int32, (c, c), 0)
    ci = lax.broadcasted_iota(jnp.int32, (c, c), 1)
    strict = ri > ci
    incl = ri >= ci

    fac = []
    for s in range(nsub):
        rs = slice(s * c, (s + 1) * c)
        lds = ld[rs]
        hi, mid, lo = _split3(lds)
        cum = _dot(tril, hi) + _dot(tril, mid) + _dot(tril, lo)
        g_inv = jnp.exp(-cum)
        cum_end = cum[c - 1:c, :]
        g_rel = jnp.exp(cum_end - cum)
        fac.append(dict(a_t=a[rs] * jnp.exp(cum - lds), r_t=r[rs] * jnp.exp(cum), b_t=b[rs] * g_inv,
                        k_t=k[rs] * g_inv, b_e=(b[rs] * g_rel).astype(BF16), k_e=(k[rs] * g_rel).astype(BF16),
                        g_end=jnp.exp(cum_end), v=v[rs].astype(BF16)))

    hs = lambda h: slice(h * head, (h + 1) * head)
    ar, sc, vh = {}, {}, {}
    for s, h in pairs:
        f = fac[s]
        ar[s, h] = jnp.concatenate([f["a_t"][:, hs(h)], f["r_t"][:, hs(h)]], axis=0).astype(BF16)
        bk = jnp.concatenate([f["b_t"][:, hs(h)], f["k_t"][:, hs(h)]], axis=0).astype(BF16)
        sc[s, h] = _dot_nt(ar[s, h], bk)
        vh[s, h] = f["v"][:, hs(h)]
    pw, nm, m_rb, lv, mv, vk = {}, {}, {}, {}, {}, {}
    for s, h in pairs:
        x = sc[s, h]
        pw[s, h] = jnp.where(strict, x[:c, :c], 0.0)
        nm[s, h] = pw[s, h]
        l_ak = jnp.where(strict, x[:c, c:], 0.0).astype(BF16)
        m_rb[s, h] = jnp.where(incl, x[c:, :c], 0.0).astype(BF16)
        m_rk = jnp.where(incl, x[c:, c:], 0.0).astype(BF16)
        lv[s, h] = _dot(l_ak, vh[s, h])
        mv[s, h] = _dot(m_rk, vh[s, h])
        vk[s, h] = _dot_tn(vh[s, h], fac[s]["k_e"][:, hs(h)])
    for _ in range(int(math.log2(c)) - 1):
        for s, h in pairs:
            pb = pw[s, h].astype(BF16)
            pw[s, h] = _dot(pb, pb)
        for s, h in pairs:
            nm[s, h] = nm[s, h] + pw[s, h] + _dot(nm[s, h].astype(BF16), pw[s, h].astype(BF16))

    for s in range(nsub):
        s0 = [s_ref[h] for h in range(nh)]
        am = [_dot_nt(ar[s, h], s0[h].astype(BF16)) for h in range(nh)]
        ub = []
        for h in range(nh):
            w = am[h][:c] + lv[s, h]
            ub.append((w + _dot(nm[s, h].astype(BF16), w.astype(BF16))).astype(BF16))
        for h in range(nh):
            y_ref[s * c:(s + 1) * c, hs(h)] = am[h][c:] + _dot(m_rb[s, h], ub[h]) + mv[s, h]
            s_ref[h] = (s0[h] * fac[s]["g_end"][:, hs(h)] + _dot_tn(ub[h], fac[s]["b_e"][:, hs(h)])
                        + vk[s, h])


_RWKV_PARAM_NAMES = ("mu", "w0", "w2p", "a0", "a2p", "g2", "k_k", "k_a", "r_k", "ln_w", "ln_b")


def _rwkv_seq_kernel(cur_ref, shift0_ref, s0_ref, *rest, chunk, head):
    np_ = len(_RWKV_PARAM_NAMES)
    p = {n: ref[...] for n, ref in zip(_RWKV_PARAM_NAMES, rest[:np_])}
    proj_ref, ones_ref, tril_ref = rest[np_:np_ + 3]
    out_ref, st_ref = rest[np_ + 3:np_ + 5]
    carry_scr, y_scr = rest[np_ + 5:]

    @pl.when(pl.program_id(1) == 0)
    def _():
        carry_scr[...] = shift0_ref[...]
        st_ref[...] = s0_ref[...]

    cur = cur_ref[...]
    rows, _ = cur.shape
    width = y_scr.shape[1]
    row = lax.broadcasted_iota(jnp.int32, cur.shape, 0)
    prev = jnp.where(row == 0, carry_scr[...], pltpu.roll(cur, 1, axis=0))
    carry_scr[...] = cur[rows - 1:rows, :]

    ones_bd = ones_ref[...]
    r, ld, k2, v, a, b, g = _rwkv_pre(cur, prev, p, ones_bd, width)

    _wkv_block(r, ld, k2, v, a, b, st_ref, y_scr, tril_ref[...], head, chunk)

    out_ref[...] = _rwkv_post(y_scr[...], r, k2, v, g, p, ones_bd, proj_ref[...], head)


def _rwkv_rows_pre_kernel(cur_ref, prev_ref, *rest, width):
    np_ = len(_RWKV_PARAM_NAMES)
    p = {n: ref[...] for n, ref in zip(_RWKV_PARAM_NAMES, rest[:np_])}
    ones_ref = rest[np_]
    outs = rest[np_ + 1:]
    vals = _rwkv_pre(cur_ref[...], prev_ref[...], p, ones_ref[...], width)
    for o_ref, val in zip(outs, vals):
        o_ref[...] = val


def _wkv_step_kernel(r_ref, ld_ref, k_ref, v_ref, a_ref, b_ref, s_ref, y_ref, so_ref, *, head):
    nb, nh = s_ref.shape[0], s_ref.shape[1]
    rows = nh * head
    eye = (lax.broadcasted_iota(jnp.int32, (head, head), 0)
           == lax.broadcasted_iota(jnp.int32, (head, head), 1)).astype(F32)
    eye_rows = jnp.concatenate([eye] * nh, axis=0)
    ones = jnp.ones((head, head), BF16)
    dec_all = jnp.exp(ld_ref[...])

    def rowsum(x):
        hi = x.astype(BF16)
        lo = (x - hi.astype(F32)).astype(BF16)
        return _dot(hi, ones) + _dot(lo, ones)

    for i in range(nb):
        def per_row(get):
            return jnp.concatenate(
                [jnp.broadcast_to(get(h), (head, head)) for h in range(nh)], axis=0)

        vec = lambda ref: per_row(lambda h: ref[i:i + 1, h * head:(h + 1) * head])
        dec = per_row(lambda h: dec_all[i:i + 1, h * head:(h + 1) * head])
        v_rows = [jnp.broadcast_to(v_ref[i:i + 1, h * head:(h + 1) * head], (head, head)) for h in range(nh)]
        v_col = jnp.concatenate([x.T for x in v_rows], axis=0)
        s = s_ref[i].reshape(rows, head)
        sa = rowsum(s * vec(a_ref))
        s2 = s * dec + sa * vec(b_ref) + v_col * vec(k_ref)
        so_ref[i] = s2.reshape(nh, head, head)
        yb = rowsum(s2 * vec(r_ref)) * eye_rows
        y2 = jnp.sum(yb.reshape(nh, head, head), axis=1)
        for h in range(nh):
            y_ref[i:i + 1, h * head:(h + 1) * head] = y2[h:h + 1, :]


def _rwkv_rows_post_kernel(y_ref, r_ref, k_ref, v_ref, g_ref, *rest, head):
    np_ = len(_RWKV_PARAM_NAMES)
    p = {n: ref[...] for n, ref in zip(_RWKV_PARAM_NAMES, rest[:np_])}
    proj_ref, ones_ref, o_ref = rest[np_:]
    o_ref[...] = _rwkv_post(y_ref[...], r_ref[...], k_ref[...], v_ref[...], g_ref[...], p,
                            ones_ref[...], proj_ref[...], head)


def _rwkv_param_list(rw):
    return [rw[n] for n in _RWKV_PARAM_NAMES]


def _rwkv_seq_call(cur, shift0, s0, rw, *, nb, length, tr):
    n_shift = cur.shape[1]
    nh, head = s0.shape[1], s0.shape[2]
    width = nh * head
    d = rw["proj"].shape[1]
    params = _rwkv_param_list(rw)
    chunk = min(WKV_CHUNK, tr)
    tril = (jnp.arange(chunk)[:, None] >= jnp.arange(chunk)[None, :]).astype(BF16)
    consts = params + [rw["proj"], rw["ones_bd"], tril]
    tiles = length // tr
    kern = functools.partial(_rwkv_seq_kernel, chunk=chunk, head=head)
    return pl.pallas_call(
        kern,
        out_shape=(jax.ShapeDtypeStruct((nb * length, d), F32),
                   jax.ShapeDtypeStruct((nb, nh, head, head), F32)),
        grid=(nb, tiles),
        in_specs=[pl.BlockSpec((tr, n_shift), lambda b, i: (b * tiles + i, 0)),
                  pl.BlockSpec((None, 1, n_shift), lambda b, i: (b, 0, 0)),
                  pl.BlockSpec((None, nh, head, head), lambda b, i: (b, 0, 0, 0))]
                 + [pl.BlockSpec(c.shape, lambda b, i, nd=c.ndim: (0,) * nd) for c in consts],
        out_specs=(pl.BlockSpec((tr, d), lambda b, i: (b * tiles + i, 0)),
                   pl.BlockSpec((None, nh, head, head), lambda b, i: (b, 0, 0, 0))),
        scratch_shapes=[pltpu.VMEM((1, n_shift), F32), pltpu.VMEM((tr, width), F32)],
        compiler_params=_cparams(("arbitrary", "arbitrary"), 40 * 2**20),
        name="rwkv_seq",
    )(cur, shift0.reshape(nb, 1, n_shift), s0, *consts)


def _rwkv_rows_call(cur, prev, s0, rw, *, seq_block):
    rows, n_shift = cur.shape
    nh, head = s0.shape[1], s0.shape[2]
    width = nh * head
    d = rw["proj"].shape[1]
    params = _rwkv_param_list(rw)
    full = lambda shape: pl.BlockSpec(shape, lambda i: (0,) * len(shape))
    vec = jax.ShapeDtypeStruct((rows, width), F32)

    pre_consts = params + [rw["ones_bd"]]
    r, ld, k2, v, a, b, g = pl.pallas_call(
        functools.partial(_rwkv_rows_pre_kernel, width=width),
        out_shape=(vec,) * 7,
        grid=(1,),
        in_specs=[full(cur.shape), full(prev.shape)] + [full(c.shape) for c in pre_consts],
        out_specs=(full((rows, width)),) * 7,
        compiler_params=_cparams(("arbitrary",), 32 * 2**20),
        name="rwkv_rows_pre",
    )(cur, prev, *pre_consts)

    blk = pl.BlockSpec((seq_block, width), lambda i: (i, 0))
    sblk = pl.BlockSpec((seq_block, nh, head, head), lambda i: (i, 0, 0, 0))
    y, s1 = pl.pallas_call(
        functools.partial(_wkv_step_kernel, head=head),
        out_shape=(vec, jax.ShapeDtypeStruct(s0.shape, F32)),
        grid=(rows // seq_block,),
        in_specs=[blk] * 6 + [sblk],
        out_specs=(blk, sblk),
        compiler_params=_cparams(("parallel",), 32 * 2**20),
        name="wkv_step",
    )(r, ld, k2, v, a, b, s0)

    post_consts = params + [rw["proj"], rw["ones_bd"]]
    out = pl.pallas_call(
        functools.partial(_rwkv_rows_post_kernel, head=head),
        out_shape=jax.ShapeDtypeStruct((rows, d), F32),
        grid=(1,),
        in_specs=[full((rows, width))] * 5 + [full(c.shape) for c in post_consts],
        out_specs=full((rows, d)),
        compiler_params=_cparams(("arbitrary",), 32 * 2**20),
        name="rwkv_rows_post",
    )(y, r, k2, v, g, *post_consts)
    return out, s1


def _merge_kernel(h_ref, gates_ref, ys5_ref, yrw_ref, gt2_ref, wout_ref,
                  sh_ref, sc_ref, gt_ref, g_ref, w1_ref, w3_ref, w2_ref, gfin_ref, o_ref, *, n_chunks):
    d = h_ref.shape[1]
    gates = gates_ref[...]
    m = jax.nn.sigmoid(gates[:, :d]) * ys5_ref[...] + jax.nn.sigmoid(gates[:, d:]) * yrw_ref[...]
    h2 = h_ref[...] + gt2_ref[...] * _dot(m.astype(BF16), wout_ref[...])
    h3 = _ffn_core(h2, sh_ref[...], sc_ref[...], gt_ref[...], g_ref[...], w1_ref, w3_ref, w2_ref, n_chunks)
    o_ref[...] = _rms(h3, gfin_ref[...])


def _merge_call(h, gates, ys5, yrw, gt2, w_out, sh, sc, gt, g, w1, w3, w2, g_fin, *, tm, tiles_per_group):
    rows, d = h.shape
    d_ff = w1.shape[1]
    kern = functools.partial(_merge_kernel, n_chunks=_ffn_chunks(d_ff))
    row_spec = lambda w: pl.BlockSpec((tm, w), lambda i: (i, 0))
    weights = (3 * d * d_ff + d * d) * 2
    tiles = 2 * tm * 6 * d * 4 + 4 * tm * d_ff * 4 + 6 * tm * d * 4
    return pl.pallas_call(
        kern,
        out_shape=jax.ShapeDtypeStruct((rows, d), F32),
        grid=(rows // tm,),
        in_specs=[row_spec(d), row_spec(2 * d), row_spec(d), row_spec(d),
                  _mod_spec(gt2, tiles_per_group), _const_spec(w_out.shape),
                  _mod_spec(sh, tiles_per_group), _mod_spec(sc, tiles_per_group), _mod_spec(gt, tiles_per_group),
                  _const_spec((1, d)), _const_spec(w1.shape), _const_spec(w3.shape), _const_spec(w2.shape),
                  _const_spec((1, d))],
        out_specs=row_spec(d),
        compiler_params=_cparams(("parallel",), weights + tiles + 8 * 2**20),
        name="merge_ffn2",
    )(h, gates, ys5, yrw, gt2, w_out, sh, sc, gt, g.reshape(1, d), w1, w3, w2, g_fin.reshape(1, d))


def _pick_tile(n, target):
    t = min(n, target)
    while n % t:
        t -= V7X_SUBLANES
    return t


def _block_diag_rows(blk):
    nk, g, r, c = blk.shape
    eye = jnp.eye(g, dtype=blk.dtype)
    return (blk[:, :, :, None, :] * eye[None, :, None, :, None]).reshape(nk, g * r, g * c)


def _s5_weights(a_shape, bb_re, bb_im, c_re, c_im, s5_d, glu_v, glu_g):
    g, c, n = bb_re.shape
    gpb = max(1, min(g, V7X_MXU_DIM // c))
    nk = g // gpb
    wb = jnp.concatenate([_block_diag_rows(bb_re.reshape(nk, gpb, c, n)),
                          _block_diag_rows(bb_im.reshape(nk, gpb, c, n))], axis=-1).astype(BF16)
    to_out = lambda w: _block_diag_rows(jnp.swapaxes(w, 1, 2).reshape(nk, gpb, n, c)).astype(BF16)
    return dict(wb=wb, wc_re=to_out(c_re), wc_im=to_out(-c_im), d=s5_d.reshape(1, g * c),
                glu_v=glu_v.astype(BF16), glu_g=glu_g.astype(BF16))


def _layer(x, c_mod, s5_re0, s5_im0, wkv0, shift0, w, *, sequential):
    nb, length, d = x.shape
    rows = nb * length
    g, n = s5_re0.shape[1], s5_re0.shape[2]
    ns = g * n
    n_shift = shift0.shape[1]
    s5_width = w["s5"]["d"].shape[1]
    widths = (s5_width, n_shift, 2 * d)

    if sequential:
        tm = _pick_tile(length, 512)
        tpg = length // tm
        mods = [c_mod[:, i].reshape(nb, 1, d) for i in range(N_MOD)]
    else:
        tm = _pick_tile(rows, 512)
        tpg = 1
        mods = [c_mod[:, i].reshape(rows // tm, tm, d) for i in range(N_MOD)]
    sh1, sc1, gt1, sh2, sc2, gt2, sh3, sc3, gt3 = mods

    x2 = x.reshape(rows, d)
    h1 = _ffn_call(x2, sh1, sc1, gt1, w["g_ffn1"], *w["ffn1"], tm=tm, tiles_per_group=tpg)
    s5_in, cur, gates = _proj_call(h1, sh2, sc2, w["g_mix"], w["w_in"], widths, tm=tm, tiles_per_group=tpg)

    h0re = s5_re0.reshape(nb, ns)
    h0im = s5_im0.reshape(nb, ns)
    if sequential:
        ts = V7X_MXU_DIM // nb
        y_s5, hre, him = _s5_seq_call(s5_in.reshape(nb, length, s5_width), h0re, h0im,
                                      w["a_re"], w["a_im"], w["s5"], ts=ts)
        y_s5 = y_s5.reshape(rows, d)
        tr = _pick_tile(length, 4 * WKV_CHUNK)
        y_rw, wkv1 = _rwkv_seq_call(cur, shift0, wkv0, w["rwkv"], nb=nb, length=length, tr=tr)
    else:
        y_s5, hre, him = _s5_rows_call(s5_in, h0re, h0im, w["a_re"], w["a_im"], w["s5"])
        y_rw, wkv1 = _rwkv_rows_call(cur, shift0, wkv0, w["rwkv"], seq_block=V7X_SUBLANES)

    tm2 = _pick_tile(tm, 256)
    tpg2 = tpg * (tm // tm2) if sequential else 1
    if not sequential:
        mods2 = [c_mod[:, i].reshape(rows // tm2, tm2, d) for i in (5, 6, 7, 8)]
    else:
        mods2 = [gt2, sh3, sc3, gt3]
    y = _merge_call(h1, gates, y_s5, y_rw, mods2[0], w["w_out"], mods2[1], mods2[2], mods2[3],
                    w["g_ffn2"], *w["ffn2"], w["g_final"], tm=tm2, tiles_per_group=tpg2)

    shift1 = cur.reshape(nb, length, n_shift)[:, -1]
    return (y.reshape(nb, length, d), hre.reshape(nb, g, n), him.reshape(nb, g, n), wkv1, shift1)


def kernel(x_prompt, x_sample, c_prompt, c_sample, state_s5_re, state_s5_im, state_wkv, state_shift, w_ada, b_ada, g_ffn1, g_mix, g_ffn2, g_final, ffn1_w1, ffn1_w3, ffn1_w2, ffn2_w1, ffn2_w3, ffn2_w2, w_in, mu_shift, s5_lam_re, s5_lam_im, s5_log_dt, s5_b_re, s5_b_im, s5_c_re, s5_c_im, s5_d, s5_glu_v, s5_glu_g, rwkv_w0, rwkv_w2, rwkv_a0, rwkv_a2, rwkv_g2, rwkv_k_k, rwkv_k_a, rwkv_r_k, rwkv_ln_w, rwkv_ln_b, rwkv_proj, w_out):
    bp, _, d = x_prompt.shape
    bs = x_sample.shape[0]
    g, n = s5_lam_re.shape
    nh, head = rwkv_r_k.shape
    width = nh * head
    n_shift = mu_shift.shape[0]
    bf = lambda t: t.astype(BF16)
    row = lambda t: t.reshape(1, -1)

    a_re, a_im, bb_re, bb_im = _s5_prep_call(s5_lam_re, s5_lam_im, s5_log_dt,
                                             jnp.swapaxes(s5_b_re, 1, 2), jnp.swapaxes(s5_b_im, 1, 2))
    w_lora = rwkv_w2.shape[0]
    a_lora = rwkv_a2.shape[0]
    w2p = jnp.concatenate([rwkv_w2, jnp.zeros((a_lora, width), F32)], axis=0)
    a2p = jnp.concatenate([jnp.zeros((w_lora, width), F32), rwkv_a2], axis=0)
    rw = dict(mu=row(mu_shift), w0=row(rwkv_w0), w2p=bf(w2p), a0=row(rwkv_a0), a2p=bf(a2p), g2=bf(rwkv_g2),
              k_k=row(rwkv_k_k), k_a=row(rwkv_k_a), r_k=row(rwkv_r_k), ln_w=row(rwkv_ln_w), ln_b=row(rwkv_ln_b),
              proj=bf(rwkv_proj),
              ones_bd=jnp.kron(jnp.eye(nh, dtype=F32), jnp.ones((head, head), F32)).astype(BF16))
    w = dict(
        g_ffn1=g_ffn1, g_mix=g_mix, g_ffn2=g_ffn2, g_final=g_final,
        ffn1=(bf(ffn1_w1), bf(ffn1_w3), bf(ffn1_w2)), ffn2=(bf(ffn2_w1), bf(ffn2_w3), bf(ffn2_w2)),
        w_in=bf(w_in), w_out=bf(w_out),
        a_re=a_re.reshape(1, g * n), a_im=a_im.reshape(1, g * n),
        s5=_s5_weights(None, bb_re, bb_im, s5_c_re, s5_c_im, s5_d, s5_glu_v, s5_glu_g),
        rwkv=rw,
    )

    mod = _mod_call(jnp.concatenate([c_prompt, c_sample], axis=0), bf(w_ada), b_ada)
    mod_p = mod[:bp].reshape(bp, N_MOD, d)
    mod_s = mod[bp:].reshape(bs, N_MOD, d)

    z_s5 = jnp.zeros((bp, g, n), state_s5_re.dtype)
    z_wkv = jnp.zeros((bp, nh, head, head), state_wkv.dtype)
    z_shift = jnp.zeros((bp, n_shift), state_shift.dtype)
    y_p, s5re_p, s5im_p, wkv_p, shift_p = _layer(x_prompt, mod_p, z_s5, z_s5, z_wkv, z_shift, w, sequential=True)
    y_s, s5re_s, s5im_s, wkv_s, shift_s = _layer(x_sample, mod_s, state_s5_re, state_s5_im, state_wkv,
                                                 state_shift, w, sequential=False)
    return (y_p, y_s, s5re_p, s5im_p, wkv_p, shift_p, s5re_s, s5im_s, wkv_s, shift_s)
```

```python
import functools
import math

import jax
import jax.numpy as jnp
from jax import lax
from jax.experimental import pallas as pl
from jax.experimental.pallas import tpu as pltpu

F32 = jnp.float32
BF16 = jnp.bfloat16

NORM_EPS = 1e-6
GN_EPS = 64e-5
N_MOD = 9

V7X_LANES = 128
V7X_SUBLANES = 8
V7X_MXU_DIM = 256
V7X_VMEM_BYTES = 64 * 1024 * 1024
VMEM_CAP_BYTES = V7X_VMEM_BYTES - 8 * 1024 * 1024

WKV_CHUNK = 64


def _cparams(semantics, vmem_bytes):
    return pltpu.CompilerParams(
        dimension_semantics=semantics,
        vmem_limit_bytes=int(min(vmem_bytes, VMEM_CAP_BYTES)),
    )


def _const_spec(shape):
    nd = len(shape)
    return pl.BlockSpec(shape, lambda *_: (0,) * nd, pipeline_mode=pl.Buffered(1))


def _dot(a, b):
    return jnp.dot(a, b, preferred_element_type=F32)


def _dot_nt(a, b):
    return lax.dot_general(a, b, (((1,), (1,)), ((), ())), preferred_element_type=F32)


def _dot_tn(a, b):
    return lax.dot_general(a, b, (((0,), (0,)), ((), ())), preferred_element_type=F32)


def _split3(x):
    hi = x.astype(BF16)
    r1 = x - hi.astype(F32)
    mid = r1.astype(BF16)
    lo = (r1 - mid.astype(F32)).astype(BF16)
    return hi, mid, lo


def _rms(x, g):
    ms = jnp.mean(x * x, axis=-1, keepdims=True)
    return x * lax.rsqrt(ms + NORM_EPS) * g


def _softplus(x):
    return jnp.maximum(x, 0.0) + jnp.log1p(jnp.exp(-jnp.abs(x)))


def _mod_kernel(c_ref, w_ref, b_ref, o_ref):
    c = c_ref[...]
    s = (c * jax.nn.sigmoid(c)).astype(BF16)
    o_ref[...] = _dot(s, w_ref[...]) + b_ref[...]


def _mod_call(c, w_ada, b_ada):
    rows, d = c.shape
    n = w_ada.shape[1]
    return pl.pallas_call(
        _mod_kernel,
        out_shape=jax.ShapeDtypeStruct((n // d, rows, d), F32),
        grid=(n // d,),
        in_specs=[
            pl.BlockSpec((rows, d), lambda j: (0, 0)),
            pl.BlockSpec((d, d), lambda j: (0, j)),
            pl.BlockSpec((1, d), lambda j: (0, j)),
        ],
        out_specs=pl.BlockSpec((None, rows, d), lambda j: (j, 0, 0)),
        compiler_params=_cparams(("parallel",), 32 * 2**20),
        name="adaln_mod",
    )(c, w_ada, b_ada.reshape(1, n))


class _ModRows:
    def __init__(self, mod, row0, nrows, tm, tiles_per_seq):
        self.mod, self.row0, self.nrows, self.tm, self.tiles_per_seq = mod, row0, nrows, tm, tiles_per_seq

    def spec(self, k):
        d = self.mod.shape[2]
        if self.tiles_per_seq is None:
            blk0 = self.row0 // self.tm
            return pl.BlockSpec((None, self.tm, d), lambda i: (k, blk0 + i, 0))
        blk0 = self.row0 // self.nrows
        return pl.BlockSpec((None, self.nrows, d), lambda i: (k, blk0, 0))


def _mod_row(ref, tiles_per_seq):
    if tiles_per_seq is None:
        return ref[...]
    return ref[pl.ds(pl.program_id(0) // tiles_per_seq, 1), :]


def _ffn_core(x, sh, sc, gt, g, w1_ref, w3_ref, w2_ref, n_chunks):
    u = (_rms(x, g) * (1.0 + sc) + sh).astype(BF16)
    ck = w1_ref.shape[1] // n_chunks
    acc = None
    for c in range(n_chunks):
        a = _dot(u, w1_ref[:, c * ck:(c + 1) * ck])
        b = _dot(u, w3_ref[:, c * ck:(c + 1) * ck])
        hm = (a * jax.nn.sigmoid(a) * b).astype(BF16)
        part = _dot(hm, w2_ref[c * ck:(c + 1) * ck, :])
        acc = part if acc is None else acc + part
    return x + (0.5 * gt) * acc


def _ffn_kernel(x_ref, sh_ref, sc_ref, gt_ref, g_ref, w1_ref, w3_ref, w2_ref, o_ref, *, n_chunks, tiles_per_seq):
    row = functools.partial(_mod_row, tiles_per_seq=tiles_per_seq)
    o_ref[...] = _ffn_core(x_ref[...], row(sh_ref), row(sc_ref), row(gt_ref), g_ref[...],
                           w1_ref, w3_ref, w2_ref, n_chunks)


def _ffn_chunks(d_ff):
    return 2 if d_ff % (2 * V7X_LANES) == 0 else 1


def _ffn_call(x2, mods, ks, g, w1, w3, w2, *, tm):
    rows, d = x2.shape
    d_ff = w1.shape[1]
    kern = functools.partial(_ffn_kernel, n_chunks=_ffn_chunks(d_ff), tiles_per_seq=mods.tiles_per_seq)
    weights = 3 * d * d_ff * 2
    tiles = 4 * tm * d * 4 + 4 * tm * d_ff * 4
    return pl.pallas_call(
        kern,
        out_shape=jax.ShapeDtypeStruct((rows, d), F32),
        grid=(rows // tm,),
        in_specs=[
            pl.BlockSpec((tm, d), lambda i: (i, 0)),
            mods.spec(ks[0]), mods.spec(ks[1]), mods.spec(ks[2]),
            _const_spec((1, d)),
            _const_spec(w1.shape), _const_spec(w3.shape), _const_spec(w2.shape),
        ],
        out_specs=pl.BlockSpec((tm, d), lambda i: (i, 0)),
        compiler_params=_cparams(("parallel",), weights + tiles + 8 * 2**20),
        name="ffn1",
    )(x2, mods.mod, mods.mod, mods.mod, g.reshape(1, d), w1, w3, w2)


def _proj_kernel(h_ref, sh_ref, sc_ref, g_ref, win_ref, s5_ref, cur_ref, gates_ref, *, tiles_per_seq):
    row = functools.partial(_mod_row, tiles_per_seq=tiles_per_seq)
    u = (_rms(h_ref[...], g_ref[...]) * (1.0 + row(sc_ref)) + row(sh_ref)).astype(BF16)
    n0 = s5_ref.shape[1]
    n1 = n0 + cur_ref.shape[1]
    s5_ref[...] = _dot(u, win_ref[:, :n0])
    cur_ref[...] = _dot(u, win_ref[:, n0:n1])
    gates_ref[...] = _dot(u, win_ref[:, n1:])


def _proj_call(h, mods, ks, g, w_in, widths, *, tm):
    rows, d = h.shape
    n_in = w_in.shape[1]
    return pl.pallas_call(
        functools.partial(_proj_kernel, tiles_per_seq=mods.tiles_per_seq),
        out_shape=tuple(jax.ShapeDtypeStruct((rows, w), F32) for w in widths),
        grid=(rows // tm,),
        in_specs=[
            pl.BlockSpec((tm, d), lambda i: (i, 0)),
            mods.spec(ks[0]), mods.spec(ks[1]),
            _const_spec((1, d)),
            _const_spec(w_in.shape),
        ],
        out_specs=tuple(pl.BlockSpec((tm, w), lambda i: (i, 0)) for w in widths),
        compiler_params=_cparams(("parallel",), d * n_in * 2 + 3 * tm * (d + n_in) * 4 + 8 * 2**20),
        name="mix_proj",
    )(h, mods.mod, mods.mod, g.reshape(1, d), w_in)


def _s5_prep_kernel(lre_ref, lim_ref, ldt_ref, bre_ref, bim_ref, cre_ref, cim_ref,
                    are_ref, aim_ref, wb_ref, wcre_ref, wcim_ref, *, gpb):
    lre = lre_ref[...]
    lim = lim_ref[...]
    dt = jnp.exp(ldt_ref[...])
    mag = jnp.exp(lre * dt)
    ang = lim * dt
    a_re = mag * jnp.cos(ang)
    a_im = mag * jnp.sin(ang)
    den = lre * lre + lim * lim
    num_re = a_re - 1.0
    k_re = (num_re * lre + a_im * lim) / den
    k_im = (a_im * lre - num_re * lim) / den
    b_re = bre_ref[...]
    b_im = bim_ref[...]
    bb_re = (k_re * b_re - k_im * b_im).astype(BF16)
    bb_im = (k_re * b_im + k_im * b_re).astype(BF16)
    c_re = cre_ref[...].astype(BF16)
    c_im = (-cim_ref[...]).astype(BF16)
    g, c, n = b_re.shape
    nb = are_ref.shape[0]
    s = gpb * n
    wb_ref[...] = jnp.zeros(wb_ref.shape, BF16)
    wcre_ref[...] = jnp.zeros(wcre_ref.shape, BF16)
    wcim_ref[...] = jnp.zeros(wcim_ref.shape, BF16)
    for gi in range(g):
        k, gl = divmod(gi, gpb)
        rows = slice(gl * c, (gl + 1) * c)
        cols = slice(gl * n, (gl + 1) * n)
        are_ref[:, gi * n:(gi + 1) * n] = jnp.broadcast_to(a_re[gi], (nb, n))
        aim_ref[:, gi * n:(gi + 1) * n] = jnp.broadcast_to(a_im[gi], (nb, n))
        wb_ref[k, rows, cols] = bb_re[gi]
        wb_ref[k, rows, s + gl * n:s + (gl + 1) * n] = bb_im[gi]
        wcre_ref[k, rows, cols] = c_re[gi]
        wcim_ref[k, rows, cols] = c_im[gi]


def _s5_prep_call(lam_re, lam_im, log_dt, b_re_t, b_im_t, c_re, c_im, nb):
    g, n = lam_re.shape
    c = b_re_t.shape[1]
    gpb = max(1, min(g, V7X_MXU_DIM // c))
    nk = g // gpb
    return pl.pallas_call(
        functools.partial(_s5_prep_kernel, gpb=gpb),
        out_shape=(jax.ShapeDtypeStruct((nb, g * n), F32), jax.ShapeDtypeStruct((nb, g * n), F32),
                   jax.ShapeDtypeStruct((nk, gpb * c, 2 * gpb * n), BF16),
                   jax.ShapeDtypeStruct((nk, gpb * c, gpb * n), BF16),
                   jax.ShapeDtypeStruct((nk, gpb * c, gpb * n), BF16)),
        name="s5_discretise",
    )(lam_re.reshape(g, 1, n), lam_im.reshape(g, 1, n), log_dt.reshape(g, 1, 1), b_re_t, b_im_t, c_re, c_im)


def _s5_input_matmul(ub, wb_ref, bure_ref, buim_ref):
    nk, kin, two_s = wb_ref.shape
    s = two_s // 2
    for k in range(nk):
        res = _dot(ub[:, k * kin:(k + 1) * kin], wb_ref[k])
        bure_ref[:, k * s:(k + 1) * s] = res[:, :s]
        buim_ref[:, k * s:(k + 1) * s] = res[:, s:]


def _s5_output(xre, xim, u, wcre_ref, wcim_ref, d_row):
    nk, kout, s = wcre_ref.shape
    xre_b = xre.astype(BF16)
    xim_b = xim.astype(BF16)
    ys = []
    for k in range(nk):
        ys.append(_dot_nt(xre_b[:, k * s:(k + 1) * s], wcre_ref[k])
                  + _dot_nt(xim_b[:, k * s:(k + 1) * s], wcim_ref[k]))
    y = jnp.concatenate(ys, axis=-1) + d_row * u
    return jax.nn.gelu(y)


def _s5_glu(zb, gv_ref, gg_ref):
    return _dot(zb, gv_ref[...]) * jax.nn.sigmoid(_dot(zb, gg_ref[...]))


def _s5_seq_kernel(u_ref, h0re_ref, h0im_ref, are_ref, aim_ref, wb_ref, wcre_ref, wcim_ref, d_ref,
                   gv_ref, gg_ref, perm_ref, permt_ref,
                   o_ref, hre_ref, him_ref,
                   bure_scr, buim_scr, xre_scr, xim_scr, *, lane_chunk):
    nb, ts, din = u_ref.shape
    rows = nb * ts

    @pl.when(pl.program_id(0) == 0)
    def _():
        hre_ref[...] = h0re_ref[...]
        him_ref[...] = h0im_ref[...]

    u_bm = u_ref[...].reshape(rows, din)
    hi, mid, lo = _split3(u_bm)
    perm = perm_ref[...]
    u_hi_tm = _dot(perm, hi)
    u_tm = u_hi_tm + _dot(perm, mid) + _dot(perm, lo)
    _s5_input_matmul(u_hi_tm.astype(BF16), wb_ref, bure_scr, buim_scr)

    ns = are_ref.shape[1]
    for lc in range(ns // lane_chunk):
        sl = slice(lc * lane_chunk, (lc + 1) * lane_chunk)
        ar = are_ref[:, sl]
        ai = aim_ref[:, sl]
        sr = hre_ref[:, sl]
        si = him_ref[:, sl]
        for t in range(ts):
            rs = slice(t * nb, (t + 1) * nb)
            nr = ar * sr - ai * si + bure_scr[rs, sl]
            ni = ar * si + ai * sr + buim_scr[rs, sl]
            xre_scr[rs, sl] = nr
            xim_scr[rs, sl] = ni
            sr, si = nr, ni
        hre_ref[:, sl] = sr
        him_ref[:, sl] = si

    z_tm = _s5_output(xre_scr[...], xim_scr[...], u_tm, wcre_ref, wcim_ref, d_ref[...])
    z_bm = _dot(permt_ref[...], z_tm.astype(BF16)).astype(BF16)
    o_ref[...] = _s5_glu(z_bm, gv_ref, gg_ref).reshape(nb, ts, o_ref.shape[2])


def _s5_rows_kernel(u_ref, h0re_ref, h0im_ref, are_ref, aim_ref, wb_ref, wcre_ref, wcim_ref, d_ref,
                    gv_ref, gg_ref, o_ref, hre_ref, him_ref, bure_scr, buim_scr):
    u = u_ref[...]
    _s5_input_matmul(u.astype(BF16), wb_ref, bure_scr, buim_scr)
    ar = are_ref[0:1, :]
    ai = aim_ref[0:1, :]
    sr = h0re_ref[...]
    si = h0im_ref[...]
    nr = ar * sr - ai * si + bure_scr[...]
    ni = ar * si + ai * sr + buim_scr[...]
    hre_ref[...] = nr
    him_ref[...] = ni
    z = _s5_output(nr, ni, u, wcre_ref, wcim_ref, d_ref[...])
    o_ref[...] = _s5_glu(z.astype(BF16), gv_ref, gg_ref)


def _s5_weight_specs(ws):
    return [_const_spec(w.shape) for w in ws]


def _s5_seq_call(s5_in3, h0re, h0im, a_re, a_im, sw, *, ts):
    nb, length, din = s5_in3.shape
    ns = a_re.shape[1]
    d = sw["glu_v"].shape[1]
    rows = nb * ts
    idx = jnp.arange(rows)
    src = (idx % nb) * ts + idx // nb
    perm = (src[:, None] == jnp.arange(rows)[None, :]).astype(BF16)
    ws = [sw["wb"], sw["wc_re"], sw["wc_im"], sw["d"], sw["glu_v"], sw["glu_g"], perm, perm.T]
    kern = functools.partial(_s5_seq_kernel, lane_chunk=4 * V7X_LANES)
    return pl.pallas_call(
        kern,
        out_shape=(jax.ShapeDtypeStruct((nb, length, d), F32),
                   jax.ShapeDtypeStruct((nb, ns), F32), jax.ShapeDtypeStruct((nb, ns), F32)),
        grid=(length // ts,),
        in_specs=[pl.BlockSpec((nb, ts, din), lambda c: (0, c, 0)),
                  _const_spec((nb, ns)), _const_spec((nb, ns)), _const_spec((nb, ns)), _const_spec((nb, ns))]
                 + _s5_weight_specs(ws),
        out_specs=(pl.BlockSpec((nb, ts, d), lambda c: (0, c, 0)),
                   pl.BlockSpec((nb, ns), lambda c: (0, 0)), pl.BlockSpec((nb, ns), lambda c: (0, 0))),
        scratch_shapes=[pltpu.VMEM((rows, ns), F32) for _ in range(4)],
        compiler_params=_cparams(("arbitrary",), 40 * 2**20),
        name="s5_seq",
    )(s5_in3, h0re, h0im, a_re, a_im, *ws)


def _s5_rows_call(s5_in, h0re, h0im, a_re, a_im, sw):
    rows, din = s5_in.shape
    ns = a_re.shape[1]
    d = sw["glu_v"].shape[1]
    ws = [sw["wb"], sw["wc_re"], sw["wc_im"], sw["d"], sw["glu_v"], sw["glu_g"]]
    full = lambda shape: pl.BlockSpec(shape, lambda i: (0,) * len(shape))
    return pl.pallas_call(
        _s5_rows_kernel,
        out_shape=(jax.ShapeDtypeStruct((rows, d), F32),
                   jax.ShapeDtypeStruct((rows, ns), F32), jax.ShapeDtypeStruct((rows, ns), F32)),
        grid=(1,),
        in_specs=[full((rows, din)), full((rows, ns)), full((rows, ns)), full(a_re.shape), full(a_im.shape)]
                 + [full(w.shape) for w in ws],
        out_specs=(full((rows, d)), full((rows, ns)), full((rows, ns))),
        scratch_shapes=[pltpu.VMEM((rows, ns), F32) for _ in range(2)],
        compiler_params=_cparams(("arbitrary",), 40 * 2**20),
        name="s5_rows",
    )(s5_in, h0re, h0im, a_re, a_im, *ws)


def _headsum(x, ones_bd):
    hi = x.astype(BF16)
    lo = (x - hi.astype(F32)).astype(BF16)
    return _dot(hi, ones_bd) + _dot(lo, ones_bd)


def _rwkv_pre(cur, prev, p, ones_bd, width):
    mixed = cur + p["mu"] * (prev - cur)
    r = mixed[:, 0:width]
    k = mixed[:, width:2 * width]
    v = mixed[:, 2 * width:3 * width]
    wa = mixed[:, 3 * width:3 * width + p["w2p"].shape[0]]
    gd = mixed[:, 3 * width + p["w2p"].shape[0]:]
    lw = _dot(jnp.tanh(wa).astype(BF16), p["w2p"])
    la = _dot(wa.astype(BF16), p["a2p"])
    w = -_softplus(-(p["w0"] + lw)) - 0.5
    ld = -jnp.exp(w)
    a = jax.nn.sigmoid(p["a0"] + la)
    g = _dot(jax.nn.sigmoid(gd).astype(BF16), p["g2"])
    kk = k * p["k_k"]
    norm = jnp.sqrt(_headsum(kk * kk, ones_bd))
    kk = kk / jnp.maximum(norm, 1e-12)
    k2 = k * (1.0 + (a - 1.0) * p["k_a"])
    return r, ld, k2, v, -kk, kk * a, g


def _rwkv_post(y, r, k2, v, g, p, ones_bd, proj, head):
    inv = 1.0 / head
    mean = _headsum(y, ones_bd) * inv
    dlt = y - mean
    var = _headsum(dlt * dlt, ones_bd) * inv
    yn = dlt * lax.rsqrt(var + GN_EPS) * p["ln_w"] + p["ln_b"]
    bonus = _headsum(r * k2 * p["r_k"], ones_bd) * v
    o = ((yn + bonus) * g).astype(BF16)
    return _dot(o, proj)


def _wkv_block(r, ld, k, v, a, b, s_ref, y_ref, tril, head, chunk):
    rows, width = r.shape
    nsub = rows // chunk
    nh = width // head
    c = chunk
    pairs = [(s, h) for s in range(nsub) for h in range(nh)]
    ri = lax.broadcasted_iota(jnp.int32, (c, c), 0)
    ci = lax.broadcasted_iota(jnp.int32, (c, c), 1)
    strict = ri > ci
    incl = ri >= ci

    fac = []
    for s in range(nsub):
        rs = slice(s * c, (s + 1) * c)
        lds = ld[rs]
        hi, mid, lo = _split3(lds)
        cum = _dot(tril, hi) + _dot(tril, mid) + _dot(tril, lo)
        g_inv = jnp.exp(-cum)
        cum_end = cum[c - 1:c, :]
        g_rel = jnp.exp(cum_end - cum)
        fac.append(dict(a_t=a[rs] * jnp.exp(cum - lds), r_t=r[rs] * jnp.exp(cum), b_t=b[rs] * g_inv,
                        k_t=k[rs] * g_inv, b_e=(b[rs] * g_rel).astype(BF16), k_e=(k[rs] * g_rel).astype(BF16),
                        g_end=jnp.exp(cum_end), v=v[rs].astype(BF16)))

    hs = lambda h: slice(h * head, (h + 1) * head)
    ar, sc, vh = {}, {}, {}
    for s, h in pairs:
        f = fac[s]
        ar[s, h] = jnp.concatenate([f["a_t"][:, hs(h)], f["r_t"][:, hs(h)]], axis=0).astype(BF16)
        bk = jnp.concatenate([f["b_t"][:, hs(h)], f["k_t"][:, hs(h)]], axis=0).astype(BF16)
        sc[s, h] = _dot_nt(ar[s, h], bk)
        vh[s, h] = f["v"][:, hs(h)]
    pw, nm, m_rb, lv, mv, vk = {}, {}, {}, {}, {}, {}
    for s, h in pairs:
        x = sc[s, h]
        pw[s, h] = jnp.where(strict, x[:c, :c], 0.0)
        nm[s, h] = pw[s, h]
        l_ak = jnp.where(strict, x[:c, c:], 0.0).astype(BF16)
        m_rb[s, h] = jnp.where(incl, x[c:, :c], 0.0).astype(BF16)
        m_rk = jnp.where(incl, x[c:, c:], 0.0).astype(BF16)
        lv[s, h] = _dot(l_ak, vh[s, h])
        mv[s, h] = _dot(m_rk, vh[s, h])
        vk[s, h] = _dot_tn(vh[s, h], fac[s]["k_e"][:, hs(h)])
    for _ in range(int(math.log2(c)) - 1):
        for s, h in pairs:
            pb = pw[s, h].astype(BF16)
            pw[s, h] = _dot(pb, pb)
        for s, h in pairs:
            nm[s, h] = nm[s, h] + pw[s, h] + _dot(nm[s, h].astype(BF16), pw[s, h].astype(BF16))

    for s in range(nsub):
        s0 = [s_ref[h] for h in range(nh)]
        am = [_dot_nt(ar[s, h], s0[h].astype(BF16)) for h in range(nh)]
        ub = []
        for h in range(nh):
            w = am[h][:c] + lv[s, h]
            ub.append((w + _dot(nm[s, h].astype(BF16), w.astype(BF16))).astype(BF16))
        for h in range(nh):
            y_ref[s * c:(s + 1) * c, hs(h)] = am[h][c:] + _dot(m_rb[s, h], ub[h]) + mv[s, h]
            s_ref[h] = (s0[h] * fac[s]["g_end"][:, hs(h)] + _dot_tn(ub[h], fac[s]["b_e"][:, hs(h)])
                        + vk[s, h])


_RWKV_VEC_NAMES = ("w0", "a0", "k_k", "k_a", "r_k", "ln_w", "ln_b")
_RWKV_CONSTS = ("mu", "vecs", "w2p", "a2p", "g2")


def _rwkv_params(mu_ref, vecs_ref, w2p_ref, a2p_ref, g2_ref):
    p = {n: vecs_ref[i:i + 1, :] for i, n in enumerate(_RWKV_VEC_NAMES)}
    p.update(mu=mu_ref[...], w2p=w2p_ref[...], a2p=a2p_ref[...], g2=g2_ref[...])
    return p


def _rwkv_seq_kernel(cur_ref, shift0_ref, s0_ref, *rest, chunk, head):
    np_ = len(_RWKV_CONSTS)
    p = _rwkv_params(*rest[:np_])
    proj_ref, ones_ref, tril_ref = rest[np_:np_ + 3]
    out_ref, st_ref = rest[np_ + 3:np_ + 5]
    carry_scr, y_scr = rest[np_ + 5:]

    @pl.when(pl.program_id(1) == 0)
    def _():
        carry_scr[...] = shift0_ref[...]
        st_ref[...] = s0_ref[...]

    cur = cur_ref[...]
    rows, _ = cur.shape
    width = y_scr.shape[1]
    row = lax.broadcasted_iota(jnp.int32, cur.shape, 0)
    prev = jnp.where(row == 0, carry_scr[...], pltpu.roll(cur, 1, axis=0))
    carry_scr[...] = cur[rows - 1:rows, :]

    ones_bd = ones_ref[...]
    r, ld, k2, v, a, b, g = _rwkv_pre(cur, prev, p, ones_bd, width)

    _wkv_block(r, ld, k2, v, a, b, st_ref, y_scr, tril_ref[...], head, chunk)

    out_ref[...] = _rwkv_post(y_scr[...], r, k2, v, g, p, ones_bd, proj_ref[...], head)


def _rwkv_rows_pre_kernel(cur_ref, prev_ref, mu_ref, vecs_ref, w2p_ref, a2p_ref, g2_ref, ones_ref,
                          r_ref, k_ref, v_ref, g_ref, t_ref, *, width):
    p = _rwkv_params(mu_ref, vecs_ref, w2p_ref, a2p_ref, g2_ref)
    r, ld, k2, v, a, b, g = _rwkv_pre(cur_ref[...], prev_ref[...], p, ones_ref[...], width)
    r_ref[...] = r
    k_ref[...] = k2
    v_ref[...] = v
    g_ref[...] = g
    for i, val in enumerate((r, ld, k2, v, a, b)):
        t_ref[i] = val.T


def _wkv_step_kernel(t_ref, s_ref, y_ref, so_ref):
    r = t_ref[0]
    dec = jnp.exp(t_ref[1])
    k = t_ref[2]
    a = t_ref[4]
    b = t_ref[5]

    def body(i, carry):
        s = s_ref[i]
        sa = jnp.sum(s * a, axis=0, keepdims=True)
        s2 = s * dec + sa * b + t_ref[3, pl.ds(i, 1), :] * k
        so_ref[i] = s2
        y_ref[pl.ds(i, 1), :] = jnp.sum(s2 * r, axis=0, keepdims=True)
        return carry

    lax.fori_loop(0, s_ref.shape[0], body, 0, unroll=V7X_SUBLANES)


def _rwkv_rows_post_kernel(yt_ref, r_ref, k_ref, v_ref, g_ref, mu_ref, vecs_ref, w2p_ref, a2p_ref, g2_ref,
                           proj_ref, ones_ref, o_ref, *, head):
    p = _rwkv_params(mu_ref, vecs_ref, w2p_ref, a2p_ref, g2_ref)
    o_ref[...] = _rwkv_post(yt_ref[...].T, r_ref[...], k_ref[...], v_ref[...], g_ref[...], p,
                            ones_ref[...], proj_ref[...], head)


def _rwkv_param_list(rw):
    return [rw[n] for n in _RWKV_CONSTS]


def _rwkv_seq_call(cur, shift0, s0, rw, *, nb, length, tr):
    n_shift = cur.shape[1]
    nh, head = s0.shape[1], s0.shape[2]
    width = nh * head
    d = rw["proj"].shape[1]
    params = _rwkv_param_list(rw)
    chunk = min(WKV_CHUNK, tr)
    tril = (jnp.arange(chunk)[:, None] >= jnp.arange(chunk)[None, :]).astype(BF16)
    consts = params + [rw["proj"], rw["ones_bd"], tril]
    tiles = length // tr
    kern = functools.partial(_rwkv_seq_kernel, chunk=chunk, head=head)
    return pl.pallas_call(
        kern,
        out_shape=(jax.ShapeDtypeStruct((nb * length, d), F32),
                   jax.ShapeDtypeStruct((nb, nh, head, head), F32)),
        grid=(nb, tiles),
        in_specs=[pl.BlockSpec((tr, n_shift), lambda b, i: (b * tiles + i, 0)),
                  pl.BlockSpec((None, 1, n_shift), lambda b, i: (b, 0, 0)),
                  pl.BlockSpec((None, nh, head, head), lambda b, i: (b, 0, 0, 0))]
                 + [pl.BlockSpec(c.shape, lambda b, i, nd=c.ndim: (0,) * nd) for c in consts],
        out_specs=(pl.BlockSpec((tr, d), lambda b, i: (b * tiles + i, 0)),
                   pl.BlockSpec((None, nh, head, head), lambda b, i: (b, 0, 0, 0))),
        scratch_shapes=[pltpu.VMEM((1, n_shift), F32), pltpu.VMEM((tr, width), F32)],
        compiler_params=_cparams(("arbitrary", "arbitrary"), 40 * 2**20),
        name="rwkv_seq",
    )(cur, shift0.reshape(nb, 1, n_shift), s0, *consts)


def _rwkv_rows_call(cur, prev, s0_t, rw):
    rows, n_shift = cur.shape
    nh, head = s0_t.shape[0], s0_t.shape[1]
    width = nh * head
    d = rw["proj"].shape[1]
    params = _rwkv_param_list(rw)
    full = lambda shape: pl.BlockSpec(shape, lambda i: (0,) * len(shape))
    vec = jax.ShapeDtypeStruct((rows, width), F32)
    n_t = 6

    pre_consts = params + [rw["ones_bd"]]
    r, k2, v, g, t = pl.pallas_call(
        functools.partial(_rwkv_rows_pre_kernel, width=width),
        out_shape=(vec,) * 4 + (jax.ShapeDtypeStruct((n_t, width, rows), F32),),
        grid=(1,),
        in_specs=[full(cur.shape), full(prev.shape)] + [full(c.shape) for c in pre_consts],
        out_specs=(full((rows, width)),) * 4 + (full((n_t, width, rows)),),
        compiler_params=_cparams(("arbitrary",), 32 * 2**20),
        name="rwkv_rows_pre",
    )(cur, prev, *pre_consts)

    sblk = pl.BlockSpec((None, head, head, rows), lambda h: (h, 0, 0, 0))
    yt, s1_t = pl.pallas_call(
        _wkv_step_kernel,
        out_shape=(jax.ShapeDtypeStruct((width, rows), F32), jax.ShapeDtypeStruct(s0_t.shape, F32)),
        grid=(nh,),
        in_specs=[pl.BlockSpec((n_t, head, rows), lambda h: (0, h, 0)), sblk],
        out_specs=(pl.BlockSpec((head, rows), lambda h: (h, 0)), sblk),
        compiler_params=_cparams(("parallel",), 32 * 2**20),
        name="wkv_step",
    )(t, s0_t)

    post_consts = params + [rw["proj"], rw["ones_bd"]]
    out = pl.pallas_call(
        functools.partial(_rwkv_rows_post_kernel, head=head),
        out_shape=jax.ShapeDtypeStruct((rows, d), F32),
        grid=(1,),
        in_specs=[full((width, rows))] + [full((rows, width))] * 4 + [full(c.shape) for c in post_consts],
        out_specs=full((rows, d)),
        compiler_params=_cparams(("arbitrary",), 32 * 2**20),
        name="rwkv_rows_post",
    )(yt, r, k2, v, g, *post_consts)
    return out, s1_t


def _merge_kernel(h_ref, gates_ref, ys5_ref, yrw_ref, gt2_ref, wout_ref,
                  sh_ref, sc_ref, gt_ref, g_ref, w1_ref, w3_ref, w2_ref, gfin_ref, o_ref, *, n_chunks,
                  tiles_per_seq):
    row = functools.partial(_mod_row, tiles_per_seq=tiles_per_seq)
    d = h_ref.shape[1]
    gates = gates_ref[...]
    m = jax.nn.sigmoid(gates[:, :d]) * ys5_ref[...] + jax.nn.sigmoid(gates[:, d:]) * yrw_ref[...]
    h2 = h_ref[...] + row(gt2_ref) * _dot(m.astype(BF16), wout_ref[...])
    h3 = _ffn_core(h2, row(sh_ref), row(sc_ref), row(gt_ref), g_ref[...], w1_ref, w3_ref, w2_ref, n_chunks)
    o_ref[...] = _rms(h3, gfin_ref[...])


def _merge_call(h, gates, ys5, yrw, mods, ks, w_out, g, w1, w3, w2, g_fin, *, tm):
    rows, d = h.shape
    d_ff = w1.shape[1]
    kern = functools.partial(_merge_kernel, n_chunks=_ffn_chunks(d_ff), tiles_per_seq=mods.tiles_per_seq)
    row_spec = lambda w: pl.BlockSpec((tm, w), lambda i: (i, 0))
    weights = (3 * d * d_ff + d * d) * 2
    tiles = 2 * tm * 6 * d * 4 + 4 * tm * d_ff * 4 + 6 * tm * d * 4
    return pl.pallas_call(
        kern,
        out_shape=jax.ShapeDtypeStruct((rows, d), F32),
        grid=(rows // tm,),
        in_specs=[row_spec(d), row_spec(2 * d), row_spec(d), row_spec(d),
                  mods.spec(ks[0]), _const_spec(w_out.shape),
                  mods.spec(ks[1]), mods.spec(ks[2]), mods.spec(ks[3]),
                  _const_spec((1, d)), _const_spec(w1.shape), _const_spec(w3.shape), _const_spec(w2.shape),
                  _const_spec((1, d))],
        out_specs=row_spec(d),
        compiler_params=_cparams(("parallel",), weights + tiles + 8 * 2**20),
        name="merge_ffn2",
    )(h, gates, ys5, yrw, mods.mod, w_out, mods.mod, mods.mod, mods.mod, g.reshape(1, d), w1, w3, w2,
      g_fin.reshape(1, d))


def _pick_tile(n, target):
    t = min(n, target)
    while n % t:
        t -= V7X_SUBLANES
    return t


def _layer(x, mod, mod_row0, s5_re0, s5_im0, wkv0, shift0, w, *, sequential):
    nb, length, d = x.shape
    rows = nb * length
    g, n = s5_re0.shape[1], s5_re0.shape[2]
    ns = g * n
    n_shift = shift0.shape[1]
    s5_width = w["s5"]["d"].shape[1]
    widths = (s5_width, n_shift, 2 * d)

    tm = _pick_tile(length if sequential else rows, 512)
    tm2 = _pick_tile(tm, 256)
    mods = _ModRows(mod, mod_row0, nb, tm, length // tm if sequential else None)
    mods2 = _ModRows(mod, mod_row0, nb, tm2, length // tm2 if sequential else None)

    x2 = x.reshape(rows, d)
    h1 = _ffn_call(x2, mods, (0, 1, 2), w["g_ffn1"], *w["ffn1"], tm=tm)
    s5_in, cur, gates = _proj_call(h1, mods, (3, 4), w["g_mix"], w["w_in"], widths, tm=tm)

    h0re = s5_re0.reshape(nb, ns)
    h0im = s5_im0.reshape(nb, ns)
    if sequential:
        ts = V7X_MXU_DIM // nb
        y_s5, hre, him = _s5_seq_call(s5_in.reshape(nb, length, s5_width), h0re, h0im,
                                      w["a_re"], w["a_im"], w["s5"], ts=ts)
        y_s5 = y_s5.reshape(rows, d)
        tr = _pick_tile(length, 4 * WKV_CHUNK)
        y_rw, wkv1 = _rwkv_seq_call(cur, shift0, wkv0, w["rwkv"], nb=nb, length=length, tr=tr)
    else:
        y_s5, hre, him = _s5_rows_call(s5_in, h0re, h0im, w["a_re"], w["a_im"], w["s5"])
        y_rw, wkv1 = _rwkv_rows_call(cur, shift0, wkv0, w["rwkv"])

    y = _merge_call(h1, gates, y_s5, y_rw, mods2, (5, 6, 7, 8), w["w_out"],
                    w["g_ffn2"], *w["ffn2"], w["g_final"], tm=tm2)

    shift1 = cur.reshape(nb, length, n_shift)[:, -1]
    return (y.reshape(nb, length, d), hre.reshape(nb, g, n), him.reshape(nb, g, n), wkv1, shift1)


def kernel(x_prompt, x_sample, c_prompt, c_sample, state_s5_re, state_s5_im, state_wkv, state_shift, w_ada, b_ada, g_ffn1, g_mix, g_ffn2, g_final, ffn1_w1, ffn1_w3, ffn1_w2, ffn2_w1, ffn2_w3, ffn2_w2, w_in, mu_shift, s5_lam_re, s5_lam_im, s5_log_dt, s5_b_re, s5_b_im, s5_c_re, s5_c_im, s5_d, s5_glu_v, s5_glu_g, rwkv_w0, rwkv_w2, rwkv_a0, rwkv_a2, rwkv_g2, rwkv_k_k, rwkv_k_a, rwkv_r_k, rwkv_ln_w, rwkv_ln_b, rwkv_proj, w_out):
    bp, _, d = x_prompt.shape
    bs = x_sample.shape[0]
    g, n = s5_lam_re.shape
    nh, head = rwkv_r_k.shape
    width = nh * head
    n_shift = mu_shift.shape[0]
    assert bs % bp == 0, "sample rows come first in the modulation array; prompt rows must stay block-aligned"
    bf = lambda t: t.astype(BF16)

    a_re, a_im, wb, wc_re, wc_im = _s5_prep_call(
        s5_lam_re, s5_lam_im, s5_log_dt, jnp.swapaxes(s5_b_re, 1, 2), jnp.swapaxes(s5_b_im, 1, 2),
        s5_c_re, s5_c_im, bp)
    w_lora = rwkv_w2.shape[0]
    a_lora = rwkv_a2.shape[0]
    w2p = jnp.concatenate([rwkv_w2, jnp.zeros((a_lora, width), F32)], axis=0)
    a2p = jnp.concatenate([jnp.zeros((w_lora, width), F32), rwkv_a2], axis=0)
    vec_rows = [rwkv_w0, rwkv_a0, rwkv_k_k, rwkv_k_a, rwkv_r_k.reshape(width), rwkv_ln_w, rwkv_ln_b]
    vecs = jnp.stack(vec_rows + [jnp.zeros_like(rwkv_w0)] * (V7X_SUBLANES - len(vec_rows)))
    rw = dict(mu=mu_shift.reshape(1, n_shift), vecs=vecs, w2p=bf(w2p), a2p=bf(a2p), g2=bf(rwkv_g2),
              proj=bf(rwkv_proj),
              ones_bd=jnp.kron(jnp.eye(nh, dtype=F32), jnp.ones((head, head), F32)).astype(BF16))
    w = dict(
        g_ffn1=g_ffn1, g_mix=g_mix, g_ffn2=g_ffn2, g_final=g_final,
        ffn1=(bf(ffn1_w1), bf(ffn1_w3), bf(ffn1_w2)), ffn2=(bf(ffn2_w1), bf(ffn2_w3), bf(ffn2_w2)),
        w_in=bf(w_in), w_out=bf(w_out),
        a_re=a_re, a_im=a_im,
        s5=dict(wb=wb, wc_re=wc_re, wc_im=wc_im, d=s5_d.reshape(1, -1), glu_v=bf(s5_glu_v), glu_g=bf(s5_glu_g)),
        rwkv=rw,
    )

    mod = _mod_call(jnp.concatenate([c_sample, c_prompt], axis=0), bf(w_ada), b_ada)

    z_s5 = jnp.zeros((bp, g, n), state_s5_re.dtype)
    z_wkv = jnp.zeros((bp, nh, head, head), state_wkv.dtype)
    z_shift = jnp.zeros((bp, n_shift), state_shift.dtype)
    y_p, s5re_p, s5im_p, wkv_p, shift_p = _layer(x_prompt, mod, bs, z_s5, z_s5, z_wkv, z_shift, w, sequential=True)
    y_s, s5re_s, s5im_s, wkv_s_t, shift_s = _layer(x_sample, mod, 0, state_s5_re, state_s5_im,
                                                   jnp.transpose(state_wkv, (1, 2, 3, 0)), state_shift, w,
                                                   sequential=False)
    wkv_s = jnp.transpose(wkv_s_t, (3, 0, 1, 2))
    return (y_p, y_s, s5re_p, s5im_p, wkv_p, shift_p, s5re_s, s5im_s, wkv_s, shift_s)
```

```python
import functools
import math

import jax
import jax.numpy as jnp
from jax import lax
from jax.experimental import pallas as pl
from jax.experimental.pallas import tpu as pltpu

F32 = jnp.float32
BF16 = jnp.bfloat16

NORM_EPS = 1e-6
GN_EPS = 64e-5
N_MOD = 9

V7X_LANES = 128
V7X_SUBLANES = 8
V7X_MXU_DIM = 256
V7X_VMEM_BYTES = 64 * 1024 * 1024
VMEM_CAP_BYTES = V7X_VMEM_BYTES - 8 * 1024 * 1024

WKV_CHUNK = 64
WKV_BLOCK_CHUNKS = 8
WKV_SEQS = 8


def _cparams(semantics, vmem_bytes):
    return pltpu.CompilerParams(
        dimension_semantics=semantics,
        vmem_limit_bytes=int(min(vmem_bytes, VMEM_CAP_BYTES)),
    )


def _const_spec(shape):
    nd = len(shape)
    return pl.BlockSpec(shape, lambda *_: (0,) * nd, pipeline_mode=pl.Buffered(1))


def _dot(a, b):
    return jnp.dot(a, b, preferred_element_type=F32)


def _dot_nt(a, b):
    return lax.dot_general(a, b, (((1,), (1,)), ((), ())), preferred_element_type=F32)


def _dot_tn(a, b):
    return lax.dot_general(a, b, (((0,), (0,)), ((), ())), preferred_element_type=F32)


def _split3(x):
    hi = x.astype(BF16)
    r1 = x - hi.astype(F32)
    mid = r1.astype(BF16)
    lo = (r1 - mid.astype(F32)).astype(BF16)
    return hi, mid, lo


def _rms(x, g):
    ms = jnp.mean(x * x, axis=-1, keepdims=True)
    return x * lax.rsqrt(ms + NORM_EPS) * g


def _softplus(x):
    return jnp.maximum(x, 0.0) + jnp.log1p(jnp.exp(-jnp.abs(x)))


def _mod_kernel(c_ref, w_ref, b_ref, o_ref):
    c = c_ref[...]
    s = (c * jax.nn.sigmoid(c)).astype(BF16)
    o_ref[...] = _dot(s, w_ref[...].astype(BF16)) + b_ref[...]


def _mod_call(c, w_ada, b_ada):
    rows, d = c.shape
    n = w_ada.shape[1]
    return pl.pallas_call(
        _mod_kernel,
        out_shape=jax.ShapeDtypeStruct((n // d, rows, d), F32),
        grid=(n // d,),
        in_specs=[
            pl.BlockSpec((rows, d), lambda j: (0, 0)),
            pl.BlockSpec((d, d), lambda j: (0, j)),
            pl.BlockSpec((1, d), lambda j: (0, j)),
        ],
        out_specs=pl.BlockSpec((None, rows, d), lambda j: (j, 0, 0)),
        compiler_params=_cparams(("parallel",), 32 * 2**20),
        name="adaln_mod",
    )(c, w_ada, b_ada.reshape(1, n))


class _ModRows:
    def __init__(self, mod, row0, nrows, tm, tiles_per_seq):
        self.mod, self.row0, self.nrows, self.tm, self.tiles_per_seq = mod, row0, nrows, tm, tiles_per_seq

    def spec(self, k):
        d = self.mod.shape[2]
        if self.tiles_per_seq is None:
            blk0 = self.row0 // self.tm
            return pl.BlockSpec((None, self.tm, d), lambda i: (k, blk0 + i, 0))
        blk0 = self.row0 // self.nrows
        return pl.BlockSpec((None, self.nrows, d), lambda i: (k, blk0, 0))


def _mod_row(ref, tiles_per_seq):
    if tiles_per_seq is None:
        return ref[...]
    return ref[pl.ds(pl.program_id(0) // tiles_per_seq, 1), :]


def _ffn_core(x, sh, sc, gt, g, w1_ref, w3_ref, w2_ref, n_chunks):
    rows = x.shape[0]
    n_split = 2 if rows % (2 * V7X_SUBLANES) == 0 and sh.shape[0] == 1 else 1
    step = rows // n_split
    bounds = _ffn_bounds(w1_ref.shape[1], n_chunks)
    outs = []
    for r in range(n_split):
        xr = x[r * step:(r + 1) * step]
        u = (_rms(xr, g) * (1.0 + sc) + sh).astype(BF16)
        acc = None
        for lo, hi in zip(bounds[:-1], bounds[1:]):
            a = _dot(u, w1_ref[:, lo:hi])
            b = _dot(u, w3_ref[:, lo:hi])
            hm = (a * jax.nn.sigmoid(a) * b).astype(BF16)
            part = _dot(hm, w2_ref[lo:hi, :])
            acc = part if acc is None else acc + part
        outs.append(xr + (0.5 * gt) * acc)
    return outs[0] if n_split == 1 else jnp.concatenate(outs, axis=0)


def _ffn_bounds(d_ff, n_chunks):
    tiles = -(-d_ff // V7X_MXU_DIM)
    return [min(d_ff, ((tiles * c + n_chunks - 1) // n_chunks) * V7X_MXU_DIM) for c in range(n_chunks + 1)]


def _ffn_kernel(x_ref, sh_ref, sc_ref, gt_ref, g_ref, w1_ref, w3_ref, w2_ref, o_ref, *, n_chunks, tiles_per_seq):
    row = functools.partial(_mod_row, tiles_per_seq=tiles_per_seq)
    o_ref[...] = _ffn_core(x_ref[...], row(sh_ref), row(sc_ref), row(gt_ref), g_ref[...],
                           w1_ref, w3_ref, w2_ref, n_chunks)


def _ffn_chunks(d_ff):
    return 2 if d_ff % (2 * V7X_LANES) == 0 else 1


def _ffn_call(x2, mods, ks, g, w1, w3, w2, *, tm):
    rows, d = x2.shape
    d_ff = w1.shape[1]
    kern = functools.partial(_ffn_kernel, n_chunks=_ffn_chunks(d_ff), tiles_per_seq=mods.tiles_per_seq)
    weights = 3 * d * d_ff * 2
    tiles = 4 * tm * d * 4 + 4 * tm * d_ff * 4
    return pl.pallas_call(
        kern,
        out_shape=jax.ShapeDtypeStruct((rows, d), F32),
        grid=(rows // tm,),
        in_specs=[
            pl.BlockSpec((tm, d), lambda i: (i, 0)),
            mods.spec(ks[0]), mods.spec(ks[1]), mods.spec(ks[2]),
            _const_spec((1, d)),
            _const_spec(w1.shape), _const_spec(w3.shape), _const_spec(w2.shape),
        ],
        out_specs=pl.BlockSpec((tm, d), lambda i: (i, 0)),
        compiler_params=_cparams(("parallel",), weights + tiles + 8 * 2**20),
        name="ffn1",
    )(x2, mods.mod, mods.mod, mods.mod, g.reshape(1, d), w1, w3, w2)


def _proj_kernel(h_ref, sh_ref, sc_ref, g_ref, win_ref, s5_ref, cur_ref, gates_ref, *, tiles_per_seq):
    row = functools.partial(_mod_row, tiles_per_seq=tiles_per_seq)
    u = (_rms(h_ref[...], g_ref[...]) * (1.0 + row(sc_ref)) + row(sh_ref)).astype(BF16)
    n0 = s5_ref.shape[1]
    n1 = n0 + cur_ref.shape[1]
    s5_ref[...] = _dot(u, win_ref[:, :n0])
    cur_ref[...] = _dot(u, win_ref[:, n0:n1])
    gates_ref[...] = _dot(u, win_ref[:, n1:])


def _proj_call(h, mods, ks, g, w_in, widths, *, tm):
    rows, d = h.shape
    n_in = w_in.shape[1]
    return pl.pallas_call(
        functools.partial(_proj_kernel, tiles_per_seq=mods.tiles_per_seq),
        out_shape=tuple(jax.ShapeDtypeStruct((rows, w), F32) for w in widths),
        grid=(rows // tm,),
        in_specs=[
            pl.BlockSpec((tm, d), lambda i: (i, 0)),
            mods.spec(ks[0]), mods.spec(ks[1]),
            _const_spec((1, d)),
            _const_spec(w_in.shape),
        ],
        out_specs=tuple(pl.BlockSpec((tm, w), lambda i: (i, 0)) for w in widths),
        compiler_params=_cparams(("parallel",), d * n_in * 2 + 3 * tm * (d + n_in) * 4 + 8 * 2**20),
        name="mix_proj",
    )(h, mods.mod, mods.mod, g.reshape(1, d), w_in)


def _s5_prep_kernel(lre_ref, lim_ref, ldt_ref, bre_ref, bim_ref, cre_ref, cim_ref,
                    are_ref, aim_ref, wb_ref, wcre_ref, wcim_ref, *, gpb):
    lre = lre_ref[...]
    lim = lim_ref[...]
    dt = jnp.exp(ldt_ref[...])
    mag = jnp.exp(lre * dt)
    ang = lim * dt
    a_re = mag * jnp.cos(ang)
    a_im = mag * jnp.sin(ang)
    den = lre * lre + lim * lim
    num_re = a_re - 1.0
    k_re = (num_re * lre + a_im * lim) / den
    k_im = (a_im * lre - num_re * lim) / den
    b_re = bre_ref[...]
    b_im = bim_ref[...]
    bb_re = (k_re * b_re - k_im * b_im).astype(BF16)
    bb_im = (k_re * b_im + k_im * b_re).astype(BF16)
    c_re = cre_ref[...].astype(BF16)
    c_im = (-cim_ref[...]).astype(BF16)
    g, c, n = b_re.shape
    nb = are_ref.shape[0]
    s = gpb * n
    wb_ref[...] = jnp.zeros(wb_ref.shape, BF16)
    wcre_ref[...] = jnp.zeros(wcre_ref.shape, BF16)
    wcim_ref[...] = jnp.zeros(wcim_ref.shape, BF16)
    for gi in range(g):
        k, gl = divmod(gi, gpb)
        rows = slice(gl * c, (gl + 1) * c)
        cols = slice(gl * n, (gl + 1) * n)
        are_ref[:, gi * n:(gi + 1) * n] = jnp.broadcast_to(a_re[gi], (nb, n))
        aim_ref[:, gi * n:(gi + 1) * n] = jnp.broadcast_to(a_im[gi], (nb, n))
        wb_ref[k, rows, cols] = bb_re[gi]
        wb_ref[k, rows, s + gl * n:s + (gl + 1) * n] = bb_im[gi]
        wcre_ref[k, rows, cols] = c_re[gi]
        wcim_ref[k, rows, cols] = c_im[gi]


def _s5_prep_call(lam_re, lam_im, log_dt, b_re_t, b_im_t, c_re, c_im, nb):
    g, n = lam_re.shape
    c = b_re_t.shape[1]
    gpb = max(1, min(g, V7X_MXU_DIM // c))
    nk = g // gpb
    return pl.pallas_call(
        functools.partial(_s5_prep_kernel, gpb=gpb),
        out_shape=(jax.ShapeDtypeStruct((nb, g * n), F32), jax.ShapeDtypeStruct((nb, g * n), F32),
                   jax.ShapeDtypeStruct((nk, gpb * c, 2 * gpb * n), BF16),
                   jax.ShapeDtypeStruct((nk, gpb * c, gpb * n), BF16),
                   jax.ShapeDtypeStruct((nk, gpb * c, gpb * n), BF16)),
        name="s5_discretise",
    )(lam_re.reshape(g, 1, n), lam_im.reshape(g, 1, n), log_dt.reshape(g, 1, 1), b_re_t, b_im_t, c_re, c_im)


def _s5_input_matmul(ub, wb_ref, bure_ref, buim_ref):
    nk, kin, two_s = wb_ref.shape
    s = two_s // 2
    for k in range(nk):
        res = _dot(ub[:, k * kin:(k + 1) * kin], wb_ref[k])
        bure_ref[:, k * s:(k + 1) * s] = res[:, :s]
        buim_ref[:, k * s:(k + 1) * s] = res[:, s:]


def _s5_output(xre, xim, u, wcre_ref, wcim_ref, d_row):
    nk, kout, s = wcre_ref.shape
    xre_b = xre.astype(BF16)
    xim_b = xim.astype(BF16)
    ys = []
    for k in range(nk):
        ys.append(_dot_nt(xre_b[:, k * s:(k + 1) * s], wcre_ref[k])
                  + _dot_nt(xim_b[:, k * s:(k + 1) * s], wcim_ref[k]))
    y = jnp.concatenate(ys, axis=-1) + d_row * u
    return jax.nn.gelu(y)


def _s5_glu(zb, gv_ref, gg_ref):
    return _dot(zb, gv_ref[...]) * jax.nn.sigmoid(_dot(zb, gg_ref[...]))


def _s5_seq_kernel(u_ref, h0re_ref, h0im_ref, are_ref, aim_ref, wb_ref, wcre_ref, wcim_ref, d_ref,
                   gv_ref, gg_ref, perm_ref, permt_ref,
                   o_ref, hre_ref, him_ref,
                   bure_scr, buim_scr, xre_scr, xim_scr, *, lane_chunk):
    nb, ts, din = u_ref.shape
    rows = nb * ts

    @pl.when(pl.program_id(0) == 0)
    def _():
        hre_ref[...] = h0re_ref[...]
        him_ref[...] = h0im_ref[...]

    u_bm = u_ref[...].reshape(rows, din)
    hi, mid, lo = _split3(u_bm)
    perm = perm_ref[...]
    u_hi_tm = _dot(perm, hi)
    u_tm = u_hi_tm + _dot(perm, mid) + _dot(perm, lo)
    _s5_input_matmul(u_hi_tm.astype(BF16), wb_ref, bure_scr, buim_scr)

    ns = are_ref.shape[1]
    for lc in range(ns // lane_chunk):
        sl = slice(lc * lane_chunk, (lc + 1) * lane_chunk)
        ar = are_ref[:, sl]
        ai = aim_ref[:, sl]
        sr = hre_ref[:, sl]
        si = him_ref[:, sl]
        for t in range(ts):
            rs = slice(t * nb, (t + 1) * nb)
            nr = ar * sr - ai * si + bure_scr[rs, sl]
            ni = ar * si + ai * sr + buim_scr[rs, sl]
            xre_scr[rs, sl] = nr
            xim_scr[rs, sl] = ni
            sr, si = nr, ni
        hre_ref[:, sl] = sr
        him_ref[:, sl] = si

    z_tm = _s5_output(xre_scr[...], xim_scr[...], u_tm, wcre_ref, wcim_ref, d_ref[...])
    z_bm = _dot(permt_ref[...], z_tm.astype(BF16)).astype(BF16)
    o_ref[...] = _s5_glu(z_bm, gv_ref, gg_ref).reshape(nb, ts, o_ref.shape[2])


def _s5_rows_kernel(u_ref, h0re_ref, h0im_ref, are_ref, aim_ref, wb_ref, wcre_ref, wcim_ref, d_ref,
                    gv_ref, gg_ref, o_ref, hre_ref, him_ref, bure_scr, buim_scr):
    u = u_ref[...]
    _s5_input_matmul(u.astype(BF16), wb_ref, bure_scr, buim_scr)
    ar = are_ref[0:1, :]
    ai = aim_ref[0:1, :]
    sr = h0re_ref[...]
    si = h0im_ref[...]
    nr = ar * sr - ai * si + bure_scr[...]
    ni = ar * si + ai * sr + buim_scr[...]
    hre_ref[...] = nr
    him_ref[...] = ni
    z = _s5_output(nr, ni, u, wcre_ref, wcim_ref, d_ref[...])
    o_ref[...] = _s5_glu(z.astype(BF16), gv_ref, gg_ref)


def _s5_weight_specs(ws):
    return [_const_spec(w.shape) for w in ws]


def _s5_seq_call(s5_in3, h0re, h0im, a_re, a_im, sw, *, ts):
    nb, length, din = s5_in3.shape
    ns = a_re.shape[1]
    d = sw["glu_v"].shape[1]
    rows = nb * ts
    idx = jnp.arange(rows)
    src = (idx % nb) * ts + idx // nb
    perm = (src[:, None] == jnp.arange(rows)[None, :]).astype(BF16)
    ws = [sw["wb"], sw["wc_re"], sw["wc_im"], sw["d"], sw["glu_v"], sw["glu_g"], perm, perm.T]
    kern = functools.partial(_s5_seq_kernel, lane_chunk=4 * V7X_LANES)
    return pl.pallas_call(
        kern,
        out_shape=(jax.ShapeDtypeStruct((nb, length, d), F32),
                   jax.ShapeDtypeStruct((nb, ns), F32), jax.ShapeDtypeStruct((nb, ns), F32)),
        grid=(length // ts,),
        in_specs=[pl.BlockSpec((nb, ts, din), lambda c: (0, c, 0)),
                  _const_spec((nb, ns)), _const_spec((nb, ns)), _const_spec((nb, ns)), _const_spec((nb, ns))]
                 + _s5_weight_specs(ws),
        out_specs=(pl.BlockSpec((nb, ts, d), lambda c: (0, c, 0)),
                   pl.BlockSpec((nb, ns), lambda c: (0, 0)), pl.BlockSpec((nb, ns), lambda c: (0, 0))),
        scratch_shapes=[pltpu.VMEM((rows, ns), F32) for _ in range(4)],
        compiler_params=_cparams(("arbitrary",), 40 * 2**20),
        name="s5_seq",
    )(s5_in3, h0re, h0im, a_re, a_im, *ws)


def _s5_rows_call(s5_in, h0re, h0im, a_re, a_im, sw):
    rows, din = s5_in.shape
    ns = a_re.shape[1]
    d = sw["glu_v"].shape[1]
    ws = [sw["wb"], sw["wc_re"], sw["wc_im"], sw["d"], sw["glu_v"], sw["glu_g"]]
    full = lambda shape: pl.BlockSpec(shape, lambda i: (0,) * len(shape))
    return pl.pallas_call(
        _s5_rows_kernel,
        out_shape=(jax.ShapeDtypeStruct((rows, d), F32),
                   jax.ShapeDtypeStruct((rows, ns), F32), jax.ShapeDtypeStruct((rows, ns), F32)),
        grid=(1,),
        in_specs=[full((rows, din)), full((rows, ns)), full((rows, ns)), full(a_re.shape), full(a_im.shape)]
                 + [full(w.shape) for w in ws],
        out_specs=(full((rows, d)), full((rows, ns)), full((rows, ns))),
        scratch_shapes=[pltpu.VMEM((rows, ns), F32) for _ in range(2)],
        compiler_params=_cparams(("arbitrary",), 40 * 2**20),
        name="s5_rows",
    )(s5_in, h0re, h0im, a_re, a_im, *ws)


def _headsum(x, ones_bd):
    hi = x.astype(BF16)
    lo = (x - hi.astype(F32)).astype(BF16)
    return _dot(hi, ones_bd) + _dot(lo, ones_bd)


def _rwkv_pre(cur, prev, p, ones_bd, width):
    mixed = cur + p["mu"] * (prev - cur)
    r = mixed[:, 0:width]
    k = mixed[:, width:2 * width]
    v = mixed[:, 2 * width:3 * width]
    wa = mixed[:, 3 * width:3 * width + p["w2p"].shape[0]]
    gd = mixed[:, 3 * width + p["w2p"].shape[0]:]
    lw = _dot(jnp.tanh(wa).astype(BF16), p["w2p"])
    la = _dot(wa.astype(BF16), p["a2p"])
    w = -_softplus(-(p["w0"] + lw)) - 0.5
    ld = -jnp.exp(w)
    a = jax.nn.sigmoid(p["a0"] + la)
    g = _dot(jax.nn.sigmoid(gd).astype(BF16), p["g2"])
    kk = k * p["k_k"]
    norm = jnp.sqrt(_headsum(kk * kk, ones_bd))
    kk = kk / jnp.maximum(norm, 1e-12)
    k2 = k * (1.0 + (a - 1.0) * p["k_a"])
    return r, ld, k2, v, -kk, kk * a, g


def _rwkv_post(y, r, k2, v, g, p, ones_bd, proj, head):
    inv = 1.0 / head
    mean = _headsum(y, ones_bd) * inv
    dlt = y - mean
    var = _headsum(dlt * dlt, ones_bd) * inv
    yn = dlt * lax.rsqrt(var + GN_EPS) * p["ln_w"] + p["ln_b"]
    bonus = _headsum(r * k2 * p["r_k"], ones_bd) * v
    o = ((yn + bonus) * g).astype(BF16)
    return _dot(o, proj)


def _wkv_block(r, ld, k, v, a, b, s_ref, y_ref, tril, head, chunk, nseq):
    rows, width = r.shape
    nsub = rows // chunk
    nh = width // head
    c = chunk
    pairs = [(s, h) for s in range(nsub) for h in range(nh)]
    ri = lax.broadcasted_iota(jnp.int32, (c, c), 0)
    ci = lax.broadcasted_iota(jnp.int32, (c, c), 1)
    strict = ri > ci
    incl = ri >= ci

    fac = []
    for s in range(nsub):
        rs = slice(s * c, (s + 1) * c)
        lds = ld[rs]
        hi, mid, lo = _split3(lds)
        cum = _dot(tril, hi) + _dot(tril, mid) + _dot(tril, lo)
        g_inv = jnp.exp(-cum)
        cum_end = cum[c - 1:c, :]
        g_rel = jnp.exp(cum_end - cum)
        fac.append(dict(a_t=a[rs] * jnp.exp(cum - lds), r_t=r[rs] * jnp.exp(cum), b_t=b[rs] * g_inv,
                        k_t=k[rs] * g_inv, b_e=(b[rs] * g_rel).astype(BF16), k_e=(k[rs] * g_rel).astype(BF16),
                        g_end=jnp.exp(cum_end), v=v[rs].astype(BF16)))

    hs = lambda h: slice(h * head, (h + 1) * head)
    ar, sc, vh = {}, {}, {}
    for s, h in pairs:
        f = fac[s]
        ar[s, h] = jnp.concatenate([f["a_t"][:, hs(h)], f["r_t"][:, hs(h)]], axis=0).astype(BF16)
        bk = jnp.concatenate([f["b_t"][:, hs(h)], f["k_t"][:, hs(h)]], axis=0).astype(BF16)
        sc[s, h] = _dot_nt(ar[s, h], bk)
        vh[s, h] = f["v"][:, hs(h)]
    pw, nm, m_rb, lv, mv, vk = {}, {}, {}, {}, {}, {}
    for s, h in pairs:
        x = sc[s, h]
        pw[s, h] = jnp.where(strict, x[:c, :c], 0.0)
        nm[s, h] = pw[s, h]
        l_ak = jnp.where(strict, x[:c, c:], 0.0).astype(BF16)
        m_rb[s, h] = jnp.where(incl, x[c:, :c], 0.0).astype(BF16)
        m_rk = jnp.where(incl, x[c:, c:], 0.0).astype(BF16)
        lv[s, h] = _dot(l_ak, vh[s, h])
        mv[s, h] = _dot(m_rk, vh[s, h])
        vk[s, h] = _dot_tn(vh[s, h], fac[s]["k_e"][:, hs(h)])
    for _ in range(int(math.log2(c)) - 1):
        for s, h in pairs:
            pb = pw[s, h].astype(BF16)
            pw[s, h] = _dot(pb, pb)
        for s, h in pairs:
            nm[s, h] = nm[s, h] + pw[s, h] + _dot(nm[s, h].astype(BF16), pw[s, h].astype(BF16))

    per_seq = nsub // nseq
    for t in range(per_seq):
        sq = [(q * per_seq + t, q, h) for q in range(nseq) for h in range(nh)]
        s0 = {(q, h): s_ref[q, h] for _, q, h in sq}
        am = {(q, h): _dot_nt(ar[s, h], s0[q, h].astype(BF16)) for s, q, h in sq}
        ub = {}
        for s, q, h in sq:
            w = am[q, h][:c] + lv[s, h]
            ub[q, h] = (w + _dot(nm[s, h].astype(BF16), w.astype(BF16))).astype(BF16)
        for s, q, h in sq:
            y_ref[s * c:(s + 1) * c, hs(h)] = am[q, h][c:] + _dot(m_rb[s, h], ub[q, h]) + mv[s, h]
            s_ref[q, h] = (s0[q, h] * fac[s]["g_end"][:, hs(h)] + _dot_tn(ub[q, h], fac[s]["b_e"][:, hs(h)])
                           + vk[s, h])


_RWKV_VEC_NAMES = ("w0", "a0", "k_k", "k_a", "r_k", "ln_w", "ln_b")
_RWKV_CONSTS = ("mu", "vecs", "w2p", "a2p", "g2")


def _rwkv_params(mu_ref, vecs_ref, w2p_ref, a2p_ref, g2_ref):
    p = {n: vecs_ref[i:i + 1, :] for i, n in enumerate(_RWKV_VEC_NAMES)}
    p.update(mu=mu_ref[...], w2p=w2p_ref[...], a2p=a2p_ref[...], g2=g2_ref[...])
    return p


def _rwkv_seq_kernel(cur_ref, shift0_ref, s0_ref, *rest, chunk, head):
    np_ = len(_RWKV_CONSTS)
    p = _rwkv_params(*rest[:np_])
    proj_ref, ones_ref, tril_ref = rest[np_:np_ + 3]
    out_ref, st_ref = rest[np_ + 3:np_ + 5]
    carry_scr, y_scr = rest[np_ + 5:]

    @pl.when(pl.program_id(1) == 0)
    def _():
        carry_scr[...] = shift0_ref[...]
        st_ref[...] = s0_ref[...]

    nseq, tr, n_shift = cur_ref.shape
    width = y_scr.shape[1]
    row = lax.broadcasted_iota(jnp.int32, (tr, n_shift), 0)
    prevs = []
    for q in range(nseq):
        cq = cur_ref[q]
        prevs.append(jnp.where(row == 0, carry_scr[q], pltpu.roll(cq, 1, axis=0)))
        carry_scr[q] = cq[tr - 1:tr, :]
    cur = cur_ref[...].reshape(nseq * tr, n_shift)
    prev = jnp.concatenate(prevs, axis=0)

    ones_bd = ones_ref[...]
    r, ld, k2, v, a, b, g = _rwkv_pre(cur, prev, p, ones_bd, width)

    _wkv_block(r, ld, k2, v, a, b, st_ref, y_scr, tril_ref[...], head, chunk, nseq)

    out = _rwkv_post(y_scr[...], r, k2, v, g, p, ones_bd, proj_ref[...], head)
    out_ref[...] = out.reshape(nseq, tr, out.shape[1])


def _rwkv_rows_pre_kernel(cur_ref, prev_ref, mu_ref, vecs_ref, w2p_ref, a2p_ref, g2_ref, ones_ref,
                          r_ref, k_ref, v_ref, g_ref, t_ref, *, width):
    p = _rwkv_params(mu_ref, vecs_ref, w2p_ref, a2p_ref, g2_ref)
    r, ld, k2, v, a, b, g = _rwkv_pre(cur_ref[...], prev_ref[...], p, ones_ref[...], width)
    r_ref[...] = r
    k_ref[...] = k2
    v_ref[...] = v
    g_ref[...] = g
    for i, val in enumerate((r, ld, k2, v, a, b)):
        t_ref[i] = val.T


def _wkv_step_kernel(t_ref, s_ref, y_ref, so_ref):
    r = t_ref[0]
    dec = jnp.exp(t_ref[1])
    k = t_ref[2]
    a = t_ref[4]
    b = t_ref[5]

    def body(i, carry):
        s = s_ref[i]
        sa = jnp.sum(s * a, axis=0, keepdims=True)
        s2 = s * dec + sa * b + t_ref[3, pl.ds(i, 1), :] * k
        so_ref[i] = s2
        y_ref[pl.ds(i, 1), :] = jnp.sum(s2 * r, axis=0, keepdims=True)
        return carry

    lax.fori_loop(0, s_ref.shape[0], body, 0, unroll=V7X_SUBLANES)


def _rwkv_rows_post_kernel(yt_ref, r_ref, k_ref, v_ref, g_ref, mu_ref, vecs_ref, w2p_ref, a2p_ref, g2_ref,
                           proj_ref, ones_ref, o_ref, *, head):
    p = _rwkv_params(mu_ref, vecs_ref, w2p_ref, a2p_ref, g2_ref)
    o_ref[...] = _rwkv_post(yt_ref[...].T, r_ref[...], k_ref[...], v_ref[...], g_ref[...], p,
                            ones_ref[...], proj_ref[...], head)


def _rwkv_param_list(rw):
    return [rw[n] for n in _RWKV_CONSTS]


def _rwkv_seq_call(cur3, shift0, s0, rw, *, tr, nseq):
    nb, length, n_shift = cur3.shape
    nh, head = s0.shape[1], s0.shape[2]
    width = nh * head
    d = rw["proj"].shape[1]
    params = _rwkv_param_list(rw)
    chunk = min(WKV_CHUNK, tr)
    tril = (jnp.arange(chunk)[:, None] >= jnp.arange(chunk)[None, :]).astype(BF16)
    consts = params + [rw["proj"], rw["ones_bd"], tril]
    kern = functools.partial(_rwkv_seq_kernel, chunk=chunk, head=head)
    return pl.pallas_call(
        kern,
        out_shape=(jax.ShapeDtypeStruct((nb, length, d), F32),
                   jax.ShapeDtypeStruct((nb, nh, head, head), F32)),
        grid=(nb // nseq, length // tr),
        in_specs=[pl.BlockSpec((nseq, tr, n_shift), lambda q, i: (q, i, 0)),
                  pl.BlockSpec((nseq, 1, n_shift), lambda q, i: (q, 0, 0)),
                  pl.BlockSpec((nseq, nh, head, head), lambda q, i: (q, 0, 0, 0))]
                 + [pl.BlockSpec(c.shape, lambda q, i, nd=c.ndim: (0,) * nd) for c in consts],
        out_specs=(pl.BlockSpec((nseq, tr, d), lambda q, i: (q, i, 0)),
                   pl.BlockSpec((nseq, nh, head, head), lambda q, i: (q, 0, 0, 0))),
        scratch_shapes=[pltpu.VMEM((nseq, 1, n_shift), F32), pltpu.VMEM((nseq * tr, width), F32)],
        compiler_params=_cparams(("arbitrary", "arbitrary"), 40 * 2**20),
        name="rwkv_seq",
    )(cur3, shift0.reshape(nb, 1, n_shift), s0, *consts)


def _rwkv_rows_call(cur, prev, s0_t, rw):
    rows, n_shift = cur.shape
    nh, head = s0_t.shape[0], s0_t.shape[1]
    width = nh * head
    d = rw["proj"].shape[1]
    params = _rwkv_param_list(rw)
    full = lambda shape: pl.BlockSpec(shape, lambda i: (0,) * len(shape))
    vec = jax.ShapeDtypeStruct((rows, width), F32)
    n_t = 6

    pre_consts = params + [rw["ones_bd"]]
    r, k2, v, g, t = pl.pallas_call(
        functools.partial(_rwkv_rows_pre_kernel, width=width),
        out_shape=(vec,) * 4 + (jax.ShapeDtypeStruct((n_t, width, rows), F32),),
        grid=(1,),
        in_specs=[full(cur.shape), full(prev.shape)] + [full(c.shape) for c in pre_consts],
        out_specs=(full((rows, width)),) * 4 + (full((n_t, width, rows)),),
        compiler_params=_cparams(("arbitrary",), 32 * 2**20),
        name="rwkv_rows_pre",
    )(cur, prev, *pre_consts)

    sblk = pl.BlockSpec((None, head, head, rows), lambda h: (h, 0, 0, 0))
    yt, s1_t = pl.pallas_call(
        _wkv_step_kernel,
        out_shape=(jax.ShapeDtypeStruct((width, rows), F32), jax.ShapeDtypeStruct(s0_t.shape, F32)),
        grid=(nh,),
        in_specs=[pl.BlockSpec((n_t, head, rows), lambda h: (0, h, 0)), sblk],
        out_specs=(pl.BlockSpec((head, rows), lambda h: (h, 0)), sblk),
        compiler_params=_cparams(("parallel",), 32 * 2**20),
        name="wkv_step",
    )(t, s0_t)

    post_consts = params + [rw["proj"], rw["ones_bd"]]
    out = pl.pallas_call(
        functools.partial(_rwkv_rows_post_kernel, head=head),
        out_shape=jax.ShapeDtypeStruct((rows, d), F32),
        grid=(1,),
        in_specs=[full((width, rows))] + [full((rows, width))] * 4 + [full(c.shape) for c in post_consts],
        out_specs=full((rows, d)),
        compiler_params=_cparams(("arbitrary",), 32 * 2**20),
        name="rwkv_rows_post",
    )(yt, r, k2, v, g, *post_consts)
    return out, s1_t


def _merge_kernel(h_ref, gates_ref, ys5_ref, yrw_ref, gt2_ref, wout_ref,
                  sh_ref, sc_ref, gt_ref, g_ref, w1_ref, w3_ref, w2_ref, gfin_ref, o_ref, *, n_chunks,
                  tiles_per_seq):
    row = functools.partial(_mod_row, tiles_per_seq=tiles_per_seq)
    d = h_ref.shape[1]
    gates = gates_ref[...]
    m = jax.nn.sigmoid(gates[:, :d]) * ys5_ref[...] + jax.nn.sigmoid(gates[:, d:]) * yrw_ref[...]
    h2 = h_ref[...] + row(gt2_ref) * _dot(m.astype(BF16), wout_ref[...])
    h3 = _ffn_core(h2, row(sh_ref), row(sc_ref), row(gt_ref), g_ref[...], w1_ref, w3_ref, w2_ref, n_chunks)
    o_ref[...] = _rms(h3, gfin_ref[...])


def _merge_call(h, gates, ys5, yrw, mods, ks, w_out, g, w1, w3, w2, g_fin, *, tm):
    rows, d = h.shape
    d_ff = w1.shape[1]
    kern = functools.partial(_merge_kernel, n_chunks=_ffn_chunks(d_ff), tiles_per_seq=mods.tiles_per_seq)
    row_spec = lambda w: pl.BlockSpec((tm, w), lambda i: (i, 0))
    weights = (3 * d * d_ff + d * d) * 2
    tiles = 2 * tm * 6 * d * 4 + 4 * tm * d_ff * 4 + 6 * tm * d * 4
    return pl.pallas_call(
        kern,
        out_shape=jax.ShapeDtypeStruct((rows, d), F32),
        grid=(rows // tm,),
        in_specs=[row_spec(d), row_spec(2 * d), row_spec(d), row_spec(d),
                  mods.spec(ks[0]), _const_spec(w_out.shape),
                  mods.spec(ks[1]), mods.spec(ks[2]), mods.spec(ks[3]),
                  _const_spec((1, d)), _const_spec(w1.shape), _const_spec(w3.shape), _const_spec(w2.shape),
                  _const_spec((1, d))],
        out_specs=row_spec(d),
        compiler_params=_cparams(("parallel",), weights + tiles + 8 * 2**20),
        name="merge_ffn2",
    )(h, gates, ys5, yrw, mods.mod, w_out, mods.mod, mods.mod, mods.mod, g.reshape(1, d), w1, w3, w2,
      g_fin.reshape(1, d))


def _pick_tile(n, target):
    t = min(n, target)
    while n % t:
        t -= V7X_SUBLANES
    return t


def _layer(x, mod, mod_row0, s5_re0, s5_im0, wkv0, shift0, w, *, sequential):
    nb, length, d = x.shape
    rows = nb * length
    g, n = s5_re0.shape[1], s5_re0.shape[2]
    ns = g * n
    n_shift = shift0.shape[1]
    s5_width = w["s5"]["d"].shape[1]
    widths = (s5_width, n_shift, 2 * d)

    tm = _pick_tile(length if sequential else rows, 512)
    tm2 = _pick_tile(tm, 512)
    mods = _ModRows(mod, mod_row0, nb, tm, length // tm if sequential else None)
    mods2 = _ModRows(mod, mod_row0, nb, tm2, length // tm2 if sequential else None)

    x2 = x.reshape(rows, d)
    h1 = _ffn_call(x2, mods, (0, 1, 2), w["g_ffn1"], *w["ffn1"], tm=tm)
    s5_in, cur, gates = _proj_call(h1, mods, (3, 4), w["g_mix"], w["w_in"], widths, tm=tm)

    h0re = s5_re0.reshape(nb, ns)
    h0im = s5_im0.reshape(nb, ns)
    if sequential:
        ts = V7X_MXU_DIM // nb
        y_s5, hre, him = _s5_seq_call(s5_in.reshape(nb, length, s5_width), h0re, h0im,
                                      w["a_re"], w["a_im"], w["s5"], ts=ts)
        y_s5 = y_s5.reshape(rows, d)
        nseq = WKV_SEQS if nb % WKV_SEQS == 0 else 1
        tr = _pick_tile(length, WKV_BLOCK_CHUNKS * WKV_CHUNK // nseq)
        y_rw, wkv1 = _rwkv_seq_call(cur.reshape(nb, length, n_shift), shift0, wkv0, w["rwkv"], tr=tr, nseq=nseq)
        y_rw = y_rw.reshape(rows, d)
    else:
        y_s5, hre, him = _s5_rows_call(s5_in, h0re, h0im, w["a_re"], w["a_im"], w["s5"])
        y_rw, wkv1 = _rwkv_rows_call(cur, shift0, wkv0, w["rwkv"])

    y = _merge_call(h1, gates, y_s5, y_rw, mods2, (5, 6, 7, 8), w["w_out"],
                    w["g_ffn2"], *w["ffn2"], w["g_final"], tm=tm2)

    shift1 = cur.reshape(nb, length, n_shift)[:, -1]
    return (y.reshape(nb, length, d), hre.reshape(nb, g, n), him.reshape(nb, g, n), wkv1, shift1)


def kernel(x_prompt, x_sample, c_prompt, c_sample, state_s5_re, state_s5_im, state_wkv, state_shift, w_ada, b_ada, g_ffn1, g_mix, g_ffn2, g_final, ffn1_w1, ffn1_w3, ffn1_w2, ffn2_w1, ffn2_w3, ffn2_w2, w_in, mu_shift, s5_lam_re, s5_lam_im, s5_log_dt, s5_b_re, s5_b_im, s5_c_re, s5_c_im, s5_d, s5_glu_v, s5_glu_g, rwkv_w0, rwkv_w2, rwkv_a0, rwkv_a2, rwkv_g2, rwkv_k_k, rwkv_k_a, rwkv_r_k, rwkv_ln_w, rwkv_ln_b, rwkv_proj, w_out):
    bp, _, d = x_prompt.shape
    bs = x_sample.shape[0]
    g, n = s5_lam_re.shape
    nh, head = rwkv_r_k.shape
    width = nh * head
    n_shift = mu_shift.shape[0]
    assert bs % bp == 0, "sample rows come first in the modulation array; prompt rows must stay block-aligned"
    bf = lambda t: t.astype(BF16)

    a_re, a_im, wb, wc_re, wc_im = _s5_prep_call(
        s5_lam_re, s5_lam_im, s5_log_dt, jnp.swapaxes(s5_b_re, 1, 2), jnp.swapaxes(s5_b_im, 1, 2),
        s5_c_re, s5_c_im, bp)
    w_lora = rwkv_w2.shape[0]
    a_lora = rwkv_a2.shape[0]
    w2p = jnp.concatenate([rwkv_w2, jnp.zeros((a_lora, width), F32)], axis=0)
    a2p = jnp.concatenate([jnp.zeros((w_lora, width), F32), rwkv_a2], axis=0)
    vec_rows = [rwkv_w0, rwkv_a0, rwkv_k_k, rwkv_k_a, rwkv_r_k.reshape(width), rwkv_ln_w, rwkv_ln_b]
    vecs = jnp.stack(vec_rows + [jnp.zeros_like(rwkv_w0)] * (V7X_SUBLANES - len(vec_rows)))
    rw = dict(mu=mu_shift.reshape(1, n_shift), vecs=vecs, w2p=bf(w2p), a2p=bf(a2p), g2=bf(rwkv_g2),
              proj=bf(rwkv_proj),
              ones_bd=jnp.kron(jnp.eye(nh, dtype=F32), jnp.ones((head, head), F32)).astype(BF16))
    w = dict(
        g_ffn1=g_ffn1, g_mix=g_mix, g_ffn2=g_ffn2, g_final=g_final,
        ffn1=(bf(ffn1_w1), bf(ffn1_w3), bf(ffn1_w2)), ffn2=(bf(ffn2_w1), bf(ffn2_w3), bf(ffn2_w2)),
        w_in=bf(w_in), w_out=bf(w_out),
        a_re=a_re, a_im=a_im,
        s5=dict(wb=wb, wc_re=wc_re, wc_im=wc_im, d=s5_d.reshape(1, -1), glu_v=bf(s5_glu_v), glu_g=bf(s5_glu_g)),
        rwkv=rw,
    )

    mod = _mod_call(jnp.concatenate([c_sample, c_prompt], axis=0), w_ada, b_ada)

    z_s5 = jnp.zeros((bp, g, n), state_s5_re.dtype)
    z_wkv = jnp.zeros((bp, nh, head, head), state_wkv.dtype)
    z_shift = jnp.zeros((bp, n_shift), state_shift.dtype)
    y_p, s5re_p, s5im_p, wkv_p, shift_p = _layer(x_prompt, mod, bs, z_s5, z_s5, z_wkv, z_shift, w, sequential=True)
    y_s, s5re_s, s5im_s, wkv_s_t, shift_s = _layer(x_sample, mod, 0, state_s5_re, state_s5_im,
                                                   jnp.transpose(state_wkv, (1, 2, 3, 0)), state_shift, w,
                                                   sequential=False)
    wkv_s = jnp.transpose(wkv_s_t, (3, 0, 1, 2))
    return (y_p, y_s, s5re_p, s5im_p, wkv_p, shift_p, s5re_s, s5im_s, wkv_s, shift_s)
```

```python
import functools
import math

import jax
import jax.numpy as jnp
from jax import lax
from jax.experimental import pallas as pl
from jax.experimental.pallas import tpu as pltpu

F32 = jnp.float32
BF16 = jnp.bfloat16

NORM_EPS = 1e-6
GN_EPS = 64e-5
N_MOD = 9

V7X_LANES = 128
V7X_SUBLANES = 8
V7X_MXU_DIM = 256
V7X_VMEM_BYTES = 64 * 1024 * 1024
VMEM_CAP_BYTES = V7X_VMEM_BYTES - 8 * 1024 * 1024

S5_STEP_ROWS = 1024
S5_OUT_PARTS = 1
WKV_CHUNK = 64
WKV_BLOCK_CHUNKS = 8
WKV_SEQS = 8
WKV_GROUPS = 1


def _cparams(semantics, vmem_bytes):
    return pltpu.CompilerParams(
        dimension_semantics=semantics,
        vmem_limit_bytes=int(min(vmem_bytes, VMEM_CAP_BYTES)),
    )


def _const_spec(shape):
    nd = len(shape)
    return pl.BlockSpec(shape, lambda *_: (0,) * nd, pipeline_mode=pl.Buffered(1))


def _dot(a, b):
    return jnp.dot(a, b, preferred_element_type=F32)


def _dot_nt(a, b):
    return lax.dot_general(a, b, (((1,), (1,)), ((), ())), preferred_element_type=F32)


def _dot_tn(a, b):
    return lax.dot_general(a, b, (((0,), (0,)), ((), ())), preferred_element_type=F32)


def _split3(x):
    hi = x.astype(BF16)
    r1 = x - hi.astype(F32)
    mid = r1.astype(BF16)
    lo = (r1 - mid.astype(F32)).astype(BF16)
    return hi, mid, lo


def _rms(x, g):
    ms = jnp.mean(x * x, axis=-1, keepdims=True)
    return x * lax.rsqrt(ms + NORM_EPS) * g


def _softplus(x):
    return jnp.maximum(x, 0.0) + jnp.log(1.0 + jnp.exp(-jnp.abs(x)))


def _mod_kernel(c_ref, w_ref, b_ref, o_ref):
    c = c_ref[...]
    s = (c * jax.nn.sigmoid(c)).astype(BF16)
    o_ref[...] = _dot(s, w_ref[...].astype(BF16)) + b_ref[...]


def _mod_call(c, w_ada, b_ada):
    rows, d = c.shape
    n = w_ada.shape[1]
    return pl.pallas_call(
        _mod_kernel,
        out_shape=jax.ShapeDtypeStruct((n // d, rows, d), F32),
        grid=(n // d,),
        in_specs=[
            pl.BlockSpec((rows, d), lambda j: (0, 0)),
            pl.BlockSpec((d, d), lambda j: (0, j)),
            pl.BlockSpec((1, d), lambda j: (0, j)),
        ],
        out_specs=pl.BlockSpec((None, rows, d), lambda j: (j, 0, 0)),
        compiler_params=_cparams(("parallel",), 32 * 2**20),
        name="adaln_mod",
    )(c, w_ada, b_ada.reshape(1, n))


class _ModRows:
    def __init__(self, mod, row0, nrows, tm, tiles_per_seq):
        self.mod, self.row0, self.nrows, self.tm, self.tiles_per_seq = mod, row0, nrows, tm, tiles_per_seq

    def spec(self, k):
        d = self.mod.shape[2]
        if self.tiles_per_seq is None:
            blk0 = self.row0 // self.tm
            return pl.BlockSpec((None, self.tm, d), lambda i: (k, blk0 + i, 0))
        blk0 = self.row0 // self.nrows
        return pl.BlockSpec((None, self.nrows, d), lambda i: (k, blk0, 0))


def _mod_row(ref, tiles_per_seq):
    if tiles_per_seq is None:
        return ref[...]
    return ref[pl.ds(pl.program_id(0) // tiles_per_seq, 1), :]


def _ffn_core(x, sh, sc, gt, g, w1_ref, w3_ref, w2_ref, n_chunks):
    rows = x.shape[0]
    n_split = 2 if rows % (2 * V7X_SUBLANES) == 0 and sh.shape[0] == 1 else 1
    step = rows // n_split
    bounds = _ffn_bounds(w1_ref.shape[1], n_chunks)
    outs = []
    for r in range(n_split):
        xr = x[r * step:(r + 1) * step]
        u = (_rms(xr, g) * (1.0 + sc) + sh).astype(BF16)
        acc = None
        for lo, hi in zip(bounds[:-1], bounds[1:]):
            a = _dot(u, w1_ref[:, lo:hi])
            b = _dot(u, w3_ref[:, lo:hi])
            hm = (a * jax.nn.sigmoid(a) * b).astype(BF16)
            part = _dot(hm, w2_ref[lo:hi, :])
            acc = part if acc is None else acc + part
        outs.append(xr + (0.5 * gt) * acc)
    return outs[0] if n_split == 1 else jnp.concatenate(outs, axis=0)


def _ffn_bounds(d_ff, n_chunks):
    tiles = -(-d_ff // V7X_MXU_DIM)
    return [min(d_ff, ((tiles * c + n_chunks - 1) // n_chunks) * V7X_MXU_DIM) for c in range(n_chunks + 1)]


def _ffn_kernel(x_ref, sh_ref, sc_ref, gt_ref, g_ref, w1_ref, w3_ref, w2_ref, o_ref, *, n_chunks, tiles_per_seq):
    row = functools.partial(_mod_row, tiles_per_seq=tiles_per_seq)
    o_ref[...] = _ffn_core(x_ref[...], row(sh_ref), row(sc_ref), row(gt_ref), g_ref[...],
                           w1_ref, w3_ref, w2_ref, n_chunks)


def _ffn_chunks(d_ff):
    return 2 if d_ff % (2 * V7X_LANES) == 0 else 1


def _ffn_call(x2, mods, ks, g, w1, w3, w2, *, tm):
    rows, d = x2.shape
    d_ff = w1.shape[1]
    kern = functools.partial(_ffn_kernel, n_chunks=_ffn_chunks(d_ff), tiles_per_seq=mods.tiles_per_seq)
    weights = 3 * d * d_ff * 2
    tiles = 4 * tm * d * 4 + 4 * tm * d_ff * 4
    return pl.pallas_call(
        kern,
        out_shape=jax.ShapeDtypeStruct((rows, d), F32),
        grid=(rows // tm,),
        in_specs=[
            pl.BlockSpec((tm, d), lambda i: (i, 0)),
            mods.spec(ks[0]), mods.spec(ks[1]), mods.spec(ks[2]),
            _const_spec((1, d)),
            _const_spec(w1.shape), _const_spec(w3.shape), _const_spec(w2.shape),
        ],
        out_specs=pl.BlockSpec((tm, d), lambda i: (i, 0)),
        compiler_params=_cparams(("parallel",), weights + tiles + 8 * 2**20),
        name="ffn1",
    )(x2, mods.mod, mods.mod, mods.mod, g.reshape(1, d), w1, w3, w2)


def _proj_kernel(h_ref, sh_ref, sc_ref, g_ref, win_ref, s5_ref, cur_ref, gates_ref, *, tiles_per_seq):
    row = functools.partial(_mod_row, tiles_per_seq=tiles_per_seq)
    u = (_rms(h_ref[...], g_ref[...]) * (1.0 + row(sc_ref)) + row(sh_ref)).astype(BF16)
    n0 = s5_ref.shape[1]
    n1 = n0 + cur_ref.shape[1]
    s5_ref[...] = _dot(u, win_ref[:, :n0])
    cur_ref[...] = _dot(u, win_ref[:, n0:n1])
    gates_ref[...] = _dot(u, win_ref[:, n1:])


def _time_major(rows, width, tm, tiles_per_seq):
    length = tiles_per_seq * tm
    spec = pl.BlockSpec((tm, width), lambda i: (i % tiles_per_seq, i // tiles_per_seq))
    return (length, rows // length * width), spec


def _proj_call(h, mods, ks, g, w_in, widths, *, tm):
    rows, d = h.shape
    n_in = w_in.shape[1]
    shapes = [(rows, w) for w in widths]
    specs = [pl.BlockSpec((tm, w), lambda i: (i, 0)) for w in widths]
    if mods.tiles_per_seq is not None:
        shapes[0], specs[0] = _time_major(rows, widths[0], tm, mods.tiles_per_seq)
    return pl.pallas_call(
        functools.partial(_proj_kernel, tiles_per_seq=mods.tiles_per_seq),
        out_shape=tuple(jax.ShapeDtypeStruct(s, F32) for s in shapes),
        grid=(rows // tm,),
        in_specs=[
            pl.BlockSpec((tm, d), lambda i: (i, 0)),
            mods.spec(ks[0]), mods.spec(ks[1]),
            _const_spec((1, d)),
            _const_spec(w_in.shape),
        ],
        out_specs=tuple(specs),
        compiler_params=_cparams(("parallel",), d * n_in * 2 + 3 * tm * (d + n_in) * 4 + 8 * 2**20),
        name="mix_proj",
    )(h, mods.mod, mods.mod, g.reshape(1, d), w_in)


def _s5_prep_kernel(lre_ref, lim_ref, ldt_ref, bre_ref, bim_ref, cre_ref, cim_ref,
                    are_ref, aim_ref, wb_ref, wcre_ref, wcim_ref, *, gpb):
    lre = lre_ref[...]
    lim = lim_ref[...]
    dt = jnp.exp(ldt_ref[...])
    mag = jnp.exp(lre * dt)
    ang = lim * dt
    a_re = mag * jnp.cos(ang)
    a_im = mag * jnp.sin(ang)
    den = lre * lre + lim * lim
    num_re = a_re - 1.0
    k_re = (num_re * lre + a_im * lim) / den
    k_im = (a_im * lre - num_re * lim) / den
    b_re = bre_ref[...]
    b_im = bim_ref[...]
    bb_re = (k_re * b_re - k_im * b_im).astype(BF16)
    bb_im = (k_re * b_im + k_im * b_re).astype(BF16)
    c_re = cre_ref[...].astype(BF16)
    c_im = (-cim_ref[...]).astype(BF16)
    g, c, n = b_re.shape
    nb = are_ref.shape[0]
    s = gpb * n
    wb_ref[...] = jnp.zeros(wb_ref.shape, BF16)
    wcre_ref[...] = jnp.zeros(wcre_ref.shape, BF16)
    wcim_ref[...] = jnp.zeros(wcim_ref.shape, BF16)
    for gi in range(g):
        k, gl = divmod(gi, gpb)
        rows = slice(gl * c, (gl + 1) * c)
        cols = slice(gl * n, (gl + 1) * n)
        are_ref[:, gi * n:(gi + 1) * n] = jnp.broadcast_to(a_re[gi], (nb, n))
        aim_ref[:, gi * n:(gi + 1) * n] = jnp.broadcast_to(a_im[gi], (nb, n))
        wb_ref[k, rows, cols] = bb_re[gi]
        wb_ref[k, rows, s + gl * n:s + (gl + 1) * n] = bb_im[gi]
        wcre_ref[k, rows, cols] = c_re[gi]
        wcim_ref[k, rows, cols] = c_im[gi]


def _s5_prep_call(lam_re, lam_im, log_dt, b_re_t, b_im_t, c_re, c_im, nb):
    g, n = lam_re.shape
    c = b_re_t.shape[1]
    gpb = max(1, min(g, V7X_MXU_DIM // c))
    nk = g // gpb
    return pl.pallas_call(
        functools.partial(_s5_prep_kernel, gpb=gpb),
        out_shape=(jax.ShapeDtypeStruct((nb, g * n), F32), jax.ShapeDtypeStruct((nb, g * n), F32),
                   jax.ShapeDtypeStruct((nk, gpb * c, 2 * gpb * n), BF16),
                   jax.ShapeDtypeStruct((nk, gpb * c, gpb * n), BF16),
                   jax.ShapeDtypeStruct((nk, gpb * c, gpb * n), BF16)),
        name="s5_discretise",
    )(lam_re.reshape(g, 1, n), lam_im.reshape(g, 1, n), log_dt.reshape(g, 1, 1), b_re_t, b_im_t, c_re, c_im)


def _s5_input_matmul(ub, wb_ref, bure_ref, buim_ref):
    nk, kin, two_s = wb_ref.shape
    s = two_s // 2
    for k in range(nk):
        res = _dot(ub[:, k * kin:(k + 1) * kin], wb_ref[k])
        bure_ref[:, k * s:(k + 1) * s] = res[:, :s]
        buim_ref[:, k * s:(k + 1) * s] = res[:, s:]


def _s5_output(xre, xim, u, wcre_ref, wcim_ref, d_row):
    nk, kout, s = wcre_ref.shape
    xre_b = xre.astype(BF16)
    xim_b = xim.astype(BF16)
    ys = []
    for k in range(nk):
        ys.append(_dot_nt(xre_b[:, k * s:(k + 1) * s], wcre_ref[k])
                  + _dot_nt(xim_b[:, k * s:(k + 1) * s], wcim_ref[k]))
    y = jnp.concatenate(ys, axis=-1) + d_row * u
    return jax.nn.gelu(y)


def _s5_glu(zb, gv_ref, gg_ref):
    return _dot(zb, gv_ref[...]) * jax.nn.sigmoid(_dot(zb, gg_ref[...]))


def _s5_seq_kernel(u_ref, h0re_ref, h0im_ref, are_ref, aim_ref, wb_ref, wcre_ref, wcim_ref, d_ref,
                   gv_ref, gg_ref,
                   o_ref, hre_ref, him_ref,
                   bure_scr, buim_scr, xre_scr, xim_scr, *, lane_chunk, n_parts):
    nb = hre_ref.shape[0]
    ts = u_ref.shape[0] // nb

    @pl.when(pl.program_id(0) == 0)
    def _():
        hre_ref[...] = h0re_ref[...]
        him_ref[...] = h0im_ref[...]

    u_tm = u_ref[...]
    _s5_input_matmul(u_tm.astype(BF16), wb_ref, bure_scr, buim_scr)

    ns = are_ref.shape[1]
    lanes = [slice(lc * lane_chunk, (lc + 1) * lane_chunk) for lc in range(ns // lane_chunk)]
    state = [(hre_ref[:, sl], him_ref[:, sl]) for sl in lanes]
    tp = ts // n_parts
    for part in range(n_parts):
        for lc, sl in enumerate(lanes):
            ar = are_ref[:, sl]
            ai = aim_ref[:, sl]
            sr, si = state[lc]
            for t in range(part * tp, (part + 1) * tp):
                rs = slice(t * nb, (t + 1) * nb)
                nr = ar * sr - ai * si + bure_scr[rs, sl]
                ni = ar * si + ai * sr + buim_scr[rs, sl]
                xre_scr[rs, sl] = nr
                xim_scr[rs, sl] = ni
                sr, si = nr, ni
            state[lc] = (sr, si)
        pr = slice(part * tp * nb, (part + 1) * tp * nb)
        z_tm = _s5_output(xre_scr[pr, :], xim_scr[pr, :], u_tm[pr], wcre_ref, wcim_ref, d_ref[...])
        o_ref[pr, :] = _s5_glu(z_tm.astype(BF16), gv_ref, gg_ref)
    for lc, sl in enumerate(lanes):
        hre_ref[:, sl], him_ref[:, sl] = state[lc]


def _s5_rows_kernel(u_ref, h0re_ref, h0im_ref, are_ref, aim_ref, wb_ref, wcre_ref, wcim_ref, d_ref,
                    gv_ref, gg_ref, o_ref, hre_ref, him_ref, bure_scr, buim_scr):
    u = u_ref[...]
    _s5_input_matmul(u.astype(BF16), wb_ref, bure_scr, buim_scr)
    ar = are_ref[0:1, :]
    ai = aim_ref[0:1, :]
    sr = h0re_ref[...]
    si = h0im_ref[...]
    nr = ar * sr - ai * si + bure_scr[...]
    ni = ar * si + ai * sr + buim_scr[...]
    hre_ref[...] = nr
    him_ref[...] = ni
    z = _s5_output(nr, ni, u, wcre_ref, wcim_ref, d_ref[...])
    o_ref[...] = _s5_glu(z.astype(BF16), gv_ref, gg_ref)


def _s5_weight_specs(ws):
    return [_const_spec(w.shape) for w in ws]


def _s5_seq_call(s5_tm, h0re, h0im, a_re, a_im, sw, *, ts):
    nb, ns = a_re.shape
    din = s5_tm.shape[1]
    d = sw["glu_v"].shape[1]
    rows = nb * ts
    ws = [sw["wb"], sw["wc_re"], sw["wc_im"], sw["d"], sw["glu_v"], sw["glu_g"]]
    kern = functools.partial(_s5_seq_kernel, lane_chunk=4 * V7X_LANES, n_parts=S5_OUT_PARTS)
    return pl.pallas_call(
        kern,
        out_shape=(jax.ShapeDtypeStruct((s5_tm.shape[0], d), F32),
                   jax.ShapeDtypeStruct((nb, ns), F32), jax.ShapeDtypeStruct((nb, ns), F32)),
        grid=(s5_tm.shape[0] // rows,),
        in_specs=[pl.BlockSpec((rows, din), lambda c: (c, 0)),
                  _const_spec((nb, ns)), _const_spec((nb, ns)), _const_spec((nb, ns)), _const_spec((nb, ns))]
                 + _s5_weight_specs(ws),
        out_specs=(pl.BlockSpec((rows, d), lambda c: (c, 0)),
                   pl.BlockSpec((nb, ns), lambda c: (0, 0)), pl.BlockSpec((nb, ns), lambda c: (0, 0))),
        scratch_shapes=[pltpu.VMEM((rows, ns), F32) for _ in range(4)],
        compiler_params=_cparams(("arbitrary",), 40 * 2**20),
        name="s5_seq",
    )(s5_tm, h0re, h0im, a_re, a_im, *ws)


def _s5_rows_call(s5_in, h0re, h0im, a_re, a_im, sw):
    rows, din = s5_in.shape
    ns = a_re.shape[1]
    d = sw["glu_v"].shape[1]
    ws = [sw["wb"], sw["wc_re"], sw["wc_im"], sw["d"], sw["glu_v"], sw["glu_g"]]
    full = lambda shape: pl.BlockSpec(shape, lambda i: (0,) * len(shape))
    return pl.pallas_call(
        _s5_rows_kernel,
        out_shape=(jax.ShapeDtypeStruct((rows, d), F32),
                   jax.ShapeDtypeStruct((rows, ns), F32), jax.ShapeDtypeStruct((rows, ns), F32)),
        grid=(1,),
        in_specs=[full((rows, din)), full((rows, ns)), full((rows, ns)), full(a_re.shape), full(a_im.shape)]
                 + [full(w.shape) for w in ws],
        out_specs=(full((rows, d)), full((rows, ns)), full((rows, ns))),
        scratch_shapes=[pltpu.VMEM((rows, ns), F32) for _ in range(2)],
        compiler_params=_cparams(("arbitrary",), 40 * 2**20),
        name="s5_rows",
    )(s5_in, h0re, h0im, a_re, a_im, *ws)


def _headsum(x, ones_bd):
    hi = x.astype(BF16)
    lo = (x - hi.astype(F32)).astype(BF16)
    return _dot(hi, ones_bd) + _dot(lo, ones_bd)


def _pre_lora(cur, prev, p, width):
    mixed = cur + p["mu"] * (prev - cur)
    nl = p["w2p"].shape[0]
    wa = mixed[:, 3 * width:3 * width + nl]
    gd = mixed[:, 3 * width + nl:]
    return dict(
        r=mixed[:, 0:width], k=mixed[:, width:2 * width], v=mixed[:, 2 * width:3 * width],
        lw=_dot(jnp.tanh(wa).astype(BF16), p["w2p"]),
        la=_dot(wa.astype(BF16), p["a2p"]),
        g=_dot(jax.nn.sigmoid(gd).astype(BF16), p["g2"]))


def _pre_gates(t, p, ones_bd):
    w = -_softplus(-(p["w0"] + t["lw"])) - 0.5
    t["ld"] = -jnp.exp(w)
    t["a"] = jax.nn.sigmoid(p["a0"] + t["la"])
    t["kk"] = t["k"] * p["k_k"]
    t["ss"] = _headsum(t["kk"] * t["kk"], ones_bd)


def _pre_keys(t, p):
    kk = t["kk"] * jnp.minimum(lax.rsqrt(t["ss"]), 1e12)
    t["k2"] = t["k"] * (1.0 + (t["a"] - 1.0) * p["k_a"])
    t["av"] = -kk
    t["bv"] = kk * t["a"]


def _rwkv_pre(cur, prev, p, ones_bd, width):
    t = _pre_lora(cur, prev, p, width)
    _pre_gates(t, p, ones_bd)
    _pre_keys(t, p)
    return t["r"], t["ld"], t["k2"], t["v"], t["av"], t["bv"], t["g"]


def _post_mean(y, ones_bd, head):
    return y - _headsum(y, ones_bd) * (1.0 / head)


def _post_stats(dlt, t, p, ones_bd, head):
    return _headsum(dlt * dlt, ones_bd) * (1.0 / head), _headsum(t["r"] * t["k2"] * p["r_k"], ones_bd)


def _post_out(dlt, var, rk, t, p, proj):
    yn = dlt * lax.rsqrt(var + GN_EPS) * p["ln_w"] + p["ln_b"]
    o = ((yn + rk * t["v"]) * t["g"]).astype(BF16)
    return _dot(o, proj)


def _rwkv_post(y, r, k2, v, g, p, ones_bd, proj, head):
    t = dict(r=r, k2=k2, v=v, g=g)
    dlt = _post_mean(y, ones_bd, head)
    var, rk = _post_stats(dlt, t, p, ones_bd, head)
    return _post_out(dlt, var, rk, t, p, proj)


def _interleave(main, other):
    out = []
    done = 0
    for i, item in enumerate(main):
        out.append(item)
        upto = (len(other) * (i + 1)) // len(main)
        out.extend(other[done:upto])
        done = upto
    return out


def _rwkv_seq_block(cur_ref, carry_scr, p, ones_bd, tril, proj_ref, s_ref, y_ref, out_ref, head, chunk, n_groups):
    nseq, tr, _ = cur_ref.shape
    width = y_ref.shape[1]
    nh = width // head
    c = chunk
    per_seq = tr // c
    gsz = nseq // n_groups
    steps = int(math.log2(c))
    hs = lambda h: slice(h * head, (h + 1) * head)
    ri = lax.broadcasted_iota(jnp.int32, (c, c), 0)
    ci = lax.broadcasted_iota(jnp.int32, (c, c), 1)
    strict = ri > ci
    incl = ri >= ci
    row = lax.broadcasted_iota(jnp.int32, cur_ref.shape[1:], 0)
    tok = {}
    fac, ar, sc, pw, nm, m_rb, lv, mv, vk, s0, am, ub, dlt, stats = ({} for _ in range(14))

    def seqs(gi):
        return range(gi * gsz, (gi + 1) * gsz)

    def chunks(gi):
        return [(q * per_seq + t, q, slice(((q - gi * gsz) * per_seq + t) * c, ((q - gi * gsz) * per_seq + t + 1) * c))
                for q in seqs(gi) for t in range(per_seq)]

    def pairs(gi):
        return [(j, q, h) for j, q, _ in chunks(gi) for h in range(nh)]

    def grows(gi):
        return slice(gi * gsz * tr, (gi + 1) * gsz * tr)

    def pre(gi):
        curs, prevs = [], []
        for q in seqs(gi):
            cq = cur_ref[q]
            prevs.append(jnp.where(row == 0, carry_scr[q], pltpu.roll(cq, 1, axis=0)))
            carry_scr[q] = cq[tr - 1:tr, :]
            curs.append(cq)
        tok[gi] = _pre_lora(jnp.concatenate(curs, axis=0), jnp.concatenate(prevs, axis=0), p, width)

    def gates(gi):
        _pre_gates(tok[gi], p, ones_bd)

    def keys(gi):
        t = tok[gi]
        _pre_keys(t, p)
        for j, _, rs in chunks(gi):
            lds = t["ld"][rs]
            hi, mid, lo = _split3(lds)
            fac[j] = dict(lds=lds, cum=_dot(tril, hi) + _dot(tril, mid) + _dot(tril, lo))

    def decay(gi):
        t = tok[gi]
        for j, _, rs in chunks(gi):
            f = fac[j]
            cum = f["cum"]
            g_inv = jnp.exp(-cum)
            cum_end = cum[c - 1:c, :]
            g_rel = jnp.exp(cum_end - cum)
            f.update(a_t=t["av"][rs] * jnp.exp(cum - f["lds"]), r_t=t["r"][rs] * jnp.exp(cum),
                     b_t=t["bv"][rs] * g_inv, k_t=t["k2"][rs] * g_inv,
                     b_e=(t["bv"][rs] * g_rel).astype(BF16), k_e=(t["k2"][rs] * g_rel).astype(BF16),
                     g_end=jnp.exp(cum_end), v=t["v"][rs].astype(BF16))

    def scores(gi):
        for j, _, h in pairs(gi):
            f = fac[j]
            ar[j, h] = jnp.concatenate([f["a_t"][:, hs(h)], f["r_t"][:, hs(h)]], axis=0).astype(BF16)
            bk = jnp.concatenate([f["b_t"][:, hs(h)], f["k_t"][:, hs(h)]], axis=0).astype(BF16)
            sc[j, h] = _dot_nt(ar[j, h], bk)

    def lowers(gi):
        for j, _, h in pairs(gi):
            x = sc[j, h]
            vh = fac[j]["v"][:, hs(h)]
            pw[j, h] = jnp.where(strict, x[:c, :c], 0.0)
            nm[j, h] = pw[j, h]
            m_rb[j, h] = jnp.where(incl, x[c:, :c], 0.0).astype(BF16)
            lv[j, h] = _dot(jnp.where(strict, x[:c, c:], 0.0).astype(BF16), vh)
            mv[j, h] = _dot(jnp.where(incl, x[c:, c:], 0.0).astype(BF16), vh)
            vk[j, h] = _dot_tn(vh, fac[j]["k_e"][:, hs(h)])

    def square(gi):
        for j, _, h in pairs(gi):
            pb = pw[j, h].astype(BF16)
            pw[j, h] = _dot(pb, pb)

    def extend(gi):
        for j, _, h in pairs(gi):
            nm[j, h] = nm[j, h] + pw[j, h] + _dot(nm[j, h].astype(BF16), pw[j, h].astype(BF16))

    def chain(t, part):
        def project(gi):
            for j, q, h in sel(gi):
                s0[j, h] = s_ref[q, h]
                am[j, h] = _dot_nt(ar[j, h], s0[j, h].astype(BF16))

        def solve(gi):
            for j, q, h in sel(gi):
                w = am[j, h][:c] + lv[j, h]
                ub[j, h] = (w + _dot(nm[j, h].astype(BF16), w.astype(BF16))).astype(BF16)

        def update(gi):
            for j, q, h in sel(gi):
                f = fac[j]
                r0 = (q * per_seq + t) * c
                y_ref[r0:r0 + c, hs(h)] = am[j, h][c:] + _dot(m_rb[j, h], ub[j, h]) + mv[j, h]
                s_ref[q, h] = (s0[j, h] * f["g_end"][:, hs(h)] + _dot_tn(ub[j, h], f["b_e"][:, hs(h)])
                               + vk[j, h])

        def sel(gi):
            return [(j, q, h) for j, q, h in pairs(gi) if j == q * per_seq + t]

        return dict(project=project, solve=solve, update=update)[part]

    def post_mean(gi):
        dlt[gi] = _post_mean(y_ref[grows(gi), :], ones_bd, head)

    def post_stats(gi):
        stats[gi] = _post_stats(dlt[gi], tok[gi], p, ones_bd, head)

    def post_out(gi):
        out = _post_out(dlt[gi], *stats[gi], tok[gi], p, proj_ref[...])
        for i, q in enumerate(seqs(gi)):
            out_ref[q] = out[i * tr:(i + 1) * tr]

    front = [pre, gates, keys, decay, scores, lowers]
    middle = [st for _ in range(steps - 1) for st in (square, extend)]
    middle += [chain(t, part) for t in range(per_seq) for part in ("project", "solve", "update")]
    middle += [post_mean, post_stats, post_out]
    order = [(st, 0) for st in front]
    for gi in range(n_groups):
        nxt = [(st, gi + 1) for st in front] if gi + 1 < n_groups else []
        order += _interleave([(st, gi) for st in middle], nxt)
    for st, gi in order:
        st(gi)


_RWKV_VEC_NAMES = ("w0", "a0", "k_k", "k_a", "r_k", "ln_w", "ln_b")
_RWKV_CONSTS = ("mu", "vecs", "w2p", "a2p", "g2")


def _rwkv_params(mu_ref, vecs_ref, w2p_ref, a2p_ref, g2_ref):
    p = {n: vecs_ref[i:i + 1, :] for i, n in enumerate(_RWKV_VEC_NAMES)}
    p.update(mu=mu_ref[...], w2p=w2p_ref[...], a2p=a2p_ref[...], g2=g2_ref[...])
    return p


def _rwkv_seq_kernel(cur_ref, shift0_ref, s0_ref, *rest, chunk, head):
    np_ = len(_RWKV_CONSTS)
    p = _rwkv_params(*rest[:np_])
    proj_ref, ones_ref, tril_ref = rest[np_:np_ + 3]
    out_ref, st_ref = rest[np_ + 3:np_ + 5]
    carry_scr, y_scr = rest[np_ + 5:]

    @pl.when(pl.program_id(1) == 0)
    def _():
        carry_scr[...] = shift0_ref[...]
        st_ref[...] = s0_ref[...]

    _rwkv_seq_block(cur_ref, carry_scr, p, ones_ref[...], tril_ref[...], proj_ref, st_ref, y_scr, out_ref,
                    head, chunk, WKV_GROUPS)


def _rwkv_rows_pre_kernel(cur_ref, prev_ref, mu_ref, vecs_ref, w2p_ref, a2p_ref, g2_ref, ones_ref,
                          r_ref, k_ref, v_ref, g_ref, t_ref, *, width):
    p = _rwkv_params(mu_ref, vecs_ref, w2p_ref, a2p_ref, g2_ref)
    r, ld, k2, v, a, b, g = _rwkv_pre(cur_ref[...], prev_ref[...], p, ones_ref[...], width)
    r_ref[...] = r
    k_ref[...] = k2
    v_ref[...] = v
    g_ref[...] = g
    for i, val in enumerate((r, ld, k2, v, a, b)):
        t_ref[i] = val.T


def _wkv_step_kernel(t_ref, s_ref, y_ref, so_ref):
    r = t_ref[0]
    dec = jnp.exp(t_ref[1])
    k = t_ref[2]
    a = t_ref[4]
    b = t_ref[5]

    def body(i, carry):
        s = s_ref[i]
        sa = jnp.sum(s * a, axis=0, keepdims=True)
        s2 = s * dec + sa * b + t_ref[3, pl.ds(i, 1), :] * k
        so_ref[i] = s2
        y_ref[pl.ds(i, 1), :] = jnp.sum(s2 * r, axis=0, keepdims=True)
        return carry

    lax.fori_loop(0, s_ref.shape[0], body, 0, unroll=V7X_SUBLANES)


def _rwkv_rows_post_kernel(yt_ref, r_ref, k_ref, v_ref, g_ref, mu_ref, vecs_ref, w2p_ref, a2p_ref, g2_ref,
                           proj_ref, ones_ref, o_ref, *, head):
    p = _rwkv_params(mu_ref, vecs_ref, w2p_ref, a2p_ref, g2_ref)
    o_ref[...] = _rwkv_post(yt_ref[...].T, r_ref[...], k_ref[...], v_ref[...], g_ref[...], p,
                            ones_ref[...], proj_ref[...], head)


def _rwkv_param_list(rw):
    return [rw[n] for n in _RWKV_CONSTS]


def _rwkv_seq_call(cur3, shift0, s0, rw, *, tr, nseq):
    nb, length, n_shift = cur3.shape
    nh, head = s0.shape[1], s0.shape[2]
    width = nh * head
    d = rw["proj"].shape[1]
    params = _rwkv_param_list(rw)
    chunk = min(WKV_CHUNK, tr)
    tril = (jnp.arange(chunk)[:, None] >= jnp.arange(chunk)[None, :]).astype(BF16)
    consts = params + [rw["proj"], rw["ones_bd"], tril]
    kern = functools.partial(_rwkv_seq_kernel, chunk=chunk, head=head)
    return pl.pallas_call(
        kern,
        out_shape=(jax.ShapeDtypeStruct((nb, length, d), F32),
                   jax.ShapeDtypeStruct((nb, nh, head, head), F32)),
        grid=(nb // nseq, length // tr),
        in_specs=[pl.BlockSpec((nseq, tr, n_shift), lambda q, i: (q, i, 0)),
                  pl.BlockSpec((nseq, 1, n_shift), lambda q, i: (q, 0, 0)),
                  pl.BlockSpec((nseq, nh, head, head), lambda q, i: (q, 0, 0, 0))]
                 + [pl.BlockSpec(c.shape, lambda q, i, nd=c.ndim: (0,) * nd) for c in consts],
        out_specs=(pl.BlockSpec((nseq, tr, d), lambda q, i: (q, i, 0)),
                   pl.BlockSpec((nseq, nh, head, head), lambda q, i: (q, 0, 0, 0))),
        scratch_shapes=[pltpu.VMEM((nseq, 1, n_shift), F32), pltpu.VMEM((nseq * tr, width), F32)],
        compiler_params=_cparams(("arbitrary", "arbitrary"), 40 * 2**20),
        name="rwkv_seq",
    )(cur3, shift0.reshape(nb, 1, n_shift), s0, *consts)


def _rwkv_rows_call(cur, prev, s0_t, rw):
    rows, n_shift = cur.shape
    nh, head = s0_t.shape[0], s0_t.shape[1]
    width = nh * head
    d = rw["proj"].shape[1]
    params = _rwkv_param_list(rw)
    full = lambda shape: pl.BlockSpec(shape, lambda i: (0,) * len(shape))
    vec = jax.ShapeDtypeStruct((rows, width), F32)
    n_t = 6

    pre_consts = params + [rw["ones_bd"]]
    r, k2, v, g, t = pl.pallas_call(
        functools.partial(_rwkv_rows_pre_kernel, width=width),
        out_shape=(vec,) * 4 + (jax.ShapeDtypeStruct((n_t, width, rows), F32),),
        grid=(1,),
        in_specs=[full(cur.shape), full(prev.shape)] + [full(c.shape) for c in pre_consts],
        out_specs=(full((rows, width)),) * 4 + (full((n_t, width, rows)),),
        compiler_params=_cparams(("arbitrary",), 32 * 2**20),
        name="rwkv_rows_pre",
    )(cur, prev, *pre_consts)

    sblk = pl.BlockSpec((None, head, head, rows), lambda h: (h, 0, 0, 0))
    yt, s1_t = pl.pallas_call(
        _wkv_step_kernel,
        out_shape=(jax.ShapeDtypeStruct((width, rows), F32), jax.ShapeDtypeStruct(s0_t.shape, F32)),
        grid=(nh,),
        in_specs=[pl.BlockSpec((n_t, head, rows), lambda h: (0, h, 0)), sblk],
        out_specs=(pl.BlockSpec((head, rows), lambda h: (h, 0)), sblk),
        compiler_params=_cparams(("parallel",), 32 * 2**20),
        name="wkv_step",
    )(t, s0_t)

    post_consts = params + [rw["proj"], rw["ones_bd"]]
    out = pl.pallas_call(
        functools.partial(_rwkv_rows_post_kernel, head=head),
        out_shape=jax.ShapeDtypeStruct((rows, d), F32),
        grid=(1,),
        in_specs=[full((width, rows))] + [full((rows, width))] * 4 + [full(c.shape) for c in post_consts],
        out_specs=full((rows, d)),
        compiler_params=_cparams(("arbitrary",), 32 * 2**20),
        name="rwkv_rows_post",
    )(yt, r, k2, v, g, *post_consts)
    return out, s1_t


def _merge_kernel(h_ref, gates_ref, ys5_ref, yrw_ref, gt2_ref, wout_ref,
                  sh_ref, sc_ref, gt_ref, g_ref, w1_ref, w3_ref, w2_ref, gfin_ref, o_ref, *, n_chunks,
                  tiles_per_seq):
    row = functools.partial(_mod_row, tiles_per_seq=tiles_per_seq)
    d = h_ref.shape[1]
    gates = gates_ref[...]
    m = jax.nn.sigmoid(gates[:, :d]) * ys5_ref[...] + jax.nn.sigmoid(gates[:, d:]) * yrw_ref[...]
    h2 = h_ref[...] + row(gt2_ref) * _dot(m.astype(BF16), wout_ref[...])
    h3 = _ffn_core(h2, row(sh_ref), row(sc_ref), row(gt_ref), g_ref[...], w1_ref, w3_ref, w2_ref, n_chunks)
    o_ref[...] = _rms(h3, gfin_ref[...])


def _merge_call(h, gates, ys5, yrw, mods, ks, w_out, g, w1, w3, w2, g_fin, *, tm):
    rows, d = h.shape
    d_ff = w1.shape[1]
    kern = functools.partial(_merge_kernel, n_chunks=_ffn_chunks(d_ff), tiles_per_seq=mods.tiles_per_seq)
    row_spec = lambda w: pl.BlockSpec((tm, w), lambda i: (i, 0))
    weights = (3 * d * d_ff + d * d) * 2
    tiles = 2 * tm * 6 * d * 4 + 4 * tm * d_ff * 4 + 6 * tm * d * 4
    ys5_spec = row_spec(d) if mods.tiles_per_seq is None else _time_major(rows, d, tm, mods.tiles_per_seq)[1]
    return pl.pallas_call(
        kern,
        out_shape=jax.ShapeDtypeStruct((rows, d), F32),
        grid=(rows // tm,),
        in_specs=[row_spec(d), row_spec(2 * d), ys5_spec, row_spec(d),
                  mods.spec(ks[0]), _const_spec(w_out.shape),
                  mods.spec(ks[1]), mods.spec(ks[2]), mods.spec(ks[3]),
                  _const_spec((1, d)), _const_spec(w1.shape), _const_spec(w3.shape), _const_spec(w2.shape),
                  _const_spec((1, d))],
        out_specs=row_spec(d),
        compiler_params=_cparams(("parallel",), weights + tiles + 8 * 2**20),
        name="merge_ffn2",
    )(h, gates, ys5, yrw, mods.mod, w_out, mods.mod, mods.mod, mods.mod, g.reshape(1, d), w1, w3, w2,
      g_fin.reshape(1, d))


def _pick_tile(n, target):
    t = min(n, target)
    while n % t:
        t -= V7X_SUBLANES
    return t


def _layer(x, mod, mod_row0, s5_re0, s5_im0, wkv0, shift0, w, *, sequential):
    nb, length, d = x.shape
    rows = nb * length
    g, n = s5_re0.shape[1], s5_re0.shape[2]
    ns = g * n
    n_shift = shift0.shape[1]
    s5_width = w["s5"]["d"].shape[1]
    widths = (s5_width, n_shift, 2 * d)

    tm = _pick_tile(length if sequential else rows, 512)
    tm2 = _pick_tile(tm, 512)
    mods = _ModRows(mod, mod_row0, nb, tm, length // tm if sequential else None)
    mods2 = _ModRows(mod, mod_row0, nb, tm2, length // tm2 if sequential else None)

    x2 = x.reshape(rows, d)
    h1 = _ffn_call(x2, mods, (0, 1, 2), w["g_ffn1"], *w["ffn1"], tm=tm)
    s5_in, cur, gates = _proj_call(h1, mods, (3, 4), w["g_mix"], w["w_in"], widths, tm=tm)

    h0re = s5_re0.reshape(nb, ns)
    h0im = s5_im0.reshape(nb, ns)
    if sequential:
        ts = _pick_tile(length, S5_STEP_ROWS // nb)
        y_s5, hre, him = _s5_seq_call(s5_in.reshape(rows, s5_width), h0re, h0im,
                                      w["a_re"], w["a_im"], w["s5"], ts=ts)
        y_s5 = y_s5.reshape(length, nb * d)
        nseq = WKV_SEQS if nb % WKV_SEQS == 0 else 1
        tr = _pick_tile(length, WKV_BLOCK_CHUNKS * WKV_CHUNK // nseq)
        y_rw, wkv1 = _rwkv_seq_call(cur.reshape(nb, length, n_shift), shift0, wkv0, w["rwkv"], tr=tr, nseq=nseq)
        y_rw = y_rw.reshape(rows, d)
    else:
        y_s5, hre, him = _s5_rows_call(s5_in, h0re, h0im, w["a_re"], w["a_im"], w["s5"])
        y_rw, wkv1 = _rwkv_rows_call(cur, shift0, wkv0, w["rwkv"])

    y = _merge_call(h1, gates, y_s5, y_rw, mods2, (5, 6, 7, 8), w["w_out"],
                    w["g_ffn2"], *w["ffn2"], w["g_final"], tm=tm2)

    shift1 = cur.reshape(nb, length, n_shift)[:, -1]
    return (y.reshape(nb, length, d), hre.reshape(nb, g, n), him.reshape(nb, g, n), wkv1, shift1)


def kernel(x_prompt, x_sample, c_prompt, c_sample, state_s5_re, state_s5_im, state_wkv, state_shift, w_ada, b_ada, g_ffn1, g_mix, g_ffn2, g_final, ffn1_w1, ffn1_w3, ffn1_w2, ffn2_w1, ffn2_w3, ffn2_w2, w_in, mu_shift, s5_lam_re, s5_lam_im, s5_log_dt, s5_b_re, s5_b_im, s5_c_re, s5_c_im, s5_d, s5_glu_v, s5_glu_g, rwkv_w0, rwkv_w2, rwkv_a0, rwkv_a2, rwkv_g2, rwkv_k_k, rwkv_k_a, rwkv_r_k, rwkv_ln_w, rwkv_ln_b, rwkv_proj, w_out):
    bp, _, d = x_prompt.shape
    bs = x_sample.shape[0]
    g, n = s5_lam_re.shape
    nh, head = rwkv_r_k.shape
    width = nh * head
    n_shift = mu_shift.shape[0]
    assert bs % bp == 0, "sample rows come first in the modulation array; prompt rows must stay block-aligned"
    bf = lambda t: t.astype(BF16)

    a_re, a_im, wb, wc_re, wc_im = _s5_prep_call(
        s5_lam_re, s5_lam_im, s5_log_dt, jnp.swapaxes(s5_b_re, 1, 2), jnp.swapaxes(s5_b_im, 1, 2),
        s5_c_re, s5_c_im, bp)
    w_lora = rwkv_w2.shape[0]
    a_lora = rwkv_a2.shape[0]
    w2p = jnp.concatenate([rwkv_w2, jnp.zeros((a_lora, width), F32)], axis=0)
    a2p = jnp.concatenate([jnp.zeros((w_lora, width), F32), rwkv_a2], axis=0)
    vec_rows = [rwkv_w0, rwkv_a0, rwkv_k_k, rwkv_k_a, rwkv_r_k.reshape(width), rwkv_ln_w, rwkv_ln_b]
    vecs = jnp.stack(vec_rows + [jnp.zeros_like(rwkv_w0)] * (V7X_SUBLANES - len(vec_rows)))
    rw = dict(mu=mu_shift.reshape(1, n_shift), vecs=vecs, w2p=bf(w2p), a2p=bf(a2p), g2=bf(rwkv_g2),
              proj=bf(rwkv_proj),
              ones_bd=jnp.kron(jnp.eye(nh, dtype=F32), jnp.ones((head, head), F32)).astype(BF16))
    w = dict(
        g_ffn1=g_ffn1, g_mix=g_mix, g_ffn2=g_ffn2, g_final=g_final,
        ffn1=(bf(ffn1_w1), bf(ffn1_w3), bf(ffn1_w2)), ffn2=(bf(ffn2_w1), bf(ffn2_w3), bf(ffn2_w2)),
        w_in=bf(w_in), w_out=bf(w_out),
        a_re=a_re, a_im=a_im,
        s5=dict(wb=wb, wc_re=wc_re, wc_im=wc_im, d=s5_d.reshape(1, -1), glu_v=bf(s5_glu_v), glu_g=bf(s5_glu_g)),
        rwkv=rw,
    )

    mod = _mod_call(jnp.concatenate([c_sample, c_prompt], axis=0), w_ada, b_ada)

    z_s5 = jnp.zeros((bp, g, n), state_s5_re.dtype)
    z_wkv = jnp.zeros((bp, nh, head, head), state_wkv.dtype)
    z_shift = jnp.zeros((bp, n_shift), state_shift.dtype)
    y_p, s5re_p, s5im_p, wkv_p, shift_p = _layer(x_prompt, mod, bs, z_s5, z_s5, z_wkv, z_shift, w, sequential=True)
    y_s, s5re_s, s5im_s, wkv_s_t, shift_s = _layer(x_sample, mod, 0, state_s5_re, state_s5_im,
                                                   jnp.transpose(state_wkv, (1, 2, 3, 0)), state_shift, w,
                                                   sequential=False)
    wkv_s = jnp.transpose(wkv_s_t, (3, 0, 1, 2))
    return (y_p, y_s, s5re_p, s5im_p, wkv_p, shift_p, s5re_s, s5im_s, wkv_s, shift_s)
```

```python
import functools
import math

import jax
import jax.numpy as jnp
from jax import lax
from jax.experimental import pallas as pl
from jax.experimental.pallas import tpu as pltpu

F32 = jnp.float32
BF16 = jnp.bfloat16

NORM_EPS = 1e-6
GN_EPS = 64e-5
N_MOD = 9

V7X_LANES = 128
V7X_SUBLANES = 8
V7X_MXU_DIM = 256
V7X_VMEM_BYTES = 64 * 1024 * 1024
VMEM_CAP_BYTES = V7X_VMEM_BYTES - 8 * 1024 * 1024

S5_STEP_ROWS = 1024
S5_OUT_PARTS = 1
WKV_CHUNK = 64
WKV_BLOCK_CHUNKS = 8
WKV_SEQS = 8
WKV_GROUPS = 1


def _cparams(semantics, vmem_bytes):
    return pltpu.CompilerParams(
        dimension_semantics=semantics,
        vmem_limit_bytes=int(min(vmem_bytes, VMEM_CAP_BYTES)),
    )


def _const_spec(shape):
    nd = len(shape)
    return pl.BlockSpec(shape, lambda *_: (0,) * nd, pipeline_mode=pl.Buffered(1))


def _dot(a, b):
    return jnp.dot(a, b, preferred_element_type=F32)


def _dot_nt(a, b):
    return lax.dot_general(a, b, (((1,), (1,)), ((), ())), preferred_element_type=F32)


def _dot_tn(a, b):
    return lax.dot_general(a, b, (((0,), (0,)), ((), ())), preferred_element_type=F32)


def _split3(x):
    hi = x.astype(BF16)
    r1 = x - hi.astype(F32)
    mid = r1.astype(BF16)
    lo = (r1 - mid.astype(F32)).astype(BF16)
    return hi, mid, lo


def _rms(x, g):
    ms = jnp.mean(x * x, axis=-1, keepdims=True)
    return x * lax.rsqrt(ms + NORM_EPS) * g


def _softplus(x):
    return jnp.maximum(x, 0.0) + jnp.log(1.0 + jnp.exp(-jnp.abs(x)))


def _mod_kernel(c_ref, w_ref, b_ref, o_ref):
    c = c_ref[...]
    s = (c * jax.nn.sigmoid(c)).astype(BF16)
    o_ref[...] = _dot(s, w_ref[...].astype(BF16)) + b_ref[...]


def _mod_call(c, w_ada, b_ada):
    rows, d = c.shape
    n = w_ada.shape[1]
    return pl.pallas_call(
        _mod_kernel,
        out_shape=jax.ShapeDtypeStruct((n // d, rows, d), F32),
        grid=(n // d,),
        in_specs=[
            pl.BlockSpec((rows, d), lambda j: (0, 0)),
            pl.BlockSpec((d, d), lambda j: (0, j)),
            pl.BlockSpec((1, d), lambda j: (0, j)),
        ],
        out_specs=pl.BlockSpec((None, rows, d), lambda j: (j, 0, 0)),
        compiler_params=_cparams(("parallel",), 32 * 2**20),
        name="adaln_mod",
    )(c, w_ada, b_ada.reshape(1, n))


class _ModRows:
    def __init__(self, mod, row0, nrows, tm, tiles_per_seq):
        self.mod, self.row0, self.nrows, self.tm, self.tiles_per_seq = mod, row0, nrows, tm, tiles_per_seq

    def spec(self, k):
        d = self.mod.shape[2]
        if self.tiles_per_seq is None:
            blk0 = self.row0 // self.tm
            return pl.BlockSpec((None, self.tm, d), lambda i: (k, blk0 + i, 0))
        blk0 = self.row0 // self.nrows
        return pl.BlockSpec((None, self.nrows, d), lambda i: (k, blk0, 0))


def _mod_row(ref, tiles_per_seq):
    if tiles_per_seq is None:
        return ref[...]
    return ref[pl.ds(pl.program_id(0) // tiles_per_seq, 1), :]


def _ffn_core(x, sh, sc, gt, g, w1_ref, w3_ref, w2_ref, n_chunks):
    rows = x.shape[0]
    n_split = 2 if rows % (2 * V7X_SUBLANES) == 0 and sh.shape[0] == 1 else 1
    step = rows // n_split
    bounds = _ffn_bounds(w1_ref.shape[1], n_chunks)
    outs = []
    for r in range(n_split):
        xr = x[r * step:(r + 1) * step]
        u = (_rms(xr, g) * (1.0 + sc) + sh).astype(BF16)
        acc = None
        for lo, hi in zip(bounds[:-1], bounds[1:]):
            a = _dot(u, w1_ref[:, lo:hi])
            b = _dot(u, w3_ref[:, lo:hi])
            hm = (a * jax.nn.sigmoid(a) * b).astype(BF16)
            part = _dot(hm, w2_ref[lo:hi, :])
            acc = part if acc is None else acc + part
        outs.append(xr + (0.5 * gt) * acc)
    return outs[0] if n_split == 1 else jnp.concatenate(outs, axis=0)


def _ffn_bounds(d_ff, n_chunks):
    tiles = -(-d_ff // V7X_MXU_DIM)
    return [min(d_ff, ((tiles * c + n_chunks - 1) // n_chunks) * V7X_MXU_DIM) for c in range(n_chunks + 1)]


def _ffn_kernel(x_ref, sh_ref, sc_ref, gt_ref, g_ref, w1_ref, w3_ref, w2_ref, o_ref, *, n_chunks, tiles_per_seq):
    row = functools.partial(_mod_row, tiles_per_seq=tiles_per_seq)
    o_ref[...] = _ffn_core(x_ref[...], row(sh_ref), row(sc_ref), row(gt_ref), g_ref[...],
                           w1_ref, w3_ref, w2_ref, n_chunks)


def _ffn_chunks(d_ff):
    return 2 if d_ff % (2 * V7X_LANES) == 0 else 1


def _ffn_call(x2, mods, ks, g, w1, w3, w2, *, tm):
    rows, d = x2.shape
    d_ff = w1.shape[1]
    kern = functools.partial(_ffn_kernel, n_chunks=_ffn_chunks(d_ff), tiles_per_seq=mods.tiles_per_seq)
    weights = 3 * d * d_ff * 2
    tiles = 4 * tm * d * 4 + 4 * tm * d_ff * 4
    return pl.pallas_call(
        kern,
        out_shape=jax.ShapeDtypeStruct((rows, d), F32),
        grid=(rows // tm,),
        in_specs=[
            pl.BlockSpec((tm, d), lambda i: (i, 0)),
            mods.spec(ks[0]), mods.spec(ks[1]), mods.spec(ks[2]),
            _const_spec((1, d)),
            _const_spec(w1.shape), _const_spec(w3.shape), _const_spec(w2.shape),
        ],
        out_specs=pl.BlockSpec((tm, d), lambda i: (i, 0)),
        compiler_params=_cparams(("parallel",), weights + tiles + 8 * 2**20),
        name="ffn1",
    )(x2, mods.mod, mods.mod, mods.mod, g.reshape(1, d), w1, w3, w2)


def _proj_kernel(h_ref, sh_ref, sc_ref, g_ref, win_ref, s5_ref, cur_ref, gates_ref, *, tiles_per_seq):
    row = functools.partial(_mod_row, tiles_per_seq=tiles_per_seq)
    u = (_rms(h_ref[...], g_ref[...]) * (1.0 + row(sc_ref)) + row(sh_ref)).astype(BF16)
    n0 = s5_ref.shape[1]
    n1 = n0 + cur_ref.shape[1]
    s5_ref[...] = _dot(u, win_ref[:, :n0])
    cur_ref[...] = _dot(u, win_ref[:, n0:n1])
    gates_ref[...] = _dot(u, win_ref[:, n1:])


def _time_major(rows, width, tm, tiles_per_seq):
    length = tiles_per_seq * tm
    spec = pl.BlockSpec((tm, width), lambda i: (i % tiles_per_seq, i // tiles_per_seq))
    return (length, rows // length * width), spec


def _proj_call(h, mods, ks, g, w_in, widths, *, tm):
    rows, d = h.shape
    n_in = w_in.shape[1]
    shapes = [(rows, w) for w in widths]
    specs = [pl.BlockSpec((tm, w), lambda i: (i, 0)) for w in widths]
    if mods.tiles_per_seq is not None:
        shapes[0], specs[0] = _time_major(rows, widths[0], tm, mods.tiles_per_seq)
    return pl.pallas_call(
        functools.partial(_proj_kernel, tiles_per_seq=mods.tiles_per_seq),
        out_shape=tuple(jax.ShapeDtypeStruct(s, F32) for s in shapes),
        grid=(rows // tm,),
        in_specs=[
            pl.BlockSpec((tm, d), lambda i: (i, 0)),
            mods.spec(ks[0]), mods.spec(ks[1]),
            _const_spec((1, d)),
            _const_spec(w_in.shape),
        ],
        out_specs=tuple(specs),
        compiler_params=_cparams(("parallel",), d * n_in * 2 + 3 * tm * (d + n_in) * 4 + 8 * 2**20),
        name="mix_proj",
    )(h, mods.mod, mods.mod, g.reshape(1, d), w_in)


def _s5_prep_kernel(lre_ref, lim_ref, ldt_ref, bre_ref, bim_ref, cre_ref, cim_ref,
                    are_ref, aim_ref, wb_ref, wcre_ref, wcim_ref, *, gpb):
    lre = lre_ref[...]
    lim = lim_ref[...]
    dt = jnp.exp(ldt_ref[...])
    mag = jnp.exp(lre * dt)
    ang = lim * dt
    a_re = mag * jnp.cos(ang)
    a_im = mag * jnp.sin(ang)
    den = lre * lre + lim * lim
    num_re = a_re - 1.0
    k_re = (num_re * lre + a_im * lim) / den
    k_im = (a_im * lre - num_re * lim) / den
    b_re = bre_ref[...]
    b_im = bim_ref[...]
    bb_re = (k_re * b_re - k_im * b_im).astype(BF16)
    bb_im = (k_re * b_im + k_im * b_re).astype(BF16)
    c_re = cre_ref[...].astype(BF16)
    c_im = (-cim_ref[...]).astype(BF16)
    g, c, n = b_re.shape
    nb = are_ref.shape[0]
    s = gpb * n
    wb_ref[...] = jnp.zeros(wb_ref.shape, BF16)
    wcre_ref[...] = jnp.zeros(wcre_ref.shape, BF16)
    wcim_ref[...] = jnp.zeros(wcim_ref.shape, BF16)
    for gi in range(g):
        k, gl = divmod(gi, gpb)
        rows = slice(gl * c, (gl + 1) * c)
        cols = slice(gl * n, (gl + 1) * n)
        are_ref[:, gi * n:(gi + 1) * n] = jnp.broadcast_to(a_re[gi], (nb, n))
        aim_ref[:, gi * n:(gi + 1) * n] = jnp.broadcast_to(a_im[gi], (nb, n))
        wb_ref[k, rows, cols] = bb_re[gi]
        wb_ref[k, rows, s + gl * n:s + (gl + 1) * n] = bb_im[gi]
        wcre_ref[k, rows, cols] = c_re[gi]
        wcim_ref[k, rows, cols] = c_im[gi]


def _s5_prep_call(lam_re, lam_im, log_dt, b_re_t, b_im_t, c_re, c_im, nb):
    g, n = lam_re.shape
    c = b_re_t.shape[1]
    gpb = max(1, min(g, V7X_MXU_DIM // c))
    nk = g // gpb
    return pl.pallas_call(
        functools.partial(_s5_prep_kernel, gpb=gpb),
        out_shape=(jax.ShapeDtypeStruct((nb, g * n), F32), jax.ShapeDtypeStruct((nb, g * n), F32),
                   jax.ShapeDtypeStruct((nk, gpb * c, 2 * gpb * n), BF16),
                   jax.ShapeDtypeStruct((nk, gpb * c, gpb * n), BF16),
                   jax.ShapeDtypeStruct((nk, gpb * c, gpb * n), BF16)),
        name="s5_discretise",
    )(lam_re.reshape(g, 1, n), lam_im.reshape(g, 1, n), log_dt.reshape(g, 1, 1), b_re_t, b_im_t, c_re, c_im)


def _s5_input_matmul(ub, wb_ref, bure_ref, buim_ref):
    nk, kin, two_s = wb_ref.shape
    s = two_s // 2
    for k in range(nk):
        res = _dot(ub[:, k * kin:(k + 1) * kin], wb_ref[k])
        bure_ref[:, k * s:(k + 1) * s] = res[:, :s]
        buim_ref[:, k * s:(k + 1) * s] = res[:, s:]


def _s5_output(xre, xim, u, wcre_ref, wcim_ref, d_row):
    nk, kout, s = wcre_ref.shape
    xre_b = xre.astype(BF16)
    xim_b = xim.astype(BF16)
    ys = []
    for k in range(nk):
        ys.append(_dot_nt(xre_b[:, k * s:(k + 1) * s], wcre_ref[k])
                  + _dot_nt(xim_b[:, k * s:(k + 1) * s], wcim_ref[k]))
    y = jnp.concatenate(ys, axis=-1) + d_row * u
    return jax.nn.gelu(y)


def _s5_glu(zb, gv_ref, gg_ref):
    return _dot(zb, gv_ref[...]) * jax.nn.sigmoid(_dot(zb, gg_ref[...]))


def _s5_seq_kernel(u_ref, h0re_ref, h0im_ref, are_ref, aim_ref, wb_ref, wcre_ref, wcim_ref, d_ref,
                   gv_ref, gg_ref,
                   o_ref, hre_ref, him_ref,
                   bure_scr, buim_scr, xre_scr, xim_scr, uslab_scr, oslab_scr, *, lane_chunk, n_parts):
    nb = hre_ref.shape[0]
    ts = u_ref.shape[0]
    din = u_ref.shape[1] // nb
    dout = o_ref.shape[1] // nb

    @pl.when(pl.program_id(0) == 0)
    def _():
        hre_ref[...] = h0re_ref[...]
        him_ref[...] = h0im_ref[...]

    for b in range(nb):
        for j in range(din // V7X_LANES):
            col = b * din + j * V7X_LANES
            uslab_scr[j, pl.ds(b, ts, stride=nb), :] = u_ref[:, col:col + V7X_LANES]
    u_tm = jnp.concatenate([uslab_scr[j] for j in range(din // V7X_LANES)], axis=1)
    _s5_input_matmul(u_tm.astype(BF16), wb_ref, bure_scr, buim_scr)

    ns = are_ref.shape[1]
    lanes = [slice(lc * lane_chunk, (lc + 1) * lane_chunk) for lc in range(ns // lane_chunk)]
    state = [(hre_ref[:, sl], him_ref[:, sl]) for sl in lanes]
    tp = ts // n_parts
    for part in range(n_parts):
        for lc, sl in enumerate(lanes):
            ar = are_ref[:, sl]
            ai = aim_ref[:, sl]
            sr, si = state[lc]
            for t in range(part * tp, (part + 1) * tp):
                rs = slice(t * nb, (t + 1) * nb)
                nr = ar * sr - ai * si + bure_scr[rs, sl]
                ni = ar * si + ai * sr + buim_scr[rs, sl]
                xre_scr[rs, sl] = nr
                xim_scr[rs, sl] = ni
                sr, si = nr, ni
            state[lc] = (sr, si)
        pr = slice(part * tp * nb, (part + 1) * tp * nb)
        z_tm = _s5_output(xre_scr[pr, :], xim_scr[pr, :], u_tm[pr], wcre_ref, wcim_ref, d_ref[...])
        out = _s5_glu(z_tm.astype(BF16), gv_ref, gg_ref)
        for j in range(dout // V7X_LANES):
            oslab_scr[j, pr, :] = out[:, j * V7X_LANES:(j + 1) * V7X_LANES]
    for lc, sl in enumerate(lanes):
        hre_ref[:, sl], him_ref[:, sl] = state[lc]
    for b in range(nb):
        for j in range(dout // V7X_LANES):
            col = b * dout + j * V7X_LANES
            o_ref[:, col:col + V7X_LANES] = oslab_scr[j, pl.ds(b, ts, stride=nb), :]


def _s5_rows_kernel(u_ref, h0re_ref, h0im_ref, are_ref, aim_ref, wb_ref, wcre_ref, wcim_ref, d_ref,
                    gv_ref, gg_ref, o_ref, hre_ref, him_ref, bure_scr, buim_scr):
    u = u_ref[...]
    _s5_input_matmul(u.astype(BF16), wb_ref, bure_scr, buim_scr)
    ar = are_ref[0:1, :]
    ai = aim_ref[0:1, :]
    sr = h0re_ref[...]
    si = h0im_ref[...]
    nr = ar * sr - ai * si + bure_scr[...]
    ni = ar * si + ai * sr + buim_scr[...]
    hre_ref[...] = nr
    him_ref[...] = ni
    z = _s5_output(nr, ni, u, wcre_ref, wcim_ref, d_ref[...])
    o_ref[...] = _s5_glu(z.astype(BF16), gv_ref, gg_ref)


def _s5_weight_specs(ws):
    return [_const_spec(w.shape) for w in ws]


def _s5_seq_call(s5_in, h0re, h0im, a_re, a_im, sw, *, ts):
    nb, ns = a_re.shape
    length = s5_in.shape[0]
    din = s5_in.shape[1] // nb
    d = sw["glu_v"].shape[1]
    rows = nb * ts
    ws = [sw["wb"], sw["wc_re"], sw["wc_im"], sw["d"], sw["glu_v"], sw["glu_g"]]
    kern = functools.partial(_s5_seq_kernel, lane_chunk=4 * V7X_LANES, n_parts=S5_OUT_PARTS)
    return pl.pallas_call(
        kern,
        out_shape=(jax.ShapeDtypeStruct((length, nb * d), F32),
                   jax.ShapeDtypeStruct((nb, ns), F32), jax.ShapeDtypeStruct((nb, ns), F32)),
        grid=(length // ts,),
        in_specs=[pl.BlockSpec((ts, nb * din), lambda c: (c, 0)),
                  _const_spec((nb, ns)), _const_spec((nb, ns)), _const_spec((nb, ns)), _const_spec((nb, ns))]
                 + _s5_weight_specs(ws),
        out_specs=(pl.BlockSpec((ts, nb * d), lambda c: (c, 0)),
                   pl.BlockSpec((nb, ns), lambda c: (0, 0)), pl.BlockSpec((nb, ns), lambda c: (0, 0))),
        scratch_shapes=[pltpu.VMEM((rows, ns), F32) for _ in range(4)]
                       + [pltpu.VMEM((din // V7X_LANES, rows, V7X_LANES), F32),
                          pltpu.VMEM((d // V7X_LANES, rows, V7X_LANES), F32)],
        compiler_params=_cparams(("arbitrary",), 48 * 2**20),
        name="s5_seq",
    )(s5_in, h0re, h0im, a_re, a_im, *ws)


def _s5_rows_call(s5_in, h0re, h0im, a_re, a_im, sw):
    rows, din = s5_in.shape
    ns = a_re.shape[1]
    d = sw["glu_v"].shape[1]
    ws = [sw["wb"], sw["wc_re"], sw["wc_im"], sw["d"], sw["glu_v"], sw["glu_g"]]
    full = lambda shape: pl.BlockSpec(shape, lambda i: (0,) * len(shape))
    return pl.pallas_call(
        _s5_rows_kernel,
        out_shape=(jax.ShapeDtypeStruct((rows, d), F32),
                   jax.ShapeDtypeStruct((rows, ns), F32), jax.ShapeDtypeStruct((rows, ns), F32)),
        grid=(1,),
        in_specs=[full((rows, din)), full((rows, ns)), full((rows, ns)), full(a_re.shape), full(a_im.shape)]
                 + [full(w.shape) for w in ws],
        out_specs=(full((rows, d)), full((rows, ns)), full((rows, ns))),
        scratch_shapes=[pltpu.VMEM((rows, ns), F32) for _ in range(2)],
        compiler_params=_cparams(("arbitrary",), 40 * 2**20),
        name="s5_rows",
    )(s5_in, h0re, h0im, a_re, a_im, *ws)


def _headsum(x, ones_bd):
    hi = x.astype(BF16)
    lo = (x - hi.astype(F32)).astype(BF16)
    return _dot(hi, ones_bd) + _dot(lo, ones_bd)


def _pre_lora(cur, prev, p, width):
    mixed = cur + p["mu"] * (prev - cur)
    nl = p["w2p"].shape[0]
    wa = mixed[:, 3 * width:3 * width + nl]
    gd = mixed[:, 3 * width + nl:]
    return dict(
        r=mixed[:, 0:width], k=mixed[:, width:2 * width], v=mixed[:, 2 * width:3 * width],
        lw=_dot(jnp.tanh(wa).astype(BF16), p["w2p"]),
        la=_dot(wa.astype(BF16), p["a2p"]),
        g=_dot(jax.nn.sigmoid(gd).astype(BF16), p["g2"]))


def _pre_gates(t, p, ones_bd):
    w = -_softplus(-(p["w0"] + t["lw"])) - 0.5
    t["ld"] = -jnp.exp(w)
    t["a"] = jax.nn.sigmoid(p["a0"] + t["la"])
    t["kk"] = t["k"] * p["k_k"]
    t["ss"] = _headsum(t["kk"] * t["kk"], ones_bd)


def _pre_keys(t, p):
    kk = t["kk"] * jnp.minimum(lax.rsqrt(t["ss"]), 1e12)
    t["k2"] = t["k"] * (1.0 + (t["a"] - 1.0) * p["k_a"])
    t["av"] = -kk
    t["bv"] = kk * t["a"]


def _rwkv_pre(cur, prev, p, ones_bd, width):
    t = _pre_lora(cur, prev, p, width)
    _pre_gates(t, p, ones_bd)
    _pre_keys(t, p)
    return t["r"], t["ld"], t["k2"], t["v"], t["av"], t["bv"], t["g"]


def _post_mean(y, ones_bd, head):
    return y - _headsum(y, ones_bd) * (1.0 / head)


def _post_stats(dlt, t, p, ones_bd, head):
    return _headsum(dlt * dlt, ones_bd) * (1.0 / head), _headsum(t["r"] * t["k2"] * p["r_k"], ones_bd)


def _post_out(dlt, var, rk, t, p, proj):
    yn = dlt * lax.rsqrt(var + GN_EPS) * p["ln_w"] + p["ln_b"]
    o = ((yn + rk * t["v"]) * t["g"]).astype(BF16)
    return _dot(o, proj)


def _rwkv_post(y, r, k2, v, g, p, ones_bd, proj, head):
    t = dict(r=r, k2=k2, v=v, g=g)
    dlt = _post_mean(y, ones_bd, head)
    var, rk = _post_stats(dlt, t, p, ones_bd, head)
    return _post_out(dlt, var, rk, t, p, proj)


def _interleave(main, other):
    out = []
    done = 0
    for i, item in enumerate(main):
        out.append(item)
        upto = (len(other) * (i + 1)) // len(main)
        out.extend(other[done:upto])
        done = upto
    return out


def _rwkv_seq_block(cur_ref, carry_scr, p, ones_bd, tril, proj_ref, s_ref, y_ref, out_ref, head, chunk, n_groups):
    nseq, tr, _ = cur_ref.shape
    width = y_ref.shape[1]
    nh = width // head
    c = chunk
    per_seq = tr // c
    gsz = nseq // n_groups
    steps = int(math.log2(c))
    hs = lambda h: slice(h * head, (h + 1) * head)
    ri = lax.broadcasted_iota(jnp.int32, (c, c), 0)
    ci = lax.broadcasted_iota(jnp.int32, (c, c), 1)
    strict = ri > ci
    incl = ri >= ci
    row = lax.broadcasted_iota(jnp.int32, cur_ref.shape[1:], 0)
    tok = {}
    fac, ar, sc, pw, nm, m_rb, lv, mv, vk, s0, am, ub, dlt, stats = ({} for _ in range(14))

    def seqs(gi):
        return range(gi * gsz, (gi + 1) * gsz)

    def chunks(gi):
        return [(q * per_seq + t, q, slice(((q - gi * gsz) * per_seq + t) * c, ((q - gi * gsz) * per_seq + t + 1) * c))
                for q in seqs(gi) for t in range(per_seq)]

    def pairs(gi):
        return [(j, q, h) for j, q, _ in chunks(gi) for h in range(nh)]

    def grows(gi):
        return slice(gi * gsz * tr, (gi + 1) * gsz * tr)

    def pre(gi):
        curs, prevs = [], []
        for q in seqs(gi):
            cq = cur_ref[q]
            prevs.append(jnp.where(row == 0, carry_scr[q], pltpu.roll(cq, 1, axis=0)))
            carry_scr[q] = cq[tr - 1:tr, :]
            curs.append(cq)
        tok[gi] = _pre_lora(jnp.concatenate(curs, axis=0), jnp.concatenate(prevs, axis=0), p, width)

    def gates(gi):
        _pre_gates(tok[gi], p, ones_bd)

    def keys(gi):
        t = tok[gi]
        _pre_keys(t, p)
        for j, _, rs in chunks(gi):
            lds = t["ld"][rs]
            hi, mid, lo = _split3(lds)
            fac[j] = dict(lds=lds, cum=_dot(tril, hi) + _dot(tril, mid) + _dot(tril, lo))

    def decay(gi):
        t = tok[gi]
        for j, _, rs in chunks(gi):
            f = fac[j]
            cum = f["cum"]
            g_inv = jnp.exp(-cum)
            cum_end = cum[c - 1:c, :]
            g_rel = jnp.exp(cum_end - cum)
            f.update(a_t=t["av"][rs] * jnp.exp(cum - f["lds"]), r_t=t["r"][rs] * jnp.exp(cum),
                     b_t=t["bv"][rs] * g_inv, k_t=t["k2"][rs] * g_inv,
                     b_e=(t["bv"][rs] * g_rel).astype(BF16), k_e=(t["k2"][rs] * g_rel).astype(BF16),
                     g_end=jnp.exp(cum_end), v=t["v"][rs].astype(BF16))

    def scores(gi):
        for j, _, h in pairs(gi):
            f = fac[j]
            ar[j, h] = jnp.concatenate([f["a_t"][:, hs(h)], f["r_t"][:, hs(h)]], axis=0).astype(BF16)
            bk = jnp.concatenate([f["b_t"][:, hs(h)], f["k_t"][:, hs(h)]], axis=0).astype(BF16)
            sc[j, h] = _dot_nt(ar[j, h], bk)

    def lowers(gi):
        for j, _, h in pairs(gi):
            x = sc[j, h]
            vh = fac[j]["v"][:, hs(h)]
            pw[j, h] = jnp.where(strict, x[:c, :c], 0.0)
            nm[j, h] = pw[j, h]
            m_rb[j, h] = jnp.where(incl, x[c:, :c], 0.0).astype(BF16)
            lv[j, h] = _dot(jnp.where(strict, x[:c, c:], 0.0).astype(BF16), vh)
            mv[j, h] = _dot(jnp.where(incl, x[c:, c:], 0.0).astype(BF16), vh)
            vk[j, h] = _dot_tn(vh, fac[j]["k_e"][:, hs(h)])

    def square(gi):
        for j, _, h in pairs(gi):
            pb = pw[j, h].astype(BF16)
            pw[j, h] = _dot(pb, pb)

    def extend(gi):
        for j, _, h in pairs(gi):
            nm[j, h] = nm[j, h] + pw[j, h] + _dot(nm[j, h].astype(BF16), pw[j, h].astype(BF16))

    def chain(t, part):
        def project(gi):
            for j, q, h in sel(gi):
                s0[j, h] = s_ref[q, h]
                am[j, h] = _dot_nt(ar[j, h], s0[j, h].astype(BF16))

        def solve(gi):
            for j, q, h in sel(gi):
                w = am[j, h][:c] + lv[j, h]
                ub[j, h] = (w + _dot(nm[j, h].astype(BF16), w.astype(BF16))).astype(BF16)

        def update(gi):
            for j, q, h in sel(gi):
                f = fac[j]
                r0 = (q * per_seq + t) * c
                y_ref[r0:r0 + c, hs(h)] = am[j, h][c:] + _dot(m_rb[j, h], ub[j, h]) + mv[j, h]
                s_ref[q, h] = (s0[j, h] * f["g_end"][:, hs(h)] + _dot_tn(ub[j, h], f["b_e"][:, hs(h)])
                               + vk[j, h])

        def sel(gi):
            return [(j, q, h) for j, q, h in pairs(gi) if j == q * per_seq + t]

        return dict(project=project, solve=solve, update=update)[part]

    def post_mean(gi):
        dlt[gi] = _post_mean(y_ref[grows(gi), :], ones_bd, head)

    def post_stats(gi):
        stats[gi] = _post_stats(dlt[gi], tok[gi], p, ones_bd, head)

    def post_out(gi):
        out = _post_out(dlt[gi], *stats[gi], tok[gi], p, proj_ref[...])
        for i, q in enumerate(seqs(gi)):
            out_ref[q] = out[i * tr:(i + 1) * tr]

    front = [pre, gates, keys, decay, scores, lowers]
    middle = [st for _ in range(steps - 1) for st in (square, extend)]
    middle += [chain(t, part) for t in range(per_seq) for part in ("project", "solve", "update")]
    middle += [post_mean, post_stats, post_out]
    order = [(st, 0) for st in front]
    for gi in range(n_groups):
        nxt = [(st, gi + 1) for st in front] if gi + 1 < n_groups else []
        order += _interleave([(st, gi) for st in middle], nxt)
    for st, gi in order:
        st(gi)


_RWKV_VEC_NAMES = ("w0", "a0", "k_k", "k_a", "r_k", "ln_w", "ln_b")
_RWKV_CONSTS = ("mu", "vecs", "w2p", "a2p", "g2")


def _rwkv_params(mu_ref, vecs_ref, w2p_ref, a2p_ref, g2_ref):
    p = {n: vecs_ref[i:i + 1, :] for i, n in enumerate(_RWKV_VEC_NAMES)}
    p.update(mu=mu_ref[...], w2p=w2p_ref[...], a2p=a2p_ref[...], g2=g2_ref[...])
    return p


def _rwkv_seq_kernel(cur_ref, shift0_ref, s0_ref, *rest, chunk, head):
    np_ = len(_RWKV_CONSTS)
    p = _rwkv_params(*rest[:np_])
    proj_ref, ones_ref, tril_ref = rest[np_:np_ + 3]
    out_ref, st_ref = rest[np_ + 3:np_ + 5]
    carry_scr, y_scr = rest[np_ + 5:]

    @pl.when(pl.program_id(1) == 0)
    def _():
        carry_scr[...] = shift0_ref[...]
        st_ref[...] = s0_ref[...]

    _rwkv_seq_block(cur_ref, carry_scr, p, ones_ref[...], tril_ref[...], proj_ref, st_ref, y_scr, out_ref,
                    head, chunk, WKV_GROUPS)


def _rwkv_rows_pre_kernel(cur_ref, prev_ref, mu_ref, vecs_ref, w2p_ref, a2p_ref, g2_ref, ones_ref,
                          r_ref, k_ref, v_ref, g_ref, t_ref, *, width):
    p = _rwkv_params(mu_ref, vecs_ref, w2p_ref, a2p_ref, g2_ref)
    r, ld, k2, v, a, b, g = _rwkv_pre(cur_ref[...], prev_ref[...], p, ones_ref[...], width)
    r_ref[...] = r
    k_ref[...] = k2
    v_ref[...] = v
    g_ref[...] = g
    for i, val in enumerate((r, ld, k2, v, a, b)):
        t_ref[i] = val.T


def _wkv_step_kernel(t_ref, s_ref, y_ref, so_ref):
    r = t_ref[0]
    dec = jnp.exp(t_ref[1])
    k = t_ref[2]
    a = t_ref[4]
    b = t_ref[5]

    def body(i, carry):
        s = s_ref[i]
        sa = jnp.sum(s * a, axis=0, keepdims=True)
        s2 = s * dec + sa * b + t_ref[3, pl.ds(i, 1), :] * k
        so_ref[i] = s2
        y_ref[pl.ds(i, 1), :] = jnp.sum(s2 * r, axis=0, keepdims=True)
        return carry

    lax.fori_loop(0, s_ref.shape[0], body, 0, unroll=V7X_SUBLANES)


def _rwkv_rows_post_kernel(yt_ref, r_ref, k_ref, v_ref, g_ref, mu_ref, vecs_ref, w2p_ref, a2p_ref, g2_ref,
                           proj_ref, ones_ref, o_ref, *, head):
    p = _rwkv_params(mu_ref, vecs_ref, w2p_ref, a2p_ref, g2_ref)
    o_ref[...] = _rwkv_post(yt_ref[...].T, r_ref[...], k_ref[...], v_ref[...], g_ref[...], p,
                            ones_ref[...], proj_ref[...], head)


def _rwkv_param_list(rw):
    return [rw[n] for n in _RWKV_CONSTS]


def _rwkv_seq_call(cur3, shift0, s0, rw, *, tr, nseq):
    nb, length, n_shift = cur3.shape
    nh, head = s0.shape[1], s0.shape[2]
    width = nh * head
    d = rw["proj"].shape[1]
    params = _rwkv_param_list(rw)
    chunk = min(WKV_CHUNK, tr)
    tril = (jnp.arange(chunk)[:, None] >= jnp.arange(chunk)[None, :]).astype(BF16)
    consts = params + [rw["proj"], rw["ones_bd"], tril]
    kern = functools.partial(_rwkv_seq_kernel, chunk=chunk, head=head)
    return pl.pallas_call(
        kern,
        out_shape=(jax.ShapeDtypeStruct((nb, length, d), F32),
                   jax.ShapeDtypeStruct((nb, nh, head, head), F32)),
        grid=(nb // nseq, length // tr),
        in_specs=[pl.BlockSpec((nseq, tr, n_shift), lambda q, i: (q, i, 0)),
                  pl.BlockSpec((nseq, 1, n_shift), lambda q, i: (q, 0, 0)),
                  pl.BlockSpec((nseq, nh, head, head), lambda q, i: (q, 0, 0, 0))]
                 + [pl.BlockSpec(c.shape, lambda q, i, nd=c.ndim: (0,) * nd) for c in consts],
        out_specs=(pl.BlockSpec((nseq, tr, d), lambda q, i: (q, i, 0)),
                   pl.BlockSpec((nseq, nh, head, head), lambda q, i: (q, 0, 0, 0))),
        scratch_shapes=[pltpu.VMEM((nseq, 1, n_shift), F32), pltpu.VMEM((nseq * tr, width), F32)],
        compiler_params=_cparams(("arbitrary", "arbitrary"), 40 * 2**20),
        name="rwkv_seq",
    )(cur3, shift0.reshape(nb, 1, n_shift), s0, *consts)


def _rwkv_rows_call(cur, prev, s0_t, rw):
    rows, n_shift = cur.shape
    nh, head = s0_t.shape[0], s0_t.shape[1]
    width = nh * head
    d = rw["proj"].shape[1]
    params = _rwkv_param_list(rw)
    full = lambda shape: pl.BlockSpec(shape, lambda i: (0,) * len(shape))
    vec = jax.ShapeDtypeStruct((rows, width), F32)
    n_t = 6

    pre_consts = params + [rw["ones_bd"]]
    r, k2, v, g, t = pl.pallas_call(
        functools.partial(_rwkv_rows_pre_kernel, width=width),
        out_shape=(vec,) * 4 + (jax.ShapeDtypeStruct((n_t, width, rows), F32),),
        grid=(1,),
        in_specs=[full(cur.shape), full(prev.shape)] + [full(c.shape) for c in pre_consts],
        out_specs=(full((rows, width)),) * 4 + (full((n_t, width, rows)),),
        compiler_params=_cparams(("arbitrary",), 32 * 2**20),
        name="rwkv_rows_pre",
    )(cur, prev, *pre_consts)

    sblk = pl.BlockSpec((None, head, head, rows), lambda h: (h, 0, 0, 0))
    yt, s1_t = pl.pallas_call(
        _wkv_step_kernel,
        out_shape=(jax.ShapeDtypeStruct((width, rows), F32), jax.ShapeDtypeStruct(s0_t.shape, F32)),
        grid=(nh,),
        in_specs=[pl.BlockSpec((n_t, head, rows), lambda h: (0, h, 0)), sblk],
        out_specs=(pl.BlockSpec((head, rows), lambda h: (h, 0)), sblk),
        compiler_params=_cparams(("parallel",), 32 * 2**20),
        name="wkv_step",
    )(t, s0_t)

    post_consts = params + [rw["proj"], rw["ones_bd"]]
    out = pl.pallas_call(
        functools.partial(_rwkv_rows_post_kernel, head=head),
        out_shape=jax.ShapeDtypeStruct((rows, d), F32),
        grid=(1,),
        in_specs=[full((width, rows))] + [full((rows, width))] * 4 + [full(c.shape) for c in post_consts],
        out_specs=full((rows, d)),
        compiler_params=_cparams(("arbitrary",), 32 * 2**20),
        name="rwkv_rows_post",
    )(yt, r, k2, v, g, *post_consts)
    return out, s1_t


def _merge_kernel(h_ref, gates_ref, ys5_ref, yrw_ref, gt2_ref, wout_ref,
                  sh_ref, sc_ref, gt_ref, g_ref, w1_ref, w3_ref, w2_ref, gfin_ref, o_ref, *, n_chunks,
                  tiles_per_seq):
    row = functools.partial(_mod_row, tiles_per_seq=tiles_per_seq)
    d = h_ref.shape[1]
    gates = gates_ref[...]
    m = jax.nn.sigmoid(gates[:, :d]) * ys5_ref[...] + jax.nn.sigmoid(gates[:, d:]) * yrw_ref[...]
    h2 = h_ref[...] + row(gt2_ref) * _dot(m.astype(BF16), wout_ref[...])
    h3 = _ffn_core(h2, row(sh_ref), row(sc_ref), row(gt_ref), g_ref[...], w1_ref, w3_ref, w2_ref, n_chunks)
    o_ref[...] = _rms(h3, gfin_ref[...])


def _merge_call(h, gates, ys5, yrw, mods, ks, w_out, g, w1, w3, w2, g_fin, *, tm):
    rows, d = h.shape
    d_ff = w1.shape[1]
    kern = functools.partial(_merge_kernel, n_chunks=_ffn_chunks(d_ff), tiles_per_seq=mods.tiles_per_seq)
    row_spec = lambda w: pl.BlockSpec((tm, w), lambda i: (i, 0))
    weights = (3 * d * d_ff + d * d) * 2
    tiles = 2 * tm * 6 * d * 4 + 4 * tm * d_ff * 4 + 6 * tm * d * 4
    ys5_spec = row_spec(d) if mods.tiles_per_seq is None else _time_major(rows, d, tm, mods.tiles_per_seq)[1]
    return pl.pallas_call(
        kern,
        out_shape=jax.ShapeDtypeStruct((rows, d), F32),
        grid=(rows // tm,),
        in_specs=[row_spec(d), row_spec(2 * d), ys5_spec, row_spec(d),
                  mods.spec(ks[0]), _const_spec(w_out.shape),
                  mods.spec(ks[1]), mods.spec(ks[2]), mods.spec(ks[3]),
                  _const_spec((1, d)), _const_spec(w1.shape), _const_spec(w3.shape), _const_spec(w2.shape),
                  _const_spec((1, d))],
        out_specs=row_spec(d),
        compiler_params=_cparams(("parallel",), weights + tiles + 8 * 2**20),
        name="merge_ffn2",
    )(h, gates, ys5, yrw, mods.mod, w_out, mods.mod, mods.mod, mods.mod, g.reshape(1, d), w1, w3, w2,
      g_fin.reshape(1, d))


def _pick_tile(n, target):
    t = min(n, target)
    while n % t:
        t -= V7X_SUBLANES
    return t


def _layer(x, mod, mod_row0, s5_re0, s5_im0, wkv0, shift0, w, *, sequential):
    nb, length, d = x.shape
    rows = nb * length
    g, n = s5_re0.shape[1], s5_re0.shape[2]
    ns = g * n
    n_shift = shift0.shape[1]
    s5_width = w["s5"]["d"].shape[1]
    widths = (s5_width, n_shift, 2 * d)

    tm = _pick_tile(length if sequential else rows, 512)
    tm2 = _pick_tile(tm, 512)
    mods = _ModRows(mod, mod_row0, nb, tm, length // tm if sequential else None)
    mods2 = _ModRows(mod, mod_row0, nb, tm2, length // tm2 if sequential else None)

    x2 = x.reshape(rows, d)
    h1 = _ffn_call(x2, mods, (0, 1, 2), w["g_ffn1"], *w["ffn1"], tm=tm)
    s5_in, cur, gates = _proj_call(h1, mods, (3, 4), w["g_mix"], w["w_in"], widths, tm=tm)

    h0re = s5_re0.reshape(nb, ns)
    h0im = s5_im0.reshape(nb, ns)
    if sequential:
        ts = _pick_tile(length, S5_STEP_ROWS // nb)
        y_s5, hre, him = _s5_seq_call(s5_in, h0re, h0im, w["a_re"], w["a_im"], w["s5"], ts=ts)
        nseq = WKV_SEQS if nb % WKV_SEQS == 0 else 1
        tr = _pick_tile(length, WKV_BLOCK_CHUNKS * WKV_CHUNK // nseq)
        y_rw, wkv1 = _rwkv_seq_call(cur.reshape(nb, length, n_shift), shift0, wkv0, w["rwkv"], tr=tr, nseq=nseq)
        y_rw = y_rw.reshape(rows, d)
    else:
        y_s5, hre, him = _s5_rows_call(s5_in, h0re, h0im, w["a_re"], w["a_im"], w["s5"])
        y_rw, wkv1 = _rwkv_rows_call(cur, shift0, wkv0, w["rwkv"])

    y = _merge_call(h1, gates, y_s5, y_rw, mods2, (5, 6, 7, 8), w["w_out"],
                    w["g_ffn2"], *w["ffn2"], w["g_final"], tm=tm2)

    shift1 = cur.reshape(nb, length, n_shift)[:, -1]
    return (y.reshape(nb, length, d), hre.reshape(nb, g, n), him.reshape(nb, g, n), wkv1, shift1)


def kernel(x_prompt, x_sample, c_prompt, c_sample, state_s5_re, state_s5_im, state_wkv, state_shift, w_ada, b_ada, g_ffn1, g_mix, g_ffn2, g_final, ffn1_w1, ffn1_w3, ffn1_w2, ffn2_w1, ffn2_w3, ffn2_w2, w_in, mu_shift, s5_lam_re, s5_lam_im, s5_log_dt, s5_b_re, s5_b_im, s5_c_re, s5_c_im, s5_d, s5_glu_v, s5_glu_g, rwkv_w0, rwkv_w2, rwkv_a0, rwkv_a2, rwkv_g2, rwkv_k_k, rwkv_k_a, rwkv_r_k, rwkv_ln_w, rwkv_ln_b, rwkv_proj, w_out):
    bp, _, d = x_prompt.shape
    bs = x_sample.shape[0]
    g, n = s5_lam_re.shape
    nh, head = rwkv_r_k.shape
    width = nh * head
    n_shift = mu_shift.shape[0]
    assert bs % bp == 0, "sample rows come first in the modulation array; prompt rows must stay block-aligned"
    bf = lambda t: t.astype(BF16)

    a_re, a_im, wb, wc_re, wc_im = _s5_prep_call(
        s5_lam_re, s5_lam_im, s5_log_dt, jnp.swapaxes(s5_b_re, 1, 2), jnp.swapaxes(s5_b_im, 1, 2),
        s5_c_re, s5_c_im, bp)
    w_lora = rwkv_w2.shape[0]
    a_lora = rwkv_a2.shape[0]
    w2p = jnp.concatenate([rwkv_w2, jnp.zeros((a_lora, width), F32)], axis=0)
    a2p = jnp.concatenate([jnp.zeros((w_lora, width), F32), rwkv_a2], axis=0)
    vec_rows = [rwkv_w0, rwkv_a0, rwkv_k_k, rwkv_k_a, rwkv_r_k.reshape(width), rwkv_ln_w, rwkv_ln_b]
    vecs = jnp.stack(vec_rows + [jnp.zeros_like(rwkv_w0)] * (V7X_SUBLANES - len(vec_rows)))
    rw = dict(mu=mu_shift.reshape(1, n_shift), vecs=vecs, w2p=bf(w2p), a2p=bf(a2p), g2=bf(rwkv_g2),
              proj=bf(rwkv_proj),
              ones_bd=jnp.kron(jnp.eye(nh, dtype=F32), jnp.ones((head, head), F32)).astype(BF16))
    w = dict(
        g_ffn1=g_ffn1, g_mix=g_mix, g_ffn2=g_ffn2, g_final=g_final,
        ffn1=(bf(ffn1_w1), bf(ffn1_w3), bf(ffn1_w2)), ffn2=(bf(ffn2_w1), bf(ffn2_w3), bf(ffn2_w2)),
        w_in=bf(w_in), w_out=bf(w_out),
        a_re=a_re, a_im=a_im,
        s5=dict(wb=wb, wc_re=wc_re, wc_im=wc_im, d=s5_d.reshape(1, -1), glu_v=bf(s5_glu_v), glu_g=bf(s5_glu_g)),
        rwkv=rw,
    )

    mod = _mod_call(jnp.concatenate([c_sample, c_prompt], axis=0), w_ada, b_ada)

    z_s5 = jnp.zeros((bp, g, n), state_s5_re.dtype)
    z_wkv = jnp.zeros((bp, nh, head, head), state_wkv.dtype)
    z_shift = jnp.zeros((bp, n_shift), state_shift.dtype)
    y_p, s5re_p, s5im_p, wkv_p, shift_p = _layer(x_prompt, mod, bs, z_s5, z_s5, z_wkv, z_shift, w, sequential=True)
    y_s, s5re_s, s5im_s, wkv_s_t, shift_s = _layer(x_sample, mod, 0, state_s5_re, state_s5_im,
                                                   jnp.transpose(state_wkv, (1, 2, 3, 0)), state_shift, w,
                                                   sequential=False)
    wkv_s = jnp.transpose(wkv_s_t, (3, 0, 1, 2))
    return (y_p, y_s, s5re_p, s5im_p, wkv_p, shift_p, s5re_s, s5im_s, wkv_s, shift_s)
```

```python
import functools
import math

import jax
import jax.numpy as jnp
from jax import lax
from jax.experimental import pallas as pl
from jax.experimental.pallas import tpu as pltpu

F32 = jnp.float32
BF16 = jnp.bfloat16

NORM_EPS = 1e-6
GN_EPS = 64e-5
N_MOD = 9

V7X_LANES = 128
V7X_SUBLANES = 8
V7X_MXU_DIM = 256
V7X_VMEM_BYTES = 64 * 1024 * 1024
VMEM_CAP_BYTES = V7X_VMEM_BYTES - 8 * 1024 * 1024

S5_STEP_ROWS = 1024
S5_OUT_PARTS = 1
WKV_CHUNK = 64
WKV_BLOCK_CHUNKS = 8
WKV_SEQS = 8
WKV_GROUPS = 1


def _cparams(semantics, vmem_bytes):
    return pltpu.CompilerParams(
        dimension_semantics=semantics,
        vmem_limit_bytes=int(min(vmem_bytes, VMEM_CAP_BYTES)),
    )


def _const_spec(shape):
    nd = len(shape)
    return pl.BlockSpec(shape, lambda *_: (0,) * nd, pipeline_mode=pl.Buffered(1))


def _dot(a, b):
    return jnp.dot(a, b, preferred_element_type=F32)


def _dot_nt(a, b):
    return lax.dot_general(a, b, (((1,), (1,)), ((), ())), preferred_element_type=F32)


def _dot_tn(a, b):
    return lax.dot_general(a, b, (((0,), (0,)), ((), ())), preferred_element_type=F32)


def _split3(x):
    hi = x.astype(BF16)
    r1 = x - hi.astype(F32)
    mid = r1.astype(BF16)
    lo = (r1 - mid.astype(F32)).astype(BF16)
    return hi, mid, lo


def _rms(x, g):
    ms = jnp.mean(x * x, axis=-1, keepdims=True)
    return x * lax.rsqrt(ms + NORM_EPS) * g


def _softplus(x):
    return jnp.maximum(x, 0.0) + jnp.log(1.0 + jnp.exp(-jnp.abs(x)))


def _mod_kernel(c_ref, w_ref, b_ref, o_ref):
    c = c_ref[...]
    s = (c * jax.nn.sigmoid(c)).astype(BF16)
    o_ref[...] = _dot(s, w_ref[...].astype(BF16)) + b_ref[...]


def _mod_call(c, w_ada, b_ada):
    rows, d = c.shape
    n = w_ada.shape[1]
    return pl.pallas_call(
        _mod_kernel,
        out_shape=jax.ShapeDtypeStruct((n // d, rows, d), F32),
        grid=(n // d,),
        in_specs=[
            pl.BlockSpec((rows, d), lambda j: (0, 0)),
            pl.BlockSpec((d, d), lambda j: (0, j)),
            pl.BlockSpec((1, d), lambda j: (0, j)),
        ],
        out_specs=pl.BlockSpec((None, rows, d), lambda j: (j, 0, 0)),
        compiler_params=_cparams(("parallel",), 32 * 2**20),
        name="adaln_mod",
    )(c, w_ada, b_ada.reshape(1, n))


class _ModRows:
    def __init__(self, mod, row0, nrows, tm, tiles_per_seq):
        self.mod, self.row0, self.nrows, self.tm, self.tiles_per_seq = mod, row0, nrows, tm, tiles_per_seq

    def spec(self, k):
        d = self.mod.shape[2]
        if self.tiles_per_seq is None:
            blk0 = self.row0 // self.tm
            return pl.BlockSpec((None, self.tm, d), lambda i: (k, blk0 + i, 0))
        blk0 = self.row0 // self.nrows
        return pl.BlockSpec((None, self.nrows, d), lambda i: (k, blk0, 0))


def _mod_row(ref, tiles_per_seq):
    if tiles_per_seq is None:
        return ref[...]
    return ref[pl.ds(pl.program_id(0) // tiles_per_seq, 1), :]


def _ffn_core(x, sh, sc, gt, g, w1_ref, w3_ref, w2_ref, n_chunks):
    rows = x.shape[0]
    n_split = 2 if rows % (2 * V7X_SUBLANES) == 0 and sh.shape[0] == 1 else 1
    step = rows // n_split
    bounds = _ffn_bounds(w1_ref.shape[1], n_chunks)
    outs = []
    for r in range(n_split):
        xr = x[r * step:(r + 1) * step]
        u = (_rms(xr, g) * (1.0 + sc) + sh).astype(BF16)
        acc = None
        for lo, hi in zip(bounds[:-1], bounds[1:]):
            a = _dot(u, w1_ref[:, lo:hi])
            b = _dot(u, w3_ref[:, lo:hi])
            hm = (a * jax.nn.sigmoid(a) * b).astype(BF16)
            part = _dot(hm, w2_ref[lo:hi, :])
            acc = part if acc is None else acc + part
        outs.append(xr + (0.5 * gt) * acc)
    return outs[0] if n_split == 1 else jnp.concatenate(outs, axis=0)


def _ffn_bounds(d_ff, n_chunks):
    tiles = -(-d_ff // V7X_MXU_DIM)
    return [min(d_ff, ((tiles * c + n_chunks - 1) // n_chunks) * V7X_MXU_DIM) for c in range(n_chunks + 1)]


def _ffn_kernel(x_ref, sh_ref, sc_ref, gt_ref, g_ref, w1_ref, w3_ref, w2_ref, *rest, n_chunks, tiles_per_seq):
    n_cast = len(rest) // 2
    o_ref = rest[n_cast]
    row = functools.partial(_mod_row, tiles_per_seq=tiles_per_seq)
    o_ref[...] = _ffn_core(x_ref[...], row(sh_ref), row(sc_ref), row(gt_ref), g_ref[...],
                           w1_ref, w3_ref, w2_ref, n_chunks)
    for src, dst in zip(rest[:n_cast], rest[n_cast + 1:]):
        dst[...] = src[...].astype(BF16)


def _cast_blocks(arr, n_steps):
    bf16_rows = 2 * V7X_SUBLANES
    nblk = n_steps
    while arr.shape[0] % nblk or (arr.shape[0] // nblk) % bf16_rows:
        nblk //= 2
    per = n_steps // nblk
    return pl.BlockSpec((arr.shape[0] // nblk, arr.shape[1]), lambda i: (i // per, 0))


def _ffn_chunks(d_ff):
    return 2 if d_ff % (2 * V7X_LANES) == 0 else 1


def _ffn_call(x2, mods, ks, g, w1, w3, w2, *, tm, casts=()):
    rows, d = x2.shape
    d_ff = w1.shape[1]
    n_steps = rows // tm
    kern = functools.partial(_ffn_kernel, n_chunks=_ffn_chunks(d_ff), tiles_per_seq=mods.tiles_per_seq)
    weights = 3 * d * d_ff * 2
    tiles = 4 * tm * d * 4 + 4 * tm * d_ff * 4
    cast_specs = [_cast_blocks(a, n_steps) for a in casts]
    cast_bytes = sum(2 * 6 * s.block_shape[0] * s.block_shape[1] for s in cast_specs)
    out = pl.pallas_call(
        kern,
        out_shape=(jax.ShapeDtypeStruct((rows, d), F32),) + tuple(jax.ShapeDtypeStruct(a.shape, BF16) for a in casts),
        grid=(n_steps,),
        in_specs=[
            pl.BlockSpec((tm, d), lambda i: (i, 0)),
            mods.spec(ks[0]), mods.spec(ks[1]), mods.spec(ks[2]),
            _const_spec((1, d)),
            _const_spec(w1.shape), _const_spec(w3.shape), _const_spec(w2.shape),
        ] + cast_specs,
        out_specs=(pl.BlockSpec((tm, d), lambda i: (i, 0)),) + tuple(cast_specs),
        compiler_params=_cparams(("arbitrary",), weights + tiles + cast_bytes + 8 * 2**20),
        name="ffn1",
    )(x2, mods.mod, mods.mod, mods.mod, g.reshape(1, d), w1, w3, w2, *casts)
    return out[0], list(out[1:])


def _proj_kernel(h_ref, sh_ref, sc_ref, g_ref, win_ref, s5_ref, cur_ref, gates_ref, *, tiles_per_seq):
    row = functools.partial(_mod_row, tiles_per_seq=tiles_per_seq)
    u = (_rms(h_ref[...], g_ref[...]) * (1.0 + row(sc_ref)) + row(sh_ref)).astype(BF16)
    n0 = s5_ref.shape[1]
    n1 = n0 + cur_ref.shape[1]
    s5_ref[...] = _dot(u, win_ref[:, :n0])
    cur_ref[...] = _dot(u, win_ref[:, n0:n1])
    gates_ref[...] = _dot(u, win_ref[:, n1:])


def _time_major(rows, width, tm, tiles_per_seq):
    length = tiles_per_seq * tm
    spec = pl.BlockSpec((tm, width), lambda i: (i % tiles_per_seq, i // tiles_per_seq))
    return (length, rows // length * width), spec


def _proj_call(h, mods, ks, g, w_in, widths, *, tm):
    rows, d = h.shape
    n_in = w_in.shape[1]
    shapes = [(rows, w) for w in widths]
    specs = [pl.BlockSpec((tm, w), lambda i: (i, 0)) for w in widths]
    if mods.tiles_per_seq is not None:
        shapes[0], specs[0] = _time_major(rows, widths[0], tm, mods.tiles_per_seq)
    return pl.pallas_call(
        functools.partial(_proj_kernel, tiles_per_seq=mods.tiles_per_seq),
        out_shape=tuple(jax.ShapeDtypeStruct(s, F32) for s in shapes),
        grid=(rows // tm,),
        in_specs=[
            pl.BlockSpec((tm, d), lambda i: (i, 0)),
            mods.spec(ks[0]), mods.spec(ks[1]),
            _const_spec((1, d)),
            _const_spec(w_in.shape),
        ],
        out_specs=tuple(specs),
        compiler_params=_cparams(("parallel",), d * n_in * 2 + 3 * tm * (d + n_in) * 4 + 8 * 2**20),
        name="mix_proj",
    )(h, mods.mod, mods.mod, g.reshape(1, d), w_in)


def _s5_prep_kernel(lre_ref, lim_ref, ldt_ref, bre_ref, bim_ref, cre_ref, cim_ref,
                    are_ref, aim_ref, wb_ref, wcre_ref, wcim_ref, *, gpb):
    lre = lre_ref[...]
    lim = lim_ref[...]
    dt = jnp.exp(ldt_ref[...])
    mag = jnp.exp(lre * dt)
    ang = lim * dt
    a_re = mag * jnp.cos(ang)
    a_im = mag * jnp.sin(ang)
    den = lre * lre + lim * lim
    num_re = a_re - 1.0
    k_re = (num_re * lre + a_im * lim) / den
    k_im = (a_im * lre - num_re * lim) / den
    b_re = bre_ref[...]
    b_im = bim_ref[...]
    bb_re = (k_re * b_re - k_im * b_im).astype(BF16)
    bb_im = (k_re * b_im + k_im * b_re).astype(BF16)
    c_re = cre_ref[...].astype(BF16)
    c_im = (-cim_ref[...]).astype(BF16)
    g, c, n = b_re.shape
    nb = are_ref.shape[0]
    s = gpb * n
    wb_ref[...] = jnp.zeros(wb_ref.shape, BF16)
    wcre_ref[...] = jnp.zeros(wcre_ref.shape, BF16)
    wcim_ref[...] = jnp.zeros(wcim_ref.shape, BF16)
    for gi in range(g):
        k, gl = divmod(gi, gpb)
        rows = slice(gl * c, (gl + 1) * c)
        cols = slice(gl * n, (gl + 1) * n)
        are_ref[:, gi * n:(gi + 1) * n] = jnp.broadcast_to(a_re[gi], (nb, n))
        aim_ref[:, gi * n:(gi + 1) * n] = jnp.broadcast_to(a_im[gi], (nb, n))
        wb_ref[k, rows, cols] = bb_re[gi]
        wb_ref[k, rows, s + gl * n:s + (gl + 1) * n] = bb_im[gi]
        wcre_ref[k, rows, cols] = c_re[gi]
        wcim_ref[k, rows, cols] = c_im[gi]


def _s5_prep_call(lam_re, lam_im, log_dt, b_re_t, b_im_t, c_re, c_im, nb):
    g, n = lam_re.shape
    c = b_re_t.shape[1]
    gpb = max(1, min(g, V7X_MXU_DIM // c))
    nk = g // gpb
    return pl.pallas_call(
        functools.partial(_s5_prep_kernel, gpb=gpb),
        out_shape=(jax.ShapeDtypeStruct((nb, g * n), F32), jax.ShapeDtypeStruct((nb, g * n), F32),
                   jax.ShapeDtypeStruct((nk, gpb * c, 2 * gpb * n), BF16),
                   jax.ShapeDtypeStruct((nk, gpb * c, gpb * n), BF16),
                   jax.ShapeDtypeStruct((nk, gpb * c, gpb * n), BF16)),
        name="s5_discretise",
    )(lam_re.reshape(g, 1, n), lam_im.reshape(g, 1, n), log_dt.reshape(g, 1, 1), b_re_t, b_im_t, c_re, c_im)


def _s5_input_matmul(ub, wb_ref, bure_ref, buim_ref):
    nk, kin, two_s = wb_ref.shape
    s = two_s // 2
    for k in range(nk):
        res = _dot(ub[:, k * kin:(k + 1) * kin], wb_ref[k])
        bure_ref[:, k * s:(k + 1) * s] = res[:, :s]
        buim_ref[:, k * s:(k + 1) * s] = res[:, s:]


def _s5_output(xre, xim, u, wcre_ref, wcim_ref, d_row):
    nk, kout, s = wcre_ref.shape
    xre_b = xre.astype(BF16)
    xim_b = xim.astype(BF16)
    ys = []
    for k in range(nk):
        ys.append(_dot_nt(xre_b[:, k * s:(k + 1) * s], wcre_ref[k])
                  + _dot_nt(xim_b[:, k * s:(k + 1) * s], wcim_ref[k]))
    y = jnp.concatenate(ys, axis=-1) + d_row * u
    return jax.nn.gelu(y)


def _s5_glu(zb, gv_ref, gg_ref):
    return _dot(zb, gv_ref[...]) * jax.nn.sigmoid(_dot(zb, gg_ref[...]))


def _s5_seq_kernel(u_ref, h0re_ref, h0im_ref, are_ref, aim_ref, wb_ref, wcre_ref, wcim_ref, d_ref,
                   gv_ref, gg_ref,
                   o_ref, hre_ref, him_ref,
                   bure_scr, buim_scr, xre_scr, xim_scr, uslab_scr, oslab_scr, *, lane_chunk, n_parts):
    nb = hre_ref.shape[0]
    ts = u_ref.shape[0]
    din = u_ref.shape[1] // nb
    dout = o_ref.shape[1] // nb

    @pl.when(pl.program_id(0) == 0)
    def _():
        hre_ref[...] = h0re_ref[...]
        him_ref[...] = h0im_ref[...]

    for b in range(nb):
        for j in range(din // V7X_LANES):
            col = b * din + j * V7X_LANES
            uslab_scr[j, pl.ds(b, ts, stride=nb), :] = u_ref[:, col:col + V7X_LANES]
    u_tm = jnp.concatenate([uslab_scr[j] for j in range(din // V7X_LANES)], axis=1)
    _s5_input_matmul(u_tm.astype(BF16), wb_ref, bure_scr, buim_scr)

    ns = are_ref.shape[1]
    lanes = [slice(lc * lane_chunk, (lc + 1) * lane_chunk) for lc in range(ns // lane_chunk)]
    state = [(hre_ref[:, sl], him_ref[:, sl]) for sl in lanes]
    tp = ts // n_parts
    for part in range(n_parts):
        for lc, sl in enumerate(lanes):
            ar = are_ref[:, sl]
            ai = aim_ref[:, sl]
            sr, si = state[lc]
            for t in range(part * tp, (part + 1) * tp):
                rs = slice(t * nb, (t + 1) * nb)
                nr = ar * sr - ai * si + bure_scr[rs, sl]
                ni = ar * si + ai * sr + buim_scr[rs, sl]
                xre_scr[rs, sl] = nr
                xim_scr[rs, sl] = ni
                sr, si = nr, ni
            state[lc] = (sr, si)
        pr = slice(part * tp * nb, (part + 1) * tp * nb)
        z_tm = _s5_output(xre_scr[pr, :], xim_scr[pr, :], u_tm[pr], wcre_ref, wcim_ref, d_ref[...])
        out = _s5_glu(z_tm.astype(BF16), gv_ref, gg_ref)
        for j in range(dout // V7X_LANES):
            oslab_scr[j, pr, :] = out[:, j * V7X_LANES:(j + 1) * V7X_LANES]
    for lc, sl in enumerate(lanes):
        hre_ref[:, sl], him_ref[:, sl] = state[lc]
    for b in range(nb):
        for j in range(dout // V7X_LANES):
            col = b * dout + j * V7X_LANES
            o_ref[:, col:col + V7X_LANES] = oslab_scr[j, pl.ds(b, ts, stride=nb), :]


def _s5_rows_kernel(u_ref, h0re_ref, h0im_ref, are_ref, aim_ref, wb_ref, wcre_ref, wcim_ref, d_ref,
                    gv_ref, gg_ref, o_ref, hre_ref, him_ref, bure_scr, buim_scr):
    u = u_ref[...]
    _s5_input_matmul(u.astype(BF16), wb_ref, bure_scr, buim_scr)
    ar = are_ref[0:1, :]
    ai = aim_ref[0:1, :]
    sr = h0re_ref[...]
    si = h0im_ref[...]
    nr = ar * sr - ai * si + bure_scr[...]
    ni = ar * si + ai * sr + buim_scr[...]
    hre_ref[...] = nr
    him_ref[...] = ni
    z = _s5_output(nr, ni, u, wcre_ref, wcim_ref, d_ref[...])
    o_ref[...] = _s5_glu(z.astype(BF16), gv_ref, gg_ref)


def _s5_weight_specs(ws):
    return [_const_spec(w.shape) for w in ws]


def _s5_seq_call(s5_in, h0re, h0im, a_re, a_im, sw, *, ts):
    nb, ns = a_re.shape
    length = s5_in.shape[0]
    din = s5_in.shape[1] // nb
    d = sw["glu_v"].shape[1]
    rows = nb * ts
    ws = [sw["wb"], sw["wc_re"], sw["wc_im"], sw["d"], sw["glu_v"], sw["glu_g"]]
    kern = functools.partial(_s5_seq_kernel, lane_chunk=4 * V7X_LANES, n_parts=S5_OUT_PARTS)
    return pl.pallas_call(
        kern,
        out_shape=(jax.ShapeDtypeStruct((length, nb * d), F32),
                   jax.ShapeDtypeStruct((nb, ns), F32), jax.ShapeDtypeStruct((nb, ns), F32)),
        grid=(length // ts,),
        in_specs=[pl.BlockSpec((ts, nb * din), lambda c: (c, 0)),
                  _const_spec((nb, ns)), _const_spec((nb, ns)), _const_spec((nb, ns)), _const_spec((nb, ns))]
                 + _s5_weight_specs(ws),
        out_specs=(pl.BlockSpec((ts, nb * d), lambda c: (c, 0)),
                   pl.BlockSpec((nb, ns), lambda c: (0, 0)), pl.BlockSpec((nb, ns), lambda c: (0, 0))),
        scratch_shapes=[pltpu.VMEM((rows, ns), F32) for _ in range(4)]
                       + [pltpu.VMEM((din // V7X_LANES, rows, V7X_LANES), F32),
                          pltpu.VMEM((d // V7X_LANES, rows, V7X_LANES), F32)],
        compiler_params=_cparams(("arbitrary",), 48 * 2**20),
        name="s5_seq",
    )(s5_in, h0re, h0im, a_re, a_im, *ws)


def _s5_rows_call(s5_in, h0re, h0im, a_re, a_im, sw):
    rows, din = s5_in.shape
    ns = a_re.shape[1]
    d = sw["glu_v"].shape[1]
    ws = [sw["wb"], sw["wc_re"], sw["wc_im"], sw["d"], sw["glu_v"], sw["glu_g"]]
    full = lambda shape: pl.BlockSpec(shape, lambda i: (0,) * len(shape))
    return pl.pallas_call(
        _s5_rows_kernel,
        out_shape=(jax.ShapeDtypeStruct((rows, d), F32),
                   jax.ShapeDtypeStruct((rows, ns), F32), jax.ShapeDtypeStruct((rows, ns), F32)),
        grid=(1,),
        in_specs=[full((rows, din)), full((rows, ns)), full((rows, ns)), full(a_re.shape), full(a_im.shape)]
                 + [full(w.shape) for w in ws],
        out_specs=(full((rows, d)), full((rows, ns)), full((rows, ns))),
        scratch_shapes=[pltpu.VMEM((rows, ns), F32) for _ in range(2)],
        compiler_params=_cparams(("arbitrary",), 40 * 2**20),
        name="s5_rows",
    )(s5_in, h0re, h0im, a_re, a_im, *ws)


def _headsum(x, ones_bd):
    hi = x.astype(BF16)
    lo = (x - hi.astype(F32)).astype(BF16)
    return _dot(hi, ones_bd) + _dot(lo, ones_bd)


def _pre_lora(cur, prev, p, width):
    mixed = cur + p["mu"] * (prev - cur)
    nl = p["w2p"].shape[0]
    wa = mixed[:, 3 * width:3 * width + nl]
    gd = mixed[:, 3 * width + nl:]
    return dict(
        r=mixed[:, 0:width], k=mixed[:, width:2 * width], v=mixed[:, 2 * width:3 * width],
        lw=_dot(jnp.tanh(wa).astype(BF16), p["w2p"]),
        la=_dot(wa.astype(BF16), p["a2p"]),
        g=_dot(jax.nn.sigmoid(gd).astype(BF16), p["g2"]))


def _pre_gates(t, p, ones_bd):
    w = -_softplus(-(p["w0"] + t["lw"])) - 0.5
    t["ld"] = -jnp.exp(w)
    t["a"] = jax.nn.sigmoid(p["a0"] + t["la"])
    t["kk"] = t["k"] * p["k_k"]
    t["ss"] = _headsum(t["kk"] * t["kk"], ones_bd)


def _pre_keys(t, p):
    kk = t["kk"] * jnp.minimum(lax.rsqrt(t["ss"]), 1e12)
    t["k2"] = t["k"] * (1.0 + (t["a"] - 1.0) * p["k_a"])
    t["av"] = -kk
    t["bv"] = kk * t["a"]


def _rwkv_pre(cur, prev, p, ones_bd, width):
    t = _pre_lora(cur, prev, p, width)
    _pre_gates(t, p, ones_bd)
    _pre_keys(t, p)
    return t["r"], t["ld"], t["k2"], t["v"], t["av"], t["bv"], t["g"]


def _post_mean(y, ones_bd, head):
    return y - _headsum(y, ones_bd) * (1.0 / head)


def _post_stats(dlt, t, p, ones_bd, head):
    return _headsum(dlt * dlt, ones_bd) * (1.0 / head), _headsum(t["r"] * t["k2"] * p["r_k"], ones_bd)


def _post_out(dlt, var, rk, t, p, proj):
    yn = dlt * lax.rsqrt(var + GN_EPS) * p["ln_w"] + p["ln_b"]
    o = ((yn + rk * t["v"]) * t["g"]).astype(BF16)
    return _dot(o, proj)


def _rwkv_post(y, r, k2, v, g, p, ones_bd, proj, head):
    t = dict(r=r, k2=k2, v=v, g=g)
    dlt = _post_mean(y, ones_bd, head)
    var, rk = _post_stats(dlt, t, p, ones_bd, head)
    return _post_out(dlt, var, rk, t, p, proj)


def _interleave(main, other):
    out = []
    done = 0
    for i, item in enumerate(main):
        out.append(item)
        upto = (len(other) * (i + 1)) // len(main)
        out.extend(other[done:upto])
        done = upto
    return out


def _rwkv_seq_block(cur_ref, carry_scr, p, ones_bd, tril, proj_ref, s_ref, y_ref, out_ref, head, chunk, n_groups):
    nseq, tr, _ = cur_ref.shape
    width = y_ref.shape[1]
    nh = width // head
    c = chunk
    per_seq = tr // c
    gsz = nseq // n_groups
    steps = int(math.log2(c))
    hs = lambda h: slice(h * head, (h + 1) * head)
    ri = lax.broadcasted_iota(jnp.int32, (c, c), 0)
    ci = lax.broadcasted_iota(jnp.int32, (c, c), 1)
    strict = ri > ci
    incl = ri >= ci
    row = lax.broadcasted_iota(jnp.int32, cur_ref.shape[1:], 0)
    tok = {}
    fac, ar, sc, pw, nm, m_rb, lv, mv, vk, s0, am, ub, dlt, stats = ({} for _ in range(14))

    def seqs(gi):
        return range(gi * gsz, (gi + 1) * gsz)

    def chunks(gi):
        return [(q * per_seq + t, q, slice(((q - gi * gsz) * per_seq + t) * c, ((q - gi * gsz) * per_seq + t + 1) * c))
                for q in seqs(gi) for t in range(per_seq)]

    def pairs(gi):
        return [(j, q, h) for j, q, _ in chunks(gi) for h in range(nh)]

    def grows(gi):
        return slice(gi * gsz * tr, (gi + 1) * gsz * tr)

    def pre(gi):
        curs, prevs = [], []
        for q in seqs(gi):
            cq = cur_ref[q]
            prevs.append(jnp.where(row == 0, carry_scr[q], pltpu.roll(cq, 1, axis=0)))
            carry_scr[q] = cq[tr - 1:tr, :]
            curs.append(cq)
        tok[gi] = _pre_lora(jnp.concatenate(curs, axis=0), jnp.concatenate(prevs, axis=0), p, width)

    def gates(gi):
        _pre_gates(tok[gi], p, ones_bd)

    def keys(gi):
        t = tok[gi]
        _pre_keys(t, p)
        for j, _, rs in chunks(gi):
            lds = t["ld"][rs]
            hi, mid, lo = _split3(lds)
            fac[j] = dict(lds=lds, cum=_dot(tril, hi) + _dot(tril, mid) + _dot(tril, lo))

    def decay(gi):
        t = tok[gi]
        for j, _, rs in chunks(gi):
            f = fac[j]
            cum = f["cum"]
            g_inv = jnp.exp(-cum)
            cum_end = cum[c - 1:c, :]
            g_rel = jnp.exp(cum_end - cum)
            f.update(a_t=t["av"][rs] * jnp.exp(cum - f["lds"]), r_t=t["r"][rs] * jnp.exp(cum),
                     b_t=t["bv"][rs] * g_inv, k_t=t["k2"][rs] * g_inv,
                     b_e=(t["bv"][rs] * g_rel).astype(BF16), k_e=(t["k2"][rs] * g_rel).astype(BF16),
                     g_end=jnp.exp(cum_end), v=t["v"][rs].astype(BF16))

    def scores(gi):
        for j, _, h in pairs(gi):
            f = fac[j]
            ar[j, h] = jnp.concatenate([f["a_t"][:, hs(h)], f["r_t"][:, hs(h)]], axis=0).astype(BF16)
            bk = jnp.concatenate([f["b_t"][:, hs(h)], f["k_t"][:, hs(h)]], axis=0).astype(BF16)
            sc[j, h] = _dot_nt(ar[j, h], bk)

    def lowers(gi):
        for j, _, h in pairs(gi):
            x = sc[j, h]
            vh = fac[j]["v"][:, hs(h)]
            pw[j, h] = jnp.where(strict, x[:c, :c], 0.0)
            nm[j, h] = pw[j, h]
            m_rb[j, h] = jnp.where(incl, x[c:, :c], 0.0).astype(BF16)
            lv[j, h] = _dot(jnp.where(strict, x[:c, c:], 0.0).astype(BF16), vh)
            mv[j, h] = _dot(jnp.where(incl, x[c:, c:], 0.0).astype(BF16), vh)
            vk[j, h] = _dot_tn(vh, fac[j]["k_e"][:, hs(h)])

    def square(gi):
        for j, _, h in pairs(gi):
            pb = pw[j, h].astype(BF16)
            pw[j, h] = _dot(pb, pb)

    def extend(gi):
        for j, _, h in pairs(gi):
            nm[j, h] = nm[j, h] + pw[j, h] + _dot(nm[j, h].astype(BF16), pw[j, h].astype(BF16))

    def chain(t, part):
        def project(gi):
            for j, q, h in sel(gi):
                s0[j, h] = s_ref[q, h]
                am[j, h] = _dot_nt(ar[j, h], s0[j, h].astype(BF16))

        def solve(gi):
            for j, q, h in sel(gi):
                w = am[j, h][:c] + lv[j, h]
                ub[j, h] = (w + _dot(nm[j, h].astype(BF16), w.astype(BF16))).astype(BF16)

        def update(gi):
            for j, q, h in sel(gi):
                f = fac[j]
                r0 = (q * per_seq + t) * c
                y_ref[r0:r0 + c, hs(h)] = am[j, h][c:] + _dot(m_rb[j, h], ub[j, h]) + mv[j, h]
                s_ref[q, h] = (s0[j, h] * f["g_end"][:, hs(h)] + _dot_tn(ub[j, h], f["b_e"][:, hs(h)])
                               + vk[j, h])

        def sel(gi):
            return [(j, q, h) for j, q, h in pairs(gi) if j == q * per_seq + t]

        return dict(project=project, solve=solve, update=update)[part]

    def post_mean(gi):
        dlt[gi] = _post_mean(y_ref[grows(gi), :], ones_bd, head)

    def post_stats(gi):
        stats[gi] = _post_stats(dlt[gi], tok[gi], p, ones_bd, head)

    def post_out(gi):
        out = _post_out(dlt[gi], *stats[gi], tok[gi], p, proj_ref[...])
        for i, q in enumerate(seqs(gi)):
            out_ref[q] = out[i * tr:(i + 1) * tr]

    front = [pre, gates, keys, decay, scores, lowers]
    middle = [st for _ in range(steps - 1) for st in (square, extend)]
    middle += [chain(t, part) for t in range(per_seq) for part in ("project", "solve", "update")]
    middle += [post_mean, post_stats, post_out]
    order = [(st, 0) for st in front]
    for gi in range(n_groups):
        nxt = [(st, gi + 1) for st in front] if gi + 1 < n_groups else []
        order += _interleave([(st, gi) for st in middle], nxt)
    for st, gi in order:
        st(gi)


_RWKV_VEC_NAMES = ("w0", "a0", "k_k", "k_a", "r_k", "ln_w", "ln_b")
_RWKV_CONSTS = ("mu", "vecs", "w2p", "a2p", "g2")


def _rwkv_params(mu_ref, vecs_ref, w2p_ref, a2p_ref, g2_ref):
    p = {n: vecs_ref[i:i + 1, :] for i, n in enumerate(_RWKV_VEC_NAMES)}
    p.update(mu=mu_ref[...], w2p=w2p_ref[...], a2p=a2p_ref[...], g2=g2_ref[...])
    return p


def _rwkv_seq_kernel(cur_ref, shift0_ref, s0_ref, *rest, chunk, head):
    np_ = len(_RWKV_CONSTS)
    p = _rwkv_params(*rest[:np_])
    proj_ref, ones_ref, tril_ref = rest[np_:np_ + 3]
    out_ref, st_ref = rest[np_ + 3:np_ + 5]
    carry_scr, y_scr = rest[np_ + 5:]

    @pl.when(pl.program_id(1) == 0)
    def _():
        carry_scr[...] = shift0_ref[...]
        st_ref[...] = s0_ref[...]

    _rwkv_seq_block(cur_ref, carry_scr, p, ones_ref[...], tril_ref[...], proj_ref, st_ref, y_scr, out_ref,
                    head, chunk, WKV_GROUPS)


def _rwkv_rows_pre_kernel(cur_ref, prev_ref, mu_ref, vecs_ref, w2p_ref, a2p_ref, g2_ref, ones_ref,
                          r_ref, k_ref, v_ref, g_ref, t_ref, *, width):
    p = _rwkv_params(mu_ref, vecs_ref, w2p_ref, a2p_ref, g2_ref)
    r, ld, k2, v, a, b, g = _rwkv_pre(cur_ref[...], prev_ref[...], p, ones_ref[...], width)
    r_ref[...] = r
    k_ref[...] = k2
    v_ref[...] = v
    g_ref[...] = g
    for i, val in enumerate((r, ld, k2, v, a, b)):
        t_ref[i] = val.T


def _wkv_step_kernel(t_ref, s_ref, y_ref, so_ref):
    r = t_ref[0]
    dec = jnp.exp(t_ref[1])
    k = t_ref[2]
    a = t_ref[4]
    b = t_ref[5]

    def body(i, carry):
        s = s_ref[i]
        sa = jnp.sum(s * a, axis=0, keepdims=True)
        s2 = s * dec + sa * b + t_ref[3, pl.ds(i, 1), :] * k
        so_ref[i] = s2
        y_ref[pl.ds(i, 1), :] = jnp.sum(s2 * r, axis=0, keepdims=True)
        return carry

    lax.fori_loop(0, s_ref.shape[0], body, 0, unroll=V7X_SUBLANES)


def _rwkv_rows_post_kernel(yt_ref, r_ref, k_ref, v_ref, g_ref, mu_ref, vecs_ref, w2p_ref, a2p_ref, g2_ref,
                           proj_ref, ones_ref, o_ref, *, head):
    p = _rwkv_params(mu_ref, vecs_ref, w2p_ref, a2p_ref, g2_ref)
    o_ref[...] = _rwkv_post(yt_ref[...].T, r_ref[...], k_ref[...], v_ref[...], g_ref[...], p,
                            ones_ref[...], proj_ref[...], head)


def _rwkv_param_list(rw):
    return [rw[n] for n in _RWKV_CONSTS]


def _rwkv_seq_call(cur3, shift0, s0, rw, *, tr, nseq):
    nb, length, n_shift = cur3.shape
    nh, head = s0.shape[1], s0.shape[2]
    width = nh * head
    d = rw["proj"].shape[1]
    params = _rwkv_param_list(rw)
    chunk = min(WKV_CHUNK, tr)
    tril = (jnp.arange(chunk)[:, None] >= jnp.arange(chunk)[None, :]).astype(BF16)
    consts = params + [rw["proj"], rw["ones_bd"], tril]
    kern = functools.partial(_rwkv_seq_kernel, chunk=chunk, head=head)
    return pl.pallas_call(
        kern,
        out_shape=(jax.ShapeDtypeStruct((nb, length, d), F32),
                   jax.ShapeDtypeStruct((nb, nh, head, head), F32)),
        grid=(nb // nseq, length // tr),
        in_specs=[pl.BlockSpec((nseq, tr, n_shift), lambda q, i: (q, i, 0)),
                  pl.BlockSpec((nseq, 1, n_shift), lambda q, i: (q, 0, 0)),
                  pl.BlockSpec((nseq, nh, head, head), lambda q, i: (q, 0, 0, 0))]
                 + [pl.BlockSpec(c.shape, lambda q, i, nd=c.ndim: (0,) * nd) for c in consts],
        out_specs=(pl.BlockSpec((nseq, tr, d), lambda q, i: (q, i, 0)),
                   pl.BlockSpec((nseq, nh, head, head), lambda q, i: (q, 0, 0, 0))),
        scratch_shapes=[pltpu.VMEM((nseq, 1, n_shift), F32), pltpu.VMEM((nseq * tr, width), F32)],
        compiler_params=_cparams(("arbitrary", "arbitrary"), 40 * 2**20),
        name="rwkv_seq",
    )(cur3, shift0.reshape(nb, 1, n_shift), s0, *consts)


def _rwkv_rows_call(cur, prev, s0_t, rw):
    rows, n_shift = cur.shape
    nh, head = s0_t.shape[0], s0_t.shape[1]
    width = nh * head
    d = rw["proj"].shape[1]
    params = _rwkv_param_list(rw)
    full = lambda shape: pl.BlockSpec(shape, lambda i: (0,) * len(shape))
    vec = jax.ShapeDtypeStruct((rows, width), F32)
    n_t = 6

    pre_consts = params + [rw["ones_bd"]]
    r, k2, v, g, t = pl.pallas_call(
        functools.partial(_rwkv_rows_pre_kernel, width=width),
        out_shape=(vec,) * 4 + (jax.ShapeDtypeStruct((n_t, width, rows), F32),),
        grid=(1,),
        in_specs=[full(cur.shape), full(prev.shape)] + [full(c.shape) for c in pre_consts],
        out_specs=(full((rows, width)),) * 4 + (full((n_t, width, rows)),),
        compiler_params=_cparams(("arbitrary",), 32 * 2**20),
        name="rwkv_rows_pre",
    )(cur, prev, *pre_consts)

    sblk = pl.BlockSpec((None, head, head, rows), lambda h: (h, 0, 0, 0))
    yt, s1_t = pl.pallas_call(
        _wkv_step_kernel,
        out_shape=(jax.ShapeDtypeStruct((width, rows), F32), jax.ShapeDtypeStruct(s0_t.shape, F32)),
        grid=(nh,),
        in_specs=[pl.BlockSpec((n_t, head, rows), lambda h: (0, h, 0)), sblk],
        out_specs=(pl.BlockSpec((head, rows), lambda h: (h, 0)), sblk),
        compiler_params=_cparams(("parallel",), 32 * 2**20),
        name="wkv_step",
    )(t, s0_t)

    post_consts = params + [rw["proj"], rw["ones_bd"]]
    out = pl.pallas_call(
        functools.partial(_rwkv_rows_post_kernel, head=head),
        out_shape=jax.ShapeDtypeStruct((rows, d), F32),
        grid=(1,),
        in_specs=[full((width, rows))] + [full((rows, width))] * 4 + [full(c.shape) for c in post_consts],
        out_specs=full((rows, d)),
        compiler_params=_cparams(("arbitrary",), 32 * 2**20),
        name="rwkv_rows_post",
    )(yt, r, k2, v, g, *post_consts)
    return out, s1_t


def _merge_kernel(h_ref, gates_ref, ys5_ref, yrw_ref, gt2_ref, wout_ref,
                  sh_ref, sc_ref, gt_ref, g_ref, w1_ref, w3_ref, w2_ref, gfin_ref, o_ref, *, n_chunks,
                  tiles_per_seq):
    row = functools.partial(_mod_row, tiles_per_seq=tiles_per_seq)
    d = h_ref.shape[1]
    gates = gates_ref[...]
    m = jax.nn.sigmoid(gates[:, :d]) * ys5_ref[...] + jax.nn.sigmoid(gates[:, d:]) * yrw_ref[...]
    h2 = h_ref[...] + row(gt2_ref) * _dot(m.astype(BF16), wout_ref[...])
    h3 = _ffn_core(h2, row(sh_ref), row(sc_ref), row(gt_ref), g_ref[...], w1_ref, w3_ref, w2_ref, n_chunks)
    o_ref[...] = _rms(h3, gfin_ref[...])


def _merge_call(h, gates, ys5, yrw, mods, ks, w_out, g, w1, w3, w2, g_fin, *, tm):
    rows, d = h.shape
    d_ff = w1.shape[1]
    kern = functools.partial(_merge_kernel, n_chunks=_ffn_chunks(d_ff), tiles_per_seq=mods.tiles_per_seq)
    row_spec = lambda w: pl.BlockSpec((tm, w), lambda i: (i, 0))
    weights = (3 * d * d_ff + d * d) * 2
    tiles = 2 * tm * 6 * d * 4 + 4 * tm * d_ff * 4 + 6 * tm * d * 4
    ys5_spec = row_spec(d) if mods.tiles_per_seq is None else _time_major(rows, d, tm, mods.tiles_per_seq)[1]
    return pl.pallas_call(
        kern,
        out_shape=jax.ShapeDtypeStruct((rows, d), F32),
        grid=(rows // tm,),
        in_specs=[row_spec(d), row_spec(2 * d), ys5_spec, row_spec(d),
                  mods.spec(ks[0]), _const_spec(w_out.shape),
                  mods.spec(ks[1]), mods.spec(ks[2]), mods.spec(ks[3]),
                  _const_spec((1, d)), _const_spec(w1.shape), _const_spec(w3.shape), _const_spec(w2.shape),
                  _const_spec((1, d))],
        out_specs=row_spec(d),
        compiler_params=_cparams(("parallel",), weights + tiles + 8 * 2**20),
        name="merge_ffn2",
    )(h, gates, ys5, yrw, mods.mod, w_out, mods.mod, mods.mod, mods.mod, g.reshape(1, d), w1, w3, w2,
      g_fin.reshape(1, d))


_LATE_WEIGHTS = ("ffn2_w1", "ffn2_w3", "ffn2_w2", "w_in", "w_out", "glu_v", "glu_g", "proj", "g2", "w2p", "a2p")


def _pick_tile(n, target):
    t = min(n, target)
    while n % t:
        t -= V7X_SUBLANES
    return t


def _layer(x, mod, mod_row0, s5_re0, s5_im0, wkv0, shift0, w, *, sequential):
    nb, length, d = x.shape
    rows = nb * length
    g, n = s5_re0.shape[1], s5_re0.shape[2]
    ns = g * n
    n_shift = shift0.shape[1]
    s5_width = w["s5"]["d"].shape[1]
    widths = (s5_width, n_shift, 2 * d)

    tm = _pick_tile(length if sequential else rows, 512)
    tm2 = _pick_tile(tm, 512)
    mods = _ModRows(mod, mod_row0, nb, tm, length // tm if sequential else None)
    mods2 = _ModRows(mod, mod_row0, nb, tm2, length // tm2 if sequential else None)

    x2 = x.reshape(rows, d)
    late = w["late"]
    to_cast = [n for n in _LATE_WEIGHTS if late[n].dtype != BF16]
    h1, cast = _ffn_call(x2, mods, (0, 1, 2), w["g_ffn1"], *w["ffn1"], tm=tm, casts=[late[n] for n in to_cast])
    late = dict(late, **dict(zip(to_cast, cast)))
    w = dict(w, late=late, w_in=late["w_in"], w_out=late["w_out"],
             ffn2=(late["ffn2_w1"], late["ffn2_w3"], late["ffn2_w2"]),
             s5=dict(w["s5"], glu_v=late["glu_v"], glu_g=late["glu_g"]),
             rwkv=dict(w["rwkv"], proj=late["proj"], g2=late["g2"], w2p=late["w2p"], a2p=late["a2p"]))
    s5_in, cur, gates = _proj_call(h1, mods, (3, 4), w["g_mix"], w["w_in"], widths, tm=tm)

    h0re = s5_re0.reshape(nb, ns)
    h0im = s5_im0.reshape(nb, ns)
    if sequential:
        ts = _pick_tile(length, S5_STEP_ROWS // nb)
        y_s5, hre, him = _s5_seq_call(s5_in, h0re, h0im, w["a_re"], w["a_im"], w["s5"], ts=ts)
        nseq = WKV_SEQS if nb % WKV_SEQS == 0 else 1
        tr = _pick_tile(length, WKV_BLOCK_CHUNKS * WKV_CHUNK // nseq)
        y_rw, wkv1 = _rwkv_seq_call(cur.reshape(nb, length, n_shift), shift0, wkv0, w["rwkv"], tr=tr, nseq=nseq)
        y_rw = y_rw.reshape(rows, d)
    else:
        y_s5, hre, him = _s5_rows_call(s5_in, h0re, h0im, w["a_re"], w["a_im"], w["s5"])
        y_rw, wkv1 = _rwkv_rows_call(cur, shift0, wkv0, w["rwkv"])

    y = _merge_call(h1, gates, y_s5, y_rw, mods2, (5, 6, 7, 8), w["w_out"],
                    w["g_ffn2"], *w["ffn2"], w["g_final"], tm=tm2)

    shift1 = cur.reshape(nb, length, n_shift)[:, -1]
    return (y.reshape(nb, length, d), hre.reshape(nb, g, n), him.reshape(nb, g, n), wkv1, shift1), w


def kernel(x_prompt, x_sample, c_prompt, c_sample, state_s5_re, state_s5_im, state_wkv, state_shift, w_ada, b_ada, g_ffn1, g_mix, g_ffn2, g_final, ffn1_w1, ffn1_w3, ffn1_w2, ffn2_w1, ffn2_w3, ffn2_w2, w_in, mu_shift, s5_lam_re, s5_lam_im, s5_log_dt, s5_b_re, s5_b_im, s5_c_re, s5_c_im, s5_d, s5_glu_v, s5_glu_g, rwkv_w0, rwkv_w2, rwkv_a0, rwkv_a2, rwkv_g2, rwkv_k_k, rwkv_k_a, rwkv_r_k, rwkv_ln_w, rwkv_ln_b, rwkv_proj, w_out):
    bp, _, d = x_prompt.shape
    bs = x_sample.shape[0]
    g, n = s5_lam_re.shape
    nh, head = rwkv_r_k.shape
    width = nh * head
    n_shift = mu_shift.shape[0]
    assert bs % bp == 0, "sample rows come first in the modulation array; prompt rows must stay block-aligned"
    bf = lambda t: t.astype(BF16)

    a_re, a_im, wb, wc_re, wc_im = _s5_prep_call(
        s5_lam_re, s5_lam_im, s5_log_dt, jnp.swapaxes(s5_b_re, 1, 2), jnp.swapaxes(s5_b_im, 1, 2),
        s5_c_re, s5_c_im, bp)
    w_lora = rwkv_w2.shape[0]
    a_lora = rwkv_a2.shape[0]
    w2p = jnp.concatenate([rwkv_w2, jnp.zeros((a_lora, width), F32)], axis=0)
    a2p = jnp.concatenate([jnp.zeros((w_lora, width), F32), rwkv_a2], axis=0)
    vec_rows = [rwkv_w0, rwkv_a0, rwkv_k_k, rwkv_k_a, rwkv_r_k.reshape(width), rwkv_ln_w, rwkv_ln_b]
    vecs = jnp.stack(vec_rows + [jnp.zeros_like(rwkv_w0)] * (V7X_SUBLANES - len(vec_rows)))
    rw = dict(mu=mu_shift.reshape(1, n_shift), vecs=vecs,
              ones_bd=jnp.kron(jnp.eye(nh, dtype=F32), jnp.ones((head, head), F32)).astype(BF16))
    late = dict(ffn2_w1=ffn2_w1, ffn2_w3=ffn2_w3, ffn2_w2=ffn2_w2, w_in=w_in, w_out=w_out,
                glu_v=s5_glu_v, glu_g=s5_glu_g, proj=rwkv_proj, g2=rwkv_g2, w2p=w2p, a2p=a2p)
    w = dict(
        g_ffn1=g_ffn1, g_mix=g_mix, g_ffn2=g_ffn2, g_final=g_final,
        ffn1=(bf(ffn1_w1), bf(ffn1_w3), bf(ffn1_w2)),
        a_re=a_re, a_im=a_im,
        s5=dict(wb=wb, wc_re=wc_re, wc_im=wc_im, d=s5_d.reshape(1, -1)),
        rwkv=rw, late=late,
    )

    mod = _mod_call(jnp.concatenate([c_sample, c_prompt], axis=0), w_ada, b_ada)

    z_s5 = jnp.zeros((bp, g, n), state_s5_re.dtype)
    z_wkv = jnp.zeros((bp, nh, head, head), state_wkv.dtype)
    z_shift = jnp.zeros((bp, n_shift), state_shift.dtype)
    (y_p, s5re_p, s5im_p, wkv_p, shift_p), w = _layer(x_prompt, mod, bs, z_s5, z_s5, z_wkv, z_shift, w,
                                                      sequential=True)
    (y_s, s5re_s, s5im_s, wkv_s_t, shift_s), _ = _layer(x_sample, mod, 0, state_s5_re, state_s5_im,
                                                        jnp.transpose(state_wkv, (1, 2, 3, 0)), state_shift, w,
                                                        sequential=False)
    wkv_s = jnp.transpose(wkv_s_t, (3, 0, 1, 2))
    return (y_p, y_s, s5re_p, s5im_p, wkv_p, shift_p, s5re_s, s5im_s, wkv_s, shift_s)
```

```python
import functools
import math

import jax
import jax.numpy as jnp
from jax import lax
from jax.experimental import pallas as pl
from jax.experimental.pallas import tpu as pltpu

F32 = jnp.float32
BF16 = jnp.bfloat16

NORM_EPS = 1e-6
GN_EPS = 64e-5
N_MOD = 9

V7X_LANES = 128
V7X_SUBLANES = 8
V7X_MXU_DIM = 256
V7X_VMEM_BYTES = 64 * 1024 * 1024
VMEM_CAP_BYTES = V7X_VMEM_BYTES - 8 * 1024 * 1024

ROW_PARTS = 4
S5_STEP_ROWS = 1024
S5_OUT_PARTS = 1
WKV_CHUNK = 64
WKV_BLOCK_CHUNKS = 8
WKV_SEQS = 8
WKV_GROUPS = 1


def _cparams(semantics, vmem_bytes):
    return pltpu.CompilerParams(
        dimension_semantics=semantics,
        vmem_limit_bytes=int(min(vmem_bytes, VMEM_CAP_BYTES)),
    )


def _const_spec(shape):
    nd = len(shape)
    return pl.BlockSpec(shape, lambda *_: (0,) * nd, pipeline_mode=pl.Buffered(1))


def _dot(a, b):
    return jnp.dot(a, b, preferred_element_type=F32)


def _dot_nt(a, b):
    return lax.dot_general(a, b, (((1,), (1,)), ((), ())), preferred_element_type=F32)


def _dot_tn(a, b):
    return lax.dot_general(a, b, (((0,), (0,)), ((), ())), preferred_element_type=F32)


def _split3(x):
    hi = x.astype(BF16)
    r1 = x - hi.astype(F32)
    mid = r1.astype(BF16)
    lo = (r1 - mid.astype(F32)).astype(BF16)
    return hi, mid, lo


def _rms(x, g):
    ms = jnp.mean(x * x, axis=-1, keepdims=True)
    return x * lax.rsqrt(ms + NORM_EPS) * g


def _sigmoid(x):
    return 0.5 * (jnp.tanh(0.5 * x) + 1.0)


def _softplus(x):
    return jnp.maximum(x, 0.0) + jnp.log(1.0 + jnp.exp(-jnp.abs(x)))


def _mod_kernel(c_ref, w_ref, b_ref, o_ref):
    c = c_ref[...]
    s = (c * _sigmoid(c)).astype(BF16)
    o_ref[...] = _dot(s, w_ref[...].astype(BF16)) + b_ref[...]


def _mod_call(c, w_ada, b_ada):
    rows, d = c.shape
    n = w_ada.shape[1]
    return pl.pallas_call(
        _mod_kernel,
        out_shape=jax.ShapeDtypeStruct((n // d, rows, d), F32),
        grid=(n // d,),
        in_specs=[
            pl.BlockSpec((rows, d), lambda j: (0, 0)),
            pl.BlockSpec((d, d), lambda j: (0, j)),
            pl.BlockSpec((1, d), lambda j: (0, j)),
        ],
        out_specs=pl.BlockSpec((None, rows, d), lambda j: (j, 0, 0)),
        compiler_params=_cparams(("parallel",), 32 * 2**20),
        name="adaln_mod",
    )(c, w_ada, b_ada.reshape(1, n))


class _ModRows:
    def __init__(self, mod, row0, nrows, tm, tiles_per_seq):
        self.mod, self.row0, self.nrows, self.tm, self.tiles_per_seq = mod, row0, nrows, tm, tiles_per_seq

    def spec(self, k):
        d = self.mod.shape[2]
        if self.tiles_per_seq is None:
            blk0 = self.row0 // self.tm
            return pl.BlockSpec((None, self.tm, d), lambda i: (k, blk0 + i, 0))
        blk0 = self.row0 // self.nrows
        return pl.BlockSpec((None, self.nrows, d), lambda i: (k, blk0, 0))


def _mod_row(ref, tiles_per_seq):
    if tiles_per_seq is None:
        return ref[...]
    return ref[pl.ds(pl.program_id(0) // tiles_per_seq, 1), :]


def _ffn_core(x, sh, sc, gt, g, w1_ref, w3_ref, w2_ref, n_chunks):
    bounds = _ffn_bounds(w1_ref.shape[1], n_chunks)
    u = (_rms(x, g) * (1.0 + sc) + sh).astype(BF16)
    acc = None
    for lo, hi in zip(bounds[:-1], bounds[1:]):
        a = _dot(u, w1_ref[:, lo:hi])
        b = _dot(u, w3_ref[:, lo:hi])
        hm = (a * _sigmoid(a) * b).astype(BF16)
        part = _dot(hm, w2_ref[lo:hi, :])
        acc = part if acc is None else acc + part
    return x + (0.5 * gt) * acc


def _row_parts(rows, n_parts):
    if rows % (n_parts * 2 * V7X_SUBLANES):
        return [slice(0, rows)]
    step = rows // n_parts
    return [slice(r * step, (r + 1) * step) for r in range(n_parts)]


def _mod_part(mod_rows, part):
    return mod_rows if mod_rows.shape[0] == 1 else mod_rows[part]


def _ffn_bounds(d_ff, n_chunks):
    tiles = -(-d_ff // V7X_MXU_DIM)
    return [min(d_ff, ((tiles * c + n_chunks - 1) // n_chunks) * V7X_MXU_DIM) for c in range(n_chunks + 1)]


def _ffn_kernel(x_ref, sh_ref, sc_ref, gt_ref, g_ref, w1_ref, w3_ref, w2_ref, *rest, n_chunks, tiles_per_seq):
    n_cast = len(rest) // 2
    o_ref = rest[n_cast]
    row = functools.partial(_mod_row, tiles_per_seq=tiles_per_seq)
    sh, sc, gt = row(sh_ref), row(sc_ref), row(gt_ref)
    for part in _row_parts(x_ref.shape[0], ROW_PARTS):
        o_ref[part, :] = _ffn_core(x_ref[part, :], _mod_part(sh, part), _mod_part(sc, part), _mod_part(gt, part),
                                   g_ref[...], w1_ref, w3_ref, w2_ref, n_chunks)
    for src, dst in zip(rest[:n_cast], rest[n_cast + 1:]):
        dst[...] = src[...].astype(BF16)


def _cast_blocks(arr, n_steps):
    bf16_rows = 2 * V7X_SUBLANES
    nblk = n_steps
    while arr.shape[0] % nblk or (arr.shape[0] // nblk) % bf16_rows:
        nblk //= 2
    per = n_steps // nblk
    return pl.BlockSpec((arr.shape[0] // nblk, arr.shape[1]), lambda i: (i // per, 0))


def _ffn_chunks(d_ff):
    return 2 if d_ff % (2 * V7X_LANES) == 0 else 1


def _ffn_call(x2, mods, ks, g, w1, w3, w2, *, tm, casts=()):
    rows, d = x2.shape
    d_ff = w1.shape[1]
    n_steps = rows // tm
    kern = functools.partial(_ffn_kernel, n_chunks=_ffn_chunks(d_ff), tiles_per_seq=mods.tiles_per_seq)
    weights = 3 * d * d_ff * 2
    tiles = 4 * tm * d * 4 + 4 * tm * d_ff * 4
    cast_specs = [_cast_blocks(a, n_steps) for a in casts]
    cast_bytes = sum(2 * 6 * s.block_shape[0] * s.block_shape[1] for s in cast_specs)
    out = pl.pallas_call(
        kern,
        out_shape=(jax.ShapeDtypeStruct((rows, d), F32),) + tuple(jax.ShapeDtypeStruct(a.shape, BF16) for a in casts),
        grid=(n_steps,),
        in_specs=[
            pl.BlockSpec((tm, d), lambda i: (i, 0)),
            mods.spec(ks[0]), mods.spec(ks[1]), mods.spec(ks[2]),
            _const_spec((1, d)),
            _const_spec(w1.shape), _const_spec(w3.shape), _const_spec(w2.shape),
        ] + cast_specs,
        out_specs=(pl.BlockSpec((tm, d), lambda i: (i, 0)),) + tuple(cast_specs),
        compiler_params=_cparams(("arbitrary",), weights + tiles + cast_bytes + 8 * 2**20),
        name="ffn1",
    )(x2, mods.mod, mods.mod, mods.mod, g.reshape(1, d), w1, w3, w2, *casts)
    return out[0], list(out[1:])


def _proj_kernel(h_ref, sh_ref, sc_ref, g_ref, win_ref, s5_ref, cur_ref, gates_ref, *, tiles_per_seq):
    row = functools.partial(_mod_row, tiles_per_seq=tiles_per_seq)
    u = (_rms(h_ref[...], g_ref[...]) * (1.0 + row(sc_ref)) + row(sh_ref)).astype(BF16)
    n0 = s5_ref.shape[1]
    n1 = n0 + cur_ref.shape[1]
    s5_ref[...] = _dot(u, win_ref[:, :n0])
    cur_ref[...] = _dot(u, win_ref[:, n0:n1])
    gates_ref[...] = _dot(u, win_ref[:, n1:])


def _time_major(rows, width, tm, tiles_per_seq):
    length = tiles_per_seq * tm
    spec = pl.BlockSpec((tm, width), lambda i: (i % tiles_per_seq, i // tiles_per_seq))
    return (length, rows // length * width), spec


def _proj_call(h, mods, ks, g, w_in, widths, *, tm):
    rows, d = h.shape
    n_in = w_in.shape[1]
    shapes = [(rows, w) for w in widths]
    specs = [pl.BlockSpec((tm, w), lambda i: (i, 0)) for w in widths]
    if mods.tiles_per_seq is not None:
        shapes[0], specs[0] = _time_major(rows, widths[0], tm, mods.tiles_per_seq)
    return pl.pallas_call(
        functools.partial(_proj_kernel, tiles_per_seq=mods.tiles_per_seq),
        out_shape=tuple(jax.ShapeDtypeStruct(s, F32) for s in shapes),
        grid=(rows // tm,),
        in_specs=[
            pl.BlockSpec((tm, d), lambda i: (i, 0)),
            mods.spec(ks[0]), mods.spec(ks[1]),
            _const_spec((1, d)),
            _const_spec(w_in.shape),
        ],
        out_specs=tuple(specs),
        compiler_params=_cparams(("parallel",), d * n_in * 2 + 3 * tm * (d + n_in) * 4 + 8 * 2**20),
        name="mix_proj",
    )(h, mods.mod, mods.mod, g.reshape(1, d), w_in)


def _s5_prep_kernel(lre_ref, lim_ref, ldt_ref, bre_ref, bim_ref, cre_ref, cim_ref,
                    are_ref, aim_ref, wb_ref, wcre_ref, wcim_ref, *, gpb):
    lre = lre_ref[...]
    lim = lim_ref[...]
    dt = jnp.exp(ldt_ref[...])
    mag = jnp.exp(lre * dt)
    ang = lim * dt
    a_re = mag * jnp.cos(ang)
    a_im = mag * jnp.sin(ang)
    den = lre * lre + lim * lim
    num_re = a_re - 1.0
    k_re = (num_re * lre + a_im * lim) / den
    k_im = (a_im * lre - num_re * lim) / den
    b_re = bre_ref[...]
    b_im = bim_ref[...]
    bb_re = (k_re * b_re - k_im * b_im).astype(BF16)
    bb_im = (k_re * b_im + k_im * b_re).astype(BF16)
    c_re = cre_ref[...].astype(BF16)
    c_im = (-cim_ref[...]).astype(BF16)
    g, c, n = b_re.shape
    nb = are_ref.shape[0]
    s = gpb * n
    wb_ref[...] = jnp.zeros(wb_ref.shape, BF16)
    wcre_ref[...] = jnp.zeros(wcre_ref.shape, BF16)
    wcim_ref[...] = jnp.zeros(wcim_ref.shape, BF16)
    for gi in range(g):
        k, gl = divmod(gi, gpb)
        rows = slice(gl * c, (gl + 1) * c)
        cols = slice(gl * n, (gl + 1) * n)
        are_ref[:, gi * n:(gi + 1) * n] = jnp.broadcast_to(a_re[gi], (nb, n))
        aim_ref[:, gi * n:(gi + 1) * n] = jnp.broadcast_to(a_im[gi], (nb, n))
        wb_ref[k, rows, cols] = bb_re[gi]
        wb_ref[k, rows, s + gl * n:s + (gl + 1) * n] = bb_im[gi]
        wcre_ref[k, rows, cols] = c_re[gi]
        wcim_ref[k, rows, cols] = c_im[gi]


def _s5_prep_call(lam_re, lam_im, log_dt, b_re_t, b_im_t, c_re, c_im, nb):
    g, n = lam_re.shape
    c = b_re_t.shape[1]
    gpb = max(1, min(g, V7X_MXU_DIM // c))
    nk = g // gpb
    return pl.pallas_call(
        functools.partial(_s5_prep_kernel, gpb=gpb),
        out_shape=(jax.ShapeDtypeStruct((nb, g * n), F32), jax.ShapeDtypeStruct((nb, g * n), F32),
                   jax.ShapeDtypeStruct((nk, gpb * c, 2 * gpb * n), BF16),
                   jax.ShapeDtypeStruct((nk, gpb * c, gpb * n), BF16),
                   jax.ShapeDtypeStruct((nk, gpb * c, gpb * n), BF16)),
        name="s5_discretise",
    )(lam_re.reshape(g, 1, n), lam_im.reshape(g, 1, n), log_dt.reshape(g, 1, 1), b_re_t, b_im_t, c_re, c_im)


def _s5_input_matmul(ub, wb_ref, bure_ref, buim_ref):
    nk, kin, two_s = wb_ref.shape
    s = two_s // 2
    for k in range(nk):
        res = _dot(ub[:, k * kin:(k + 1) * kin], wb_ref[k])
        bure_ref[:, k * s:(k + 1) * s] = res[:, :s]
        buim_ref[:, k * s:(k + 1) * s] = res[:, s:]


def _s5_output(xre, xim, u, wcre_ref, wcim_ref, d_row):
    nk, kout, s = wcre_ref.shape
    xre_b = xre.astype(BF16)
    xim_b = xim.astype(BF16)
    ys = []
    for k in range(nk):
        ys.append(_dot_nt(xre_b[:, k * s:(k + 1) * s], wcre_ref[k])
                  + _dot_nt(xim_b[:, k * s:(k + 1) * s], wcim_ref[k]))
    y = jnp.concatenate(ys, axis=-1) + d_row * u
    return jax.nn.gelu(y)


def _s5_glu(zb, gv_ref, gg_ref):
    return _dot(zb, gv_ref[...]) * _sigmoid(_dot(zb, gg_ref[...]))


def _s5_seq_kernel(u_ref, h0re_ref, h0im_ref, are_ref, aim_ref, wb_ref, wcre_ref, wcim_ref, d_ref,
                   gv_ref, gg_ref,
                   o_ref, hre_ref, him_ref,
                   bure_scr, buim_scr, xre_scr, xim_scr, uslab_scr, oslab_scr, *, lane_chunk, n_parts):
    nb = hre_ref.shape[0]
    ts = u_ref.shape[0]
    din = u_ref.shape[1] // nb
    dout = o_ref.shape[1] // nb

    @pl.when(pl.program_id(0) == 0)
    def _():
        hre_ref[...] = h0re_ref[...]
        him_ref[...] = h0im_ref[...]

    for b in range(nb):
        for j in range(din // V7X_LANES):
            col = b * din + j * V7X_LANES
            uslab_scr[j, pl.ds(b, ts, stride=nb), :] = u_ref[:, col:col + V7X_LANES]
    u_tm = jnp.concatenate([uslab_scr[j] for j in range(din // V7X_LANES)], axis=1)
    _s5_input_matmul(u_tm.astype(BF16), wb_ref, bure_scr, buim_scr)

    ns = are_ref.shape[1]
    lanes = [slice(lc * lane_chunk, (lc + 1) * lane_chunk) for lc in range(ns // lane_chunk)]
    state = [(hre_ref[:, sl], him_ref[:, sl]) for sl in lanes]
    tp = ts // n_parts
    for part in range(n_parts):
        for lc, sl in enumerate(lanes):
            ar = are_ref[:, sl]
            ai = aim_ref[:, sl]
            sr, si = state[lc]
            for t in range(part * tp, (part + 1) * tp):
                rs = slice(t * nb, (t + 1) * nb)
                nr = ar * sr - ai * si + bure_scr[rs, sl]
                ni = ar * si + ai * sr + buim_scr[rs, sl]
                xre_scr[rs, sl] = nr
                xim_scr[rs, sl] = ni
                sr, si = nr, ni
            state[lc] = (sr, si)
        pr = slice(part * tp * nb, (part + 1) * tp * nb)
        z_tm = _s5_output(xre_scr[pr, :], xim_scr[pr, :], u_tm[pr], wcre_ref, wcim_ref, d_ref[...])
        out = _s5_glu(z_tm.astype(BF16), gv_ref, gg_ref)
        for j in range(dout // V7X_LANES):
            oslab_scr[j, pr, :] = out[:, j * V7X_LANES:(j + 1) * V7X_LANES]
    for lc, sl in enumerate(lanes):
        hre_ref[:, sl], him_ref[:, sl] = state[lc]
    for b in range(nb):
        for j in range(dout // V7X_LANES):
            col = b * dout + j * V7X_LANES
            o_ref[:, col:col + V7X_LANES] = oslab_scr[j, pl.ds(b, ts, stride=nb), :]


def _s5_rows_kernel(u_ref, h0re_ref, h0im_ref, are_ref, aim_ref, wb_ref, wcre_ref, wcim_ref, d_ref,
                    gv_ref, gg_ref, o_ref, hre_ref, him_ref, bure_scr, buim_scr):
    u = u_ref[...]
    _s5_input_matmul(u.astype(BF16), wb_ref, bure_scr, buim_scr)
    ar = are_ref[0:1, :]
    ai = aim_ref[0:1, :]
    sr = h0re_ref[...]
    si = h0im_ref[...]
    nr = ar * sr - ai * si + bure_scr[...]
    ni = ar * si + ai * sr + buim_scr[...]
    hre_ref[...] = nr
    him_ref[...] = ni
    z = _s5_output(nr, ni, u, wcre_ref, wcim_ref, d_ref[...])
    o_ref[...] = _s5_glu(z.astype(BF16), gv_ref, gg_ref)


def _s5_weight_specs(ws):
    return [_const_spec(w.shape) for w in ws]


def _s5_seq_call(s5_in, h0re, h0im, a_re, a_im, sw, *, ts):
    nb, ns = a_re.shape
    length = s5_in.shape[0]
    din = s5_in.shape[1] // nb
    d = sw["glu_v"].shape[1]
    rows = nb * ts
    ws = [sw["wb"], sw["wc_re"], sw["wc_im"], sw["d"], sw["glu_v"], sw["glu_g"]]
    kern = functools.partial(_s5_seq_kernel, lane_chunk=4 * V7X_LANES, n_parts=S5_OUT_PARTS)
    return pl.pallas_call(
        kern,
        out_shape=(jax.ShapeDtypeStruct((length, nb * d), F32),
                   jax.ShapeDtypeStruct((nb, ns), F32), jax.ShapeDtypeStruct((nb, ns), F32)),
        grid=(length // ts,),
        in_specs=[pl.BlockSpec((ts, nb * din), lambda c: (c, 0)),
                  _const_spec((nb, ns)), _const_spec((nb, ns)), _const_spec((nb, ns)), _const_spec((nb, ns))]
                 + _s5_weight_specs(ws),
        out_specs=(pl.BlockSpec((ts, nb * d), lambda c: (c, 0)),
                   pl.BlockSpec((nb, ns), lambda c: (0, 0)), pl.BlockSpec((nb, ns), lambda c: (0, 0))),
        scratch_shapes=[pltpu.VMEM((rows, ns), F32) for _ in range(4)]
                       + [pltpu.VMEM((din // V7X_LANES, rows, V7X_LANES), F32),
                          pltpu.VMEM((d // V7X_LANES, rows, V7X_LANES), F32)],
        compiler_params=_cparams(("arbitrary",), 48 * 2**20),
        name="s5_seq",
    )(s5_in, h0re, h0im, a_re, a_im, *ws)


def _s5_rows_call(s5_in, h0re, h0im, a_re, a_im, sw):
    rows, din = s5_in.shape
    ns = a_re.shape[1]
    d = sw["glu_v"].shape[1]
    ws = [sw["wb"], sw["wc_re"], sw["wc_im"], sw["d"], sw["glu_v"], sw["glu_g"]]
    full = lambda shape: pl.BlockSpec(shape, lambda i: (0,) * len(shape))
    return pl.pallas_call(
        _s5_rows_kernel,
        out_shape=(jax.ShapeDtypeStruct((rows, d), F32),
                   jax.ShapeDtypeStruct((rows, ns), F32), jax.ShapeDtypeStruct((rows, ns), F32)),
        grid=(1,),
        in_specs=[full((rows, din)), full((rows, ns)), full((rows, ns)), full(a_re.shape), full(a_im.shape)]
                 + [full(w.shape) for w in ws],
        out_specs=(full((rows, d)), full((rows, ns)), full((rows, ns))),
        scratch_shapes=[pltpu.VMEM((rows, ns), F32) for _ in range(2)],
        compiler_params=_cparams(("arbitrary",), 40 * 2**20),
        name="s5_rows",
    )(s5_in, h0re, h0im, a_re, a_im, *ws)


def _headsum(x, ones_bd):
    hi = x.astype(BF16)
    lo = (x - hi.astype(F32)).astype(BF16)
    return _dot(hi, ones_bd) + _dot(lo, ones_bd)


def _pre_lora(cur, prev, p, width):
    mixed = cur + p["mu"] * (prev - cur)
    nl = p["w2p"].shape[0]
    wa = mixed[:, 3 * width:3 * width + nl]
    gd = mixed[:, 3 * width + nl:]
    return dict(
        r=mixed[:, 0:width], k=mixed[:, width:2 * width], v=mixed[:, 2 * width:3 * width],
        lw=_dot(jnp.tanh(wa).astype(BF16), p["w2p"]),
        la=_dot(wa.astype(BF16), p["a2p"]),
        g=_dot(_sigmoid(gd).astype(BF16), p["g2"]))


def _pre_gates(t, p, ones_bd):
    w = -_softplus(-(p["w0"] + t["lw"])) - 0.5
    t["ld"] = -jnp.exp(w)
    t["a"] = _sigmoid(p["a0"] + t["la"])
    t["kk"] = t["k"] * p["k_k"]
    t["ss"] = _headsum(t["kk"] * t["kk"], ones_bd)


def _pre_keys(t, p):
    kk = t["kk"] * jnp.minimum(lax.rsqrt(t["ss"]), 1e12)
    t["k2"] = t["k"] * (1.0 + (t["a"] - 1.0) * p["k_a"])
    t["av"] = -kk
    t["bv"] = kk * t["a"]


def _rwkv_pre(cur, prev, p, ones_bd, width):
    t = _pre_lora(cur, prev, p, width)
    _pre_gates(t, p, ones_bd)
    _pre_keys(t, p)
    return t["r"], t["ld"], t["k2"], t["v"], t["av"], t["bv"], t["g"]


def _post_mean(y, ones_bd, head):
    return y - _headsum(y, ones_bd) * (1.0 / head)


def _post_stats(dlt, t, p, ones_bd, head):
    return _headsum(dlt * dlt, ones_bd) * (1.0 / head), _headsum(t["r"] * t["k2"] * p["r_k"], ones_bd)


def _post_out(dlt, var, rk, t, p, proj):
    yn = dlt * lax.rsqrt(var + GN_EPS) * p["ln_w"] + p["ln_b"]
    o = ((yn + rk * t["v"]) * t["g"]).astype(BF16)
    return _dot(o, proj)


def _rwkv_post(y, r, k2, v, g, p, ones_bd, proj, head):
    t = dict(r=r, k2=k2, v=v, g=g)
    dlt = _post_mean(y, ones_bd, head)
    var, rk = _post_stats(dlt, t, p, ones_bd, head)
    return _post_out(dlt, var, rk, t, p, proj)


def _interleave(main, other):
    out = []
    done = 0
    for i, item in enumerate(main):
        out.append(item)
        upto = (len(other) * (i + 1)) // len(main)
        out.extend(other[done:upto])
        done = upto
    return out


def _rwkv_seq_block(cur_ref, carry_scr, p, ones_bd, tril, proj_ref, s_ref, y_ref, out_ref, head, chunk, n_groups):
    nseq, tr, _ = cur_ref.shape
    width = y_ref.shape[1]
    nh = width // head
    c = chunk
    per_seq = tr // c
    gsz = nseq // n_groups
    steps = int(math.log2(c))
    hs = lambda h: slice(h * head, (h + 1) * head)
    ri = lax.broadcasted_iota(jnp.int32, (c, c), 0)
    ci = lax.broadcasted_iota(jnp.int32, (c, c), 1)
    strict = ri > ci
    incl = ri >= ci
    row = lax.broadcasted_iota(jnp.int32, cur_ref.shape[1:], 0)
    tok = {}
    fac, ar, sc, pw, nm, m_rb, lv, mv, vk, s0, am, ub, dlt, stats = ({} for _ in range(14))

    def seqs(gi):
        return range(gi * gsz, (gi + 1) * gsz)

    def chunks(gi):
        return [(q * per_seq + t, q, slice(((q - gi * gsz) * per_seq + t) * c, ((q - gi * gsz) * per_seq + t + 1) * c))
                for q in seqs(gi) for t in range(per_seq)]

    def pairs(gi):
        return [(j, q, h) for j, q, _ in chunks(gi) for h in range(nh)]

    def grows(gi):
        return slice(gi * gsz * tr, (gi + 1) * gsz * tr)

    def pre(gi):
        curs, prevs = [], []
        for q in seqs(gi):
            cq = cur_ref[q]
            prevs.append(jnp.where(row == 0, carry_scr[q], pltpu.roll(cq, 1, axis=0)))
            carry_scr[q] = cq[tr - 1:tr, :]
            curs.append(cq)
        tok[gi] = _pre_lora(jnp.concatenate(curs, axis=0), jnp.concatenate(prevs, axis=0), p, width)

    def gates(gi):
        _pre_gates(tok[gi], p, ones_bd)

    def keys(gi):
        t = tok[gi]
        _pre_keys(t, p)
        for j, _, rs in chunks(gi):
            lds = t["ld"][rs]
            hi, mid, lo = _split3(lds)
            fac[j] = dict(lds=lds, cum=_dot(tril, hi) + _dot(tril, mid) + _dot(tril, lo))

    def decay(gi):
        t = tok[gi]
        for j, _, rs in chunks(gi):
            f = fac[j]
            cum = f["cum"]
            g_inv = jnp.exp(-cum)
            cum_end = cum[c - 1:c, :]
            g_rel = jnp.exp(cum_end - cum)
            f.update(a_t=t["av"][rs] * jnp.exp(cum - f["lds"]), r_t=t["r"][rs] * jnp.exp(cum),
                     b_t=t["bv"][rs] * g_inv, k_t=t["k2"][rs] * g_inv,
                     b_e=(t["bv"][rs] * g_rel).astype(BF16), k_e=(t["k2"][rs] * g_rel).astype(BF16),
                     g_end=jnp.exp(cum_end), v=t["v"][rs].astype(BF16))

    def scores(gi):
        for j, _, h in pairs(gi):
            f = fac[j]
            ar[j, h] = jnp.concatenate([f["a_t"][:, hs(h)], f["r_t"][:, hs(h)]], axis=0).astype(BF16)
            bk = jnp.concatenate([f["b_t"][:, hs(h)], f["k_t"][:, hs(h)]], axis=0).astype(BF16)
            sc[j, h] = _dot_nt(ar[j, h], bk)

    def lowers(gi):
        for j, _, h in pairs(gi):
            x = sc[j, h]
            vh = fac[j]["v"][:, hs(h)]
            pw[j, h] = jnp.where(strict, x[:c, :c], 0.0)
            nm[j, h] = pw[j, h]
            m_rb[j, h] = jnp.where(incl, x[c:, :c], 0.0).astype(BF16)
            lv[j, h] = _dot(jnp.where(strict, x[:c, c:], 0.0).astype(BF16), vh)
            mv[j, h] = _dot(jnp.where(incl, x[c:, c:], 0.0).astype(BF16), vh)
            vk[j, h] = _dot_tn(vh, fac[j]["k_e"][:, hs(h)])

    def square(gi):
        for j, _, h in pairs(gi):
            pb = pw[j, h].astype(BF16)
            pw[j, h] = _dot(pb, pb)

    def extend(gi):
        for j, _, h in pairs(gi):
            nm[j, h] = nm[j, h] + pw[j, h] + _dot(nm[j, h].astype(BF16), pw[j, h].astype(BF16))

    def chain(t, part):
        def project(gi):
            for j, q, h in sel(gi):
                s0[j, h] = s_ref[q, h]
                am[j, h] = _dot_nt(ar[j, h], s0[j, h].astype(BF16))

        def solve(gi):
            for j, q, h in sel(gi):
                w = am[j, h][:c] + lv[j, h]
                ub[j, h] = (w + _dot(nm[j, h].astype(BF16), w.astype(BF16))).astype(BF16)

        def update(gi):
            for j, q, h in sel(gi):
                f = fac[j]
                r0 = (q * per_seq + t) * c
                y_ref[r0:r0 + c, hs(h)] = am[j, h][c:] + _dot(m_rb[j, h], ub[j, h]) + mv[j, h]
                s_ref[q, h] = (s0[j, h] * f["g_end"][:, hs(h)] + _dot_tn(ub[j, h], f["b_e"][:, hs(h)])
                               + vk[j, h])

        def sel(gi):
            return [(j, q, h) for j, q, h in pairs(gi) if j == q * per_seq + t]

        return dict(project=project, solve=solve, update=update)[part]

    def post_mean(gi):
        dlt[gi] = _post_mean(y_ref[grows(gi), :], ones_bd, head)

    def post_stats(gi):
        stats[gi] = _post_stats(dlt[gi], tok[gi], p, ones_bd, head)

    def post_out(gi):
        out = _post_out(dlt[gi], *stats[gi], tok[gi], p, proj_ref[...])
        for i, q in enumerate(seqs(gi)):
            out_ref[q] = out[i * tr:(i + 1) * tr]

    front = [pre, gates, keys, decay, scores, lowers]
    middle = [st for _ in range(steps - 1) for st in (square, extend)]
    middle += [chain(t, part) for t in range(per_seq) for part in ("project", "solve", "update")]
    middle += [post_mean, post_stats, post_out]
    order = [(st, 0) for st in front]
    for gi in range(n_groups):
        nxt = [(st, gi + 1) for st in front] if gi + 1 < n_groups else []
        order += _interleave([(st, gi) for st in middle], nxt)
    for st, gi in order:
        st(gi)


_RWKV_VEC_NAMES = ("w0", "a0", "k_k", "k_a", "r_k", "ln_w", "ln_b")
_RWKV_CONSTS = ("mu", "vecs", "w2p", "a2p", "g2")


def _rwkv_params(mu_ref, vecs_ref, w2p_ref, a2p_ref, g2_ref):
    p = {n: vecs_ref[i:i + 1, :] for i, n in enumerate(_RWKV_VEC_NAMES)}
    p.update(mu=mu_ref[...], w2p=w2p_ref[...], a2p=a2p_ref[...], g2=g2_ref[...])
    return p


def _rwkv_seq_kernel(cur_ref, shift0_ref, s0_ref, *rest, chunk, head):
    np_ = len(_RWKV_CONSTS)
    p = _rwkv_params(*rest[:np_])
    proj_ref, ones_ref, tril_ref = rest[np_:np_ + 3]
    out_ref, st_ref = rest[np_ + 3:np_ + 5]
    carry_scr, y_scr = rest[np_ + 5:]

    @pl.when(pl.program_id(1) == 0)
    def _():
        carry_scr[...] = shift0_ref[...]
        st_ref[...] = s0_ref[...]

    _rwkv_seq_block(cur_ref, carry_scr, p, ones_ref[...], tril_ref[...], proj_ref, st_ref, y_scr, out_ref,
                    head, chunk, WKV_GROUPS)


def _rwkv_rows_pre_kernel(cur_ref, prev_ref, mu_ref, vecs_ref, w2p_ref, a2p_ref, g2_ref, ones_ref,
                          r_ref, k_ref, v_ref, g_ref, t_ref, *, width):
    p = _rwkv_params(mu_ref, vecs_ref, w2p_ref, a2p_ref, g2_ref)
    r, ld, k2, v, a, b, g = _rwkv_pre(cur_ref[...], prev_ref[...], p, ones_ref[...], width)
    r_ref[...] = r
    k_ref[...] = k2
    v_ref[...] = v
    g_ref[...] = g
    for i, val in enumerate((r, ld, k2, v, a, b)):
        t_ref[i] = val.T


def _wkv_step_kernel(t_ref, s_ref, y_ref, so_ref):
    r = t_ref[0]
    dec = jnp.exp(t_ref[1])
    k = t_ref[2]
    a = t_ref[4]
    b = t_ref[5]

    def body(i, carry):
        s = s_ref[i]
        sa = jnp.sum(s * a, axis=0, keepdims=True)
        s2 = s * dec + sa * b + t_ref[3, pl.ds(i, 1), :] * k
        so_ref[i] = s2
        y_ref[pl.ds(i, 1), :] = jnp.sum(s2 * r, axis=0, keepdims=True)
        return carry

    lax.fori_loop(0, s_ref.shape[0], body, 0, unroll=V7X_SUBLANES)


def _rwkv_rows_post_kernel(yt_ref, r_ref, k_ref, v_ref, g_ref, mu_ref, vecs_ref, w2p_ref, a2p_ref, g2_ref,
                           proj_ref, ones_ref, o_ref, *, head):
    p = _rwkv_params(mu_ref, vecs_ref, w2p_ref, a2p_ref, g2_ref)
    o_ref[...] = _rwkv_post(yt_ref[...].T, r_ref[...], k_ref[...], v_ref[...], g_ref[...], p,
                            ones_ref[...], proj_ref[...], head)


def _rwkv_param_list(rw):
    return [rw[n] for n in _RWKV_CONSTS]


def _rwkv_seq_call(cur3, shift0, s0, rw, *, tr, nseq):
    nb, length, n_shift = cur3.shape
    nh, head = s0.shape[1], s0.shape[2]
    width = nh * head
    d = rw["proj"].shape[1]
    params = _rwkv_param_list(rw)
    chunk = min(WKV_CHUNK, tr)
    tril = (jnp.arange(chunk)[:, None] >= jnp.arange(chunk)[None, :]).astype(BF16)
    consts = params + [rw["proj"], rw["ones_bd"], tril]
    kern = functools.partial(_rwkv_seq_kernel, chunk=chunk, head=head)
    return pl.pallas_call(
        kern,
        out_shape=(jax.ShapeDtypeStruct((nb, length, d), F32),
                   jax.ShapeDtypeStruct((nb, nh, head, head), F32)),
        grid=(nb // nseq, length // tr),
        in_specs=[pl.BlockSpec((nseq, tr, n_shift), lambda q, i: (q, i, 0)),
                  pl.BlockSpec((nseq, 1, n_shift), lambda q, i: (q, 0, 0)),
                  pl.BlockSpec((nseq, nh, head, head), lambda q, i: (q, 0, 0, 0))]
                 + [pl.BlockSpec(c.shape, lambda q, i, nd=c.ndim: (0,) * nd) for c in consts],
        out_specs=(pl.BlockSpec((nseq, tr, d), lambda q, i: (q, i, 0)),
                   pl.BlockSpec((nseq, nh, head, head), lambda q, i: (q, 0, 0, 0))),
        scratch_shapes=[pltpu.VMEM((nseq, 1, n_shift), F32), pltpu.VMEM((nseq * tr, width), F32)],
        compiler_params=_cparams(("arbitrary", "arbitrary"), 40 * 2**20),
        name="rwkv_seq",
    )(cur3, shift0.reshape(nb, 1, n_shift), s0, *consts)


def _rwkv_rows_call(cur, prev, s0_t, rw):
    rows, n_shift = cur.shape
    nh, head = s0_t.shape[0], s0_t.shape[1]
    width = nh * head
    d = rw["proj"].shape[1]
    params = _rwkv_param_list(rw)
    full = lambda shape: pl.BlockSpec(shape, lambda i: (0,) * len(shape))
    vec = jax.ShapeDtypeStruct((rows, width), F32)
    n_t = 6

    pre_consts = params + [rw["ones_bd"]]
    r, k2, v, g, t = pl.pallas_call(
        functools.partial(_rwkv_rows_pre_kernel, width=width),
        out_shape=(vec,) * 4 + (jax.ShapeDtypeStruct((n_t, width, rows), F32),),
        grid=(1,),
        in_specs=[full(cur.shape), full(prev.shape)] + [full(c.shape) for c in pre_consts],
        out_specs=(full((rows, width)),) * 4 + (full((n_t, width, rows)),),
        compiler_params=_cparams(("arbitrary",), 32 * 2**20),
        name="rwkv_rows_pre",
    )(cur, prev, *pre_consts)

    sblk = pl.BlockSpec((None, head, head, rows), lambda h: (h, 0, 0, 0))
    yt, s1_t = pl.pallas_call(
        _wkv_step_kernel,
        out_shape=(jax.ShapeDtypeStruct((width, rows), F32), jax.ShapeDtypeStruct(s0_t.shape, F32)),
        grid=(nh,),
        in_specs=[pl.BlockSpec((n_t, head, rows), lambda h: (0, h, 0)), sblk],
        out_specs=(pl.BlockSpec((head, rows), lambda h: (h, 0)), sblk),
        compiler_params=_cparams(("parallel",), 32 * 2**20),
        name="wkv_step",
    )(t, s0_t)

    post_consts = params + [rw["proj"], rw["ones_bd"]]
    out = pl.pallas_call(
        functools.partial(_rwkv_rows_post_kernel, head=head),
        out_shape=jax.ShapeDtypeStruct((rows, d), F32),
        grid=(1,),
        in_specs=[full((width, rows))] + [full((rows, width))] * 4 + [full(c.shape) for c in post_consts],
        out_specs=full((rows, d)),
        compiler_params=_cparams(("arbitrary",), 32 * 2**20),
        name="rwkv_rows_post",
    )(yt, r, k2, v, g, *post_consts)
    return out, s1_t


def _merge_kernel(h_ref, gates_ref, ys5_ref, yrw_ref, gt2_ref, wout_ref,
                  sh_ref, sc_ref, gt_ref, g_ref, w1_ref, w3_ref, w2_ref, gfin_ref, o_ref, *, n_chunks,
                  tiles_per_seq):
    row = functools.partial(_mod_row, tiles_per_seq=tiles_per_seq)
    d = h_ref.shape[1]
    gt2, sh, sc, gt = row(gt2_ref), row(sh_ref), row(sc_ref), row(gt_ref)
    parts = _row_parts(h_ref.shape[0], ROW_PARTS)
    mixed = []
    for part in parts:
        m = (_sigmoid(gates_ref[part, :d]) * ys5_ref[part, :]
             + _sigmoid(gates_ref[part, d:]) * yrw_ref[part, :])
        mixed.append(_dot(m.astype(BF16), wout_ref[...]))
    for part, mo in zip(parts, mixed):
        h2 = h_ref[part, :] + _mod_part(gt2, part) * mo
        h3 = _ffn_core(h2, _mod_part(sh, part), _mod_part(sc, part), _mod_part(gt, part), g_ref[...],
                       w1_ref, w3_ref, w2_ref, n_chunks)
        o_ref[part, :] = _rms(h3, gfin_ref[...])


def _merge_call(h, gates, ys5, yrw, mods, ks, w_out, g, w1, w3, w2, g_fin, *, tm):
    rows, d = h.shape
    d_ff = w1.shape[1]
    kern = functools.partial(_merge_kernel, n_chunks=_ffn_chunks(d_ff), tiles_per_seq=mods.tiles_per_seq)
    row_spec = lambda w: pl.BlockSpec((tm, w), lambda i: (i, 0))
    weights = (3 * d * d_ff + d * d) * 2
    tiles = 2 * tm * 6 * d * 4 + 4 * tm * d_ff * 4 + 6 * tm * d * 4
    ys5_spec = row_spec(d) if mods.tiles_per_seq is None else _time_major(rows, d, tm, mods.tiles_per_seq)[1]
    return pl.pallas_call(
        kern,
        out_shape=jax.ShapeDtypeStruct((rows, d), F32),
        grid=(rows // tm,),
        in_specs=[row_spec(d), row_spec(2 * d), ys5_spec, row_spec(d),
                  mods.spec(ks[0]), _const_spec(w_out.shape),
                  mods.spec(ks[1]), mods.spec(ks[2]), mods.spec(ks[3]),
                  _const_spec((1, d)), _const_spec(w1.shape), _const_spec(w3.shape), _const_spec(w2.shape),
                  _const_spec((1, d))],
        out_specs=row_spec(d),
        compiler_params=_cparams(("parallel",), weights + tiles + 8 * 2**20),
        name="merge_ffn2",
    )(h, gates, ys5, yrw, mods.mod, w_out, mods.mod, mods.mod, mods.mod, g.reshape(1, d), w1, w3, w2,
      g_fin.reshape(1, d))


_LATE_WEIGHTS = ("ffn2_w1", "ffn2_w3", "ffn2_w2", "w_in", "w_out", "glu_v", "glu_g", "proj", "g2", "w2p", "a2p")


def _pick_tile(n, target):
    t = min(n, target)
    while n % t:
        t -= V7X_SUBLANES
    return t


def _layer(x, mod, mod_row0, s5_re0, s5_im0, wkv0, shift0, w, *, sequential):
    nb, length, d = x.shape
    rows = nb * length
    g, n = s5_re0.shape[1], s5_re0.shape[2]
    ns = g * n
    n_shift = shift0.shape[1]
    s5_width = w["s5"]["d"].shape[1]
    widths = (s5_width, n_shift, 2 * d)

    tm = _pick_tile(length if sequential else rows, 512)
    tm2 = _pick_tile(tm, 512)
    mods = _ModRows(mod, mod_row0, nb, tm, length // tm if sequential else None)
    mods2 = _ModRows(mod, mod_row0, nb, tm2, length // tm2 if sequential else None)

    x2 = x.reshape(rows, d)
    late = w["late"]
    to_cast = [n for n in _LATE_WEIGHTS if late[n].dtype != BF16]
    h1, cast = _ffn_call(x2, mods, (0, 1, 2), w["g_ffn1"], *w["ffn1"], tm=tm, casts=[late[n] for n in to_cast])
    late = dict(late, **dict(zip(to_cast, cast)))
    w = dict(w, late=late, w_in=late["w_in"], w_out=late["w_out"],
             ffn2=(late["ffn2_w1"], late["ffn2_w3"], late["ffn2_w2"]),
             s5=dict(w["s5"], glu_v=late["glu_v"], glu_g=late["glu_g"]),
             rwkv=dict(w["rwkv"], proj=late["proj"], g2=late["g2"], w2p=late["w2p"], a2p=late["a2p"]))
    s5_in, cur, gates = _proj_call(h1, mods, (3, 4), w["g_mix"], w["w_in"], widths, tm=tm)

    h0re = s5_re0.reshape(nb, ns)
    h0im = s5_im0.reshape(nb, ns)
    if sequential:
        ts = _pick_tile(length, S5_STEP_ROWS // nb)
        y_s5, hre, him = _s5_seq_call(s5_in, h0re, h0im, w["a_re"], w["a_im"], w["s5"], ts=ts)
        nseq = WKV_SEQS if nb % WKV_SEQS == 0 else 1
        tr = _pick_tile(length, WKV_BLOCK_CHUNKS * WKV_CHUNK // nseq)
        y_rw, wkv1 = _rwkv_seq_call(cur.reshape(nb, length, n_shift), shift0, wkv0, w["rwkv"], tr=tr, nseq=nseq)
        y_rw = y_rw.reshape(rows, d)
    else:
        y_s5, hre, him = _s5_rows_call(s5_in, h0re, h0im, w["a_re"], w["a_im"], w["s5"])
        y_rw, wkv1 = _rwkv_rows_call(cur, shift0, wkv0, w["rwkv"])

    y = _merge_call(h1, gates, y_s5, y_rw, mods2, (5, 6, 7, 8), w["w_out"],
                    w["g_ffn2"], *w["ffn2"], w["g_final"], tm=tm2)

    shift1 = cur.reshape(nb, length, n_shift)[:, -1]
    return (y.reshape(nb, length, d), hre.reshape(nb, g, n), him.reshape(nb, g, n), wkv1, shift1), w


def kernel(x_prompt, x_sample, c_prompt, c_sample, state_s5_re, state_s5_im, state_wkv, state_shift, w_ada, b_ada, g_ffn1, g_mix, g_ffn2, g_final, ffn1_w1, ffn1_w3, ffn1_w2, ffn2_w1, ffn2_w3, ffn2_w2, w_in, mu_shift, s5_lam_re, s5_lam_im, s5_log_dt, s5_b_re, s5_b_im, s5_c_re, s5_c_im, s5_d, s5_glu_v, s5_glu_g, rwkv_w0, rwkv_w2, rwkv_a0, rwkv_a2, rwkv_g2, rwkv_k_k, rwkv_k_a, rwkv_r_k, rwkv_ln_w, rwkv_ln_b, rwkv_proj, w_out):
    bp, _, d = x_prompt.shape
    bs = x_sample.shape[0]
    g, n = s5_lam_re.shape
    nh, head = rwkv_r_k.shape
    width = nh * head
    n_shift = mu_shift.shape[0]
    assert bs % bp == 0, "sample rows come first in the modulation array; prompt rows must stay block-aligned"
    bf = lambda t: t.astype(BF16)

    a_re, a_im, wb, wc_re, wc_im = _s5_prep_call(
        s5_lam_re, s5_lam_im, s5_log_dt, jnp.swapaxes(s5_b_re, 1, 2), jnp.swapaxes(s5_b_im, 1, 2),
        s5_c_re, s5_c_im, bp)
    w_lora = rwkv_w2.shape[0]
    a_lora = rwkv_a2.shape[0]
    w2p = jnp.concatenate([rwkv_w2, jnp.zeros((a_lora, width), F32)], axis=0)
    a2p = jnp.concatenate([jnp.zeros((w_lora, width), F32), rwkv_a2], axis=0)
    vec_rows = [rwkv_w0, rwkv_a0, rwkv_k_k, rwkv_k_a, rwkv_r_k.reshape(width), rwkv_ln_w, rwkv_ln_b]
    vecs = jnp.stack(vec_rows + [jnp.zeros_like(rwkv_w0)] * (V7X_SUBLANES - len(vec_rows)))
    rw = dict(mu=mu_shift.reshape(1, n_shift), vecs=vecs,
              ones_bd=jnp.kron(jnp.eye(nh, dtype=F32), jnp.ones((head, head), F32)).astype(BF16))
    late = dict(ffn2_w1=ffn2_w1, ffn2_w3=ffn2_w3, ffn2_w2=ffn2_w2, w_in=w_in, w_out=w_out,
                glu_v=s5_glu_v, glu_g=s5_glu_g, proj=rwkv_proj, g2=rwkv_g2, w2p=w2p, a2p=a2p)
    w = dict(
        g_ffn1=g_ffn1, g_mix=g_mix, g_ffn2=g_ffn2, g_final=g_final,
        ffn1=(bf(ffn1_w1), bf(ffn1_w3), bf(ffn1_w2)),
        a_re=a_re, a_im=a_im,
        s5=dict(wb=wb, wc_re=wc_re, wc_im=wc_im, d=s5_d.reshape(1, -1)),
        rwkv=rw, late=late,
    )

    mod = _mod_call(jnp.concatenate([c_sample, c_prompt], axis=0), w_ada, b_ada)

    z_s5 = jnp.zeros((bp, g, n), state_s5_re.dtype)
    z_wkv = jnp.zeros((bp, nh, head, head), state_wkv.dtype)
    z_shift = jnp.zeros((bp, n_shift), state_shift.dtype)
    (y_p, s5re_p, s5im_p, wkv_p, shift_p), w = _layer(x_prompt, mod, bs, z_s5, z_s5, z_wkv, z_shift, w,
                                                      sequential=True)
    (y_s, s5re_s, s5im_s, wkv_s_t, shift_s), _ = _layer(x_sample, mod, 0, state_s5_re, state_s5_im,
                                                        jnp.transpose(state_wkv, (1, 2, 3, 0)), state_shift, w,
                                                        sequential=False)
    wkv_s = jnp.transpose(wkv_s_t, (3, 0, 1, 2))
    return (y_p, y_s, s5re_p, s5im_p, wkv_p, shift_p, s5re_s, s5im_s, wkv_s, shift_s)
```

```python
import functools
import math

import jax
import jax.numpy as jnp
from jax import lax
from jax.experimental import pallas as pl
from jax.experimental.pallas import tpu as pltpu

F32 = jnp.float32
BF16 = jnp.bfloat16

NORM_EPS = 1e-6
GN_EPS = 64e-5
N_MOD = 9

V7X_LANES = 128
V7X_SUBLANES = 8
V7X_MXU_DIM = 256
V7X_VMEM_BYTES = 64 * 1024 * 1024
VMEM_CAP_BYTES = V7X_VMEM_BYTES - 8 * 1024 * 1024

ROW_PARTS = 2
S5_STEP_ROWS = 1024
S5_OUT_PARTS = 1
WKV_CHUNK = 64
WKV_BLOCK_CHUNKS = 8
WKV_SEQS = 8
WKV_GROUPS = 1


def _cparams(semantics, vmem_bytes):
    return pltpu.CompilerParams(
        dimension_semantics=semantics,
        vmem_limit_bytes=int(min(vmem_bytes, VMEM_CAP_BYTES)),
    )


def _const_spec(shape):
    nd = len(shape)
    return pl.BlockSpec(shape, lambda *_: (0,) * nd, pipeline_mode=pl.Buffered(1))


def _dot(a, b):
    return jnp.dot(a, b, preferred_element_type=F32)


def _dot_nt(a, b):
    return lax.dot_general(a, b, (((1,), (1,)), ((), ())), preferred_element_type=F32)


def _dot_tn(a, b):
    return lax.dot_general(a, b, (((0,), (0,)), ((), ())), preferred_element_type=F32)


def _split3(x):
    hi = x.astype(BF16)
    r1 = x - hi.astype(F32)
    mid = r1.astype(BF16)
    lo = (r1 - mid.astype(F32)).astype(BF16)
    return hi, mid, lo


def _rms(x, g):
    ms = jnp.mean(x * x, axis=-1, keepdims=True)
    return x * lax.rsqrt(ms + NORM_EPS) * g


def _sigmoid(x):
    return jax.nn.sigmoid(x)


def _softplus(x):
    return jnp.maximum(x, 0.0) + jnp.log(1.0 + jnp.exp(-jnp.abs(x)))


def _mod_kernel(c_ref, w_ref, b_ref, o_ref):
    c = c_ref[...]
    s = (c * _sigmoid(c)).astype(BF16)
    o_ref[...] = _dot(s, w_ref[...].astype(BF16)) + b_ref[...]


def _mod_call(c, w_ada, b_ada):
    rows, d = c.shape
    n = w_ada.shape[1]
    return pl.pallas_call(
        _mod_kernel,
        out_shape=jax.ShapeDtypeStruct((n // d, rows, d), F32),
        grid=(n // d,),
        in_specs=[
            pl.BlockSpec((rows, d), lambda j: (0, 0)),
            pl.BlockSpec((d, d), lambda j: (0, j)),
            pl.BlockSpec((1, d), lambda j: (0, j)),
        ],
        out_specs=pl.BlockSpec((None, rows, d), lambda j: (j, 0, 0)),
        compiler_params=_cparams(("parallel",), 32 * 2**20),
        name="adaln_mod",
    )(c, w_ada, b_ada.reshape(1, n))


class _ModRows:
    def __init__(self, mod, row0, nrows, tm, tiles_per_seq):
        self.mod, self.row0, self.nrows, self.tm, self.tiles_per_seq = mod, row0, nrows, tm, tiles_per_seq

    def spec(self, k):
        d = self.mod.shape[2]
        if self.tiles_per_seq is None:
            blk0 = self.row0 // self.tm
            return pl.BlockSpec((None, self.tm, d), lambda i: (k, blk0 + i, 0))
        blk0 = self.row0 // self.nrows
        return pl.BlockSpec((None, self.nrows, d), lambda i: (k, blk0, 0))


def _mod_row(ref, tiles_per_seq):
    if tiles_per_seq is None:
        return ref[...]
    return ref[pl.ds(pl.program_id(0) // tiles_per_seq, 1), :]


def _ffn_core(x, sh, sc, gt, g, w1_ref, w3_ref, w2_ref, n_chunks):
    bounds = _ffn_bounds(w1_ref.shape[1], n_chunks)
    u = (_rms(x, g) * (1.0 + sc) + sh).astype(BF16)
    acc = None
    for lo, hi in zip(bounds[:-1], bounds[1:]):
        a = _dot(u, w1_ref[:, lo:hi])
        b = _dot(u, w3_ref[:, lo:hi])
        hm = (a * _sigmoid(a) * b).astype(BF16)
        part = _dot(hm, w2_ref[lo:hi, :])
        acc = part if acc is None else acc + part
    return x + (0.5 * gt) * acc


def _row_parts(rows, n_parts):
    if rows % (n_parts * 2 * V7X_SUBLANES):
        return [slice(0, rows)]
    step = rows // n_parts
    return [slice(r * step, (r + 1) * step) for r in range(n_parts)]


def _mod_part(mod_rows, part):
    return mod_rows if mod_rows.shape[0] == 1 else mod_rows[part]


def _ffn_bounds(d_ff, n_chunks):
    tiles = -(-d_ff // V7X_MXU_DIM)
    return [min(d_ff, ((tiles * c + n_chunks - 1) // n_chunks) * V7X_MXU_DIM) for c in range(n_chunks + 1)]


def _ffn_kernel(x_ref, sh_ref, sc_ref, gt_ref, g_ref, w1_ref, w3_ref, w2_ref, *rest, n_chunks, tiles_per_seq):
    n_cast = len(rest) // 2
    o_ref = rest[n_cast]
    row = functools.partial(_mod_row, tiles_per_seq=tiles_per_seq)
    sh, sc, gt = row(sh_ref), row(sc_ref), row(gt_ref)
    for part in _row_parts(x_ref.shape[0], ROW_PARTS):
        o_ref[part, :] = _ffn_core(x_ref[part, :], _mod_part(sh, part), _mod_part(sc, part), _mod_part(gt, part),
                                   g_ref[...], w1_ref, w3_ref, w2_ref, n_chunks)
    for src, dst in zip(rest[:n_cast], rest[n_cast + 1:]):
        dst[...] = src[...].astype(BF16)


def _cast_blocks(arr, n_steps):
    bf16_rows = 2 * V7X_SUBLANES
    nblk = n_steps
    while arr.shape[0] % nblk or (arr.shape[0] // nblk) % bf16_rows:
        nblk //= 2
    per = n_steps // nblk
    return pl.BlockSpec((arr.shape[0] // nblk, arr.shape[1]), lambda i: (i // per, 0))


def _ffn_chunks(d_ff):
    return 2 if d_ff % (2 * V7X_LANES) == 0 else 1


def _ffn_call(x2, mods, ks, g, w1, w3, w2, *, tm, casts=()):
    rows, d = x2.shape
    d_ff = w1.shape[1]
    n_steps = rows // tm
    kern = functools.partial(_ffn_kernel, n_chunks=_ffn_chunks(d_ff), tiles_per_seq=mods.tiles_per_seq)
    weights = 3 * d * d_ff * 2
    tiles = 4 * tm * d * 4 + 4 * tm * d_ff * 4
    cast_specs = [_cast_blocks(a, n_steps) for a in casts]
    cast_bytes = sum(2 * 6 * s.block_shape[0] * s.block_shape[1] for s in cast_specs)
    out = pl.pallas_call(
        kern,
        out_shape=(jax.ShapeDtypeStruct((rows, d), F32),) + tuple(jax.ShapeDtypeStruct(a.shape, BF16) for a in casts),
        grid=(n_steps,),
        in_specs=[
            pl.BlockSpec((tm, d), lambda i: (i, 0)),
            mods.spec(ks[0]), mods.spec(ks[1]), mods.spec(ks[2]),
            _const_spec((1, d)),
            _const_spec(w1.shape), _const_spec(w3.shape), _const_spec(w2.shape),
        ] + cast_specs,
        out_specs=(pl.BlockSpec((tm, d), lambda i: (i, 0)),) + tuple(cast_specs),
        compiler_params=_cparams(("arbitrary",), weights + tiles + cast_bytes + 8 * 2**20),
        name="ffn1",
    )(x2, mods.mod, mods.mod, mods.mod, g.reshape(1, d), w1, w3, w2, *casts)
    return out[0], list(out[1:])


def _proj_kernel(h_ref, sh_ref, sc_ref, g_ref, win_ref, s5_ref, cur_ref, gates_ref, *, tiles_per_seq):
    row = functools.partial(_mod_row, tiles_per_seq=tiles_per_seq)
    u = (_rms(h_ref[...], g_ref[...]) * (1.0 + row(sc_ref)) + row(sh_ref)).astype(BF16)
    n0 = s5_ref.shape[1]
    n1 = n0 + cur_ref.shape[1]
    s5_ref[...] = _dot(u, win_ref[:, :n0])
    cur_ref[...] = _dot(u, win_ref[:, n0:n1])
    gates_ref[...] = _dot(u, win_ref[:, n1:])


def _time_major(rows, width, tm, tiles_per_seq):
    length = tiles_per_seq * tm
    spec = pl.BlockSpec((tm, width), lambda i: (i % tiles_per_seq, i // tiles_per_seq))
    return (length, rows // length * width), spec


def _proj_call(h, mods, ks, g, w_in, widths, *, tm):
    rows, d = h.shape
    n_in = w_in.shape[1]
    shapes = [(rows, w) for w in widths]
    specs = [pl.BlockSpec((tm, w), lambda i: (i, 0)) for w in widths]
    if mods.tiles_per_seq is not None:
        shapes[0], specs[0] = _time_major(rows, widths[0], tm, mods.tiles_per_seq)
    return pl.pallas_call(
        functools.partial(_proj_kernel, tiles_per_seq=mods.tiles_per_seq),
        out_shape=tuple(jax.ShapeDtypeStruct(s, F32) for s in shapes),
        grid=(rows // tm,),
        in_specs=[
            pl.BlockSpec((tm, d), lambda i: (i, 0)),
            mods.spec(ks[0]), mods.spec(ks[1]),
            _const_spec((1, d)),
            _const_spec(w_in.shape),
        ],
        out_specs=tuple(specs),
        compiler_params=_cparams(("parallel",), d * n_in * 2 + 3 * tm * (d + n_in) * 4 + 8 * 2**20),
        name="mix_proj",
    )(h, mods.mod, mods.mod, g.reshape(1, d), w_in)


def _s5_prep_kernel(lre_ref, lim_ref, ldt_ref, bre_ref, bim_ref, cre_ref, cim_ref,
                    are_ref, aim_ref, wb_ref, wcre_ref, wcim_ref, *, gpb):
    lre = lre_ref[...]
    lim = lim_ref[...]
    dt = jnp.exp(ldt_ref[...])
    mag = jnp.exp(lre * dt)
    ang = lim * dt
    a_re = mag * jnp.cos(ang)
    a_im = mag * jnp.sin(ang)
    den = lre * lre + lim * lim
    num_re = a_re - 1.0
    k_re = (num_re * lre + a_im * lim) / den
    k_im = (a_im * lre - num_re * lim) / den
    b_re = bre_ref[...]
    b_im = bim_ref[...]
    bb_re = (k_re * b_re - k_im * b_im).astype(BF16)
    bb_im = (k_re * b_im + k_im * b_re).astype(BF16)
    c_re = cre_ref[...].astype(BF16)
    c_im = (-cim_ref[...]).astype(BF16)
    g, c, n = b_re.shape
    nb = are_ref.shape[0]
    s = gpb * n
    wb_ref[...] = jnp.zeros(wb_ref.shape, BF16)
    wcre_ref[...] = jnp.zeros(wcre_ref.shape, BF16)
    wcim_ref[...] = jnp.zeros(wcim_ref.shape, BF16)
    for gi in range(g):
        k, gl = divmod(gi, gpb)
        rows = slice(gl * c, (gl + 1) * c)
        cols = slice(gl * n, (gl + 1) * n)
        are_ref[:, gi * n:(gi + 1) * n] = jnp.broadcast_to(a_re[gi], (nb, n))
        aim_ref[:, gi * n:(gi + 1) * n] = jnp.broadcast_to(a_im[gi], (nb, n))
        wb_ref[k, rows, cols] = bb_re[gi]
        wb_ref[k, rows, s + gl * n:s + (gl + 1) * n] = bb_im[gi]
        wcre_ref[k, rows, cols] = c_re[gi]
        wcim_ref[k, rows, cols] = c_im[gi]


def _s5_prep_call(lam_re, lam_im, log_dt, b_re_t, b_im_t, c_re, c_im, nb):
    g, n = lam_re.shape
    c = b_re_t.shape[1]
    gpb = max(1, min(g, V7X_MXU_DIM // c))
    nk = g // gpb
    return pl.pallas_call(
        functools.partial(_s5_prep_kernel, gpb=gpb),
        out_shape=(jax.ShapeDtypeStruct((nb, g * n), F32), jax.ShapeDtypeStruct((nb, g * n), F32),
                   jax.ShapeDtypeStruct((nk, gpb * c, 2 * gpb * n), BF16),
                   jax.ShapeDtypeStruct((nk, gpb * c, gpb * n), BF16),
                   jax.ShapeDtypeStruct((nk, gpb * c, gpb * n), BF16)),
        name="s5_discretise",
    )(lam_re.reshape(g, 1, n), lam_im.reshape(g, 1, n), log_dt.reshape(g, 1, 1), b_re_t, b_im_t, c_re, c_im)


def _s5_input_matmul(ub, wb_ref, bure_ref, buim_ref):
    nk, kin, two_s = wb_ref.shape
    s = two_s // 2
    for k in range(nk):
        res = _dot(ub[:, k * kin:(k + 1) * kin], wb_ref[k])
        bure_ref[:, k * s:(k + 1) * s] = res[:, :s]
        buim_ref[:, k * s:(k + 1) * s] = res[:, s:]


def _s5_output(xre, xim, u, wcre_ref, wcim_ref, d_row):
    nk, kout, s = wcre_ref.shape
    xre_b = xre.astype(BF16)
    xim_b = xim.astype(BF16)
    ys = []
    for k in range(nk):
        ys.append(_dot_nt(xre_b[:, k * s:(k + 1) * s], wcre_ref[k])
                  + _dot_nt(xim_b[:, k * s:(k + 1) * s], wcim_ref[k]))
    y = jnp.concatenate(ys, axis=-1) + d_row * u
    return jax.nn.gelu(y)


def _s5_glu(zb, gv_ref, gg_ref):
    return _dot(zb, gv_ref[...]) * _sigmoid(_dot(zb, gg_ref[...]))


def _s5_seq_kernel(u_ref, h0re_ref, h0im_ref, are_ref, aim_ref, wb_ref, wcre_ref, wcim_ref, d_ref,
                   gv_ref, gg_ref,
                   o_ref, hre_ref, him_ref,
                   bure_scr, buim_scr, xre_scr, xim_scr, uslab_scr, oslab_scr, *, lane_chunk, n_parts):
    nb = hre_ref.shape[0]
    ts = u_ref.shape[0]
    din = u_ref.shape[1] // nb
    dout = o_ref.shape[1] // nb

    @pl.when(pl.program_id(0) == 0)
    def _():
        hre_ref[...] = h0re_ref[...]
        him_ref[...] = h0im_ref[...]

    for b in range(nb):
        for j in range(din // V7X_LANES):
            col = b * din + j * V7X_LANES
            uslab_scr[j, pl.ds(b, ts, stride=nb), :] = u_ref[:, col:col + V7X_LANES]
    u_tm = jnp.concatenate([uslab_scr[j] for j in range(din // V7X_LANES)], axis=1)
    _s5_input_matmul(u_tm.astype(BF16), wb_ref, bure_scr, buim_scr)

    ns = are_ref.shape[1]
    lanes = [slice(lc * lane_chunk, (lc + 1) * lane_chunk) for lc in range(ns // lane_chunk)]
    state = [(hre_ref[:, sl], him_ref[:, sl]) for sl in lanes]
    tp = ts // n_parts
    for part in range(n_parts):
        for lc, sl in enumerate(lanes):
            ar = are_ref[:, sl]
            ai = aim_ref[:, sl]
            sr, si = state[lc]
            for t in range(part * tp, (part + 1) * tp):
                rs = slice(t * nb, (t + 1) * nb)
                nr = ar * sr - ai * si + bure_scr[rs, sl]
                ni = ar * si + ai * sr + buim_scr[rs, sl]
                xre_scr[rs, sl] = nr
                xim_scr[rs, sl] = ni
                sr, si = nr, ni
            state[lc] = (sr, si)
        pr = slice(part * tp * nb, (part + 1) * tp * nb)
        z_tm = _s5_output(xre_scr[pr, :], xim_scr[pr, :], u_tm[pr], wcre_ref, wcim_ref, d_ref[...])
        out = _s5_glu(z_tm.astype(BF16), gv_ref, gg_ref)
        for j in range(dout // V7X_LANES):
            oslab_scr[j, pr, :] = out[:, j * V7X_LANES:(j + 1) * V7X_LANES]
    for lc, sl in enumerate(lanes):
        hre_ref[:, sl], him_ref[:, sl] = state[lc]
    for b in range(nb):
        for j in range(dout // V7X_LANES):
            col = b * dout + j * V7X_LANES
            o_ref[:, col:col + V7X_LANES] = oslab_scr[j, pl.ds(b, ts, stride=nb), :]


def _s5_rows_kernel(u_ref, h0re_ref, h0im_ref, are_ref, aim_ref, wb_ref, wcre_ref, wcim_ref, d_ref,
                    gv_ref, gg_ref, o_ref, hre_ref, him_ref, bure_scr, buim_scr):
    u = u_ref[...]
    _s5_input_matmul(u.astype(BF16), wb_ref, bure_scr, buim_scr)
    ar = are_ref[0:1, :]
    ai = aim_ref[0:1, :]
    sr = h0re_ref[...]
    si = h0im_ref[...]
    nr = ar * sr - ai * si + bure_scr[...]
    ni = ar * si + ai * sr + buim_scr[...]
    hre_ref[...] = nr
    him_ref[...] = ni
    z = _s5_output(nr, ni, u, wcre_ref, wcim_ref, d_ref[...])
    o_ref[...] = _s5_glu(z.astype(BF16), gv_ref, gg_ref)


def _s5_weight_specs(ws):
    return [_const_spec(w.shape) for w in ws]


def _s5_seq_call(s5_in, h0re, h0im, a_re, a_im, sw, *, ts):
    nb, ns = a_re.shape
    length = s5_in.shape[0]
    din = s5_in.shape[1] // nb
    d = sw["glu_v"].shape[1]
    rows = nb * ts
    ws = [sw["wb"], sw["wc_re"], sw["wc_im"], sw["d"], sw["glu_v"], sw["glu_g"]]
    kern = functools.partial(_s5_seq_kernel, lane_chunk=4 * V7X_LANES, n_parts=S5_OUT_PARTS)
    return pl.pallas_call(
        kern,
        out_shape=(jax.ShapeDtypeStruct((length, nb * d), F32),
                   jax.ShapeDtypeStruct((nb, ns), F32), jax.ShapeDtypeStruct((nb, ns), F32)),
        grid=(length // ts,),
        in_specs=[pl.BlockSpec((ts, nb * din), lambda c: (c, 0)),
                  _const_spec((nb, ns)), _const_spec((nb, ns)), _const_spec((nb, ns)), _const_spec((nb, ns))]
                 + _s5_weight_specs(ws),
        out_specs=(pl.BlockSpec((ts, nb * d), lambda c: (c, 0)),
                   pl.BlockSpec((nb, ns), lambda c: (0, 0)), pl.BlockSpec((nb, ns), lambda c: (0, 0))),
        scratch_shapes=[pltpu.VMEM((rows, ns), F32) for _ in range(4)]
                       + [pltpu.VMEM((din // V7X_LANES, rows, V7X_LANES), F32),
                          pltpu.VMEM((d // V7X_LANES, rows, V7X_LANES), F32)],
        compiler_params=_cparams(("arbitrary",), 48 * 2**20),
        name="s5_seq",
    )(s5_in, h0re, h0im, a_re, a_im, *ws)


def _s5_rows_call(s5_in, h0re, h0im, a_re, a_im, sw):
    rows, din = s5_in.shape
    ns = a_re.shape[1]
    d = sw["glu_v"].shape[1]
    ws = [sw["wb"], sw["wc_re"], sw["wc_im"], sw["d"], sw["glu_v"], sw["glu_g"]]
    full = lambda shape: pl.BlockSpec(shape, lambda i: (0,) * len(shape))
    return pl.pallas_call(
        _s5_rows_kernel,
        out_shape=(jax.ShapeDtypeStruct((rows, d), F32),
                   jax.ShapeDtypeStruct((rows, ns), F32), jax.ShapeDtypeStruct((rows, ns), F32)),
        grid=(1,),
        in_specs=[full((rows, din)), full((rows, ns)), full((rows, ns)), full(a_re.shape), full(a_im.shape)]
                 + [full(w.shape) for w in ws],
        out_specs=(full((rows, d)), full((rows, ns)), full((rows, ns))),
        scratch_shapes=[pltpu.VMEM((rows, ns), F32) for _ in range(2)],
        compiler_params=_cparams(("arbitrary",), 40 * 2**20),
        name="s5_rows",
    )(s5_in, h0re, h0im, a_re, a_im, *ws)


def _headsum(x, ones_bd):
    hi = x.astype(BF16)
    lo = (x - hi.astype(F32)).astype(BF16)
    return _dot(hi, ones_bd) + _dot(lo, ones_bd)


def _pre_lora(cur, prev, p, width):
    mixed = cur + p["mu"] * (prev - cur)
    nl = p["w2p"].shape[0]
    wa = mixed[:, 3 * width:3 * width + nl]
    gd = mixed[:, 3 * width + nl:]
    return dict(
        r=mixed[:, 0:width], k=mixed[:, width:2 * width], v=mixed[:, 2 * width:3 * width],
        lw=_dot(jnp.tanh(wa).astype(BF16), p["w2p"]),
        la=_dot(wa.astype(BF16), p["a2p"]),
        g=_dot(_sigmoid(gd).astype(BF16), p["g2"]))


def _pre_gates(t, p, ones_bd):
    w = -_softplus(-(p["w0"] + t["lw"])) - 0.5
    t["ld"] = -jnp.exp(w)
    t["a"] = _sigmoid(p["a0"] + t["la"])
    t["kk"] = t["k"] * p["k_k"]
    t["ss"] = _headsum(t["kk"] * t["kk"], ones_bd)


def _pre_keys(t, p):
    kk = t["kk"] * jnp.minimum(lax.rsqrt(t["ss"]), 1e12)
    t["k2"] = t["k"] * (1.0 + (t["a"] - 1.0) * p["k_a"])
    t["av"] = -kk
    t["bv"] = kk * t["a"]


def _rwkv_pre(cur, prev, p, ones_bd, width):
    t = _pre_lora(cur, prev, p, width)
    _pre_gates(t, p, ones_bd)
    _pre_keys(t, p)
    return t["r"], t["ld"], t["k2"], t["v"], t["av"], t["bv"], t["g"]


def _post_mean(y, ones_bd, head):
    return y - _headsum(y, ones_bd) * (1.0 / head)


def _post_stats(dlt, t, p, ones_bd, head):
    return _headsum(dlt * dlt, ones_bd) * (1.0 / head), _headsum(t["r"] * t["k2"] * p["r_k"], ones_bd)


def _post_out(dlt, var, rk, t, p, proj):
    yn = dlt * lax.rsqrt(var + GN_EPS) * p["ln_w"] + p["ln_b"]
    o = ((yn + rk * t["v"]) * t["g"]).astype(BF16)
    return _dot(o, proj)


def _rwkv_post(y, r, k2, v, g, p, ones_bd, proj, head):
    t = dict(r=r, k2=k2, v=v, g=g)
    dlt = _post_mean(y, ones_bd, head)
    var, rk = _post_stats(dlt, t, p, ones_bd, head)
    return _post_out(dlt, var, rk, t, p, proj)


def _interleave(main, other):
    out = []
    done = 0
    for i, item in enumerate(main):
        out.append(item)
        upto = (len(other) * (i + 1)) // len(main)
        out.extend(other[done:upto])
        done = upto
    return out


def _rwkv_seq_block(cur_ref, carry_scr, p, ones_bd, tril, proj_ref, s_ref, y_ref, out_ref, head, chunk, n_groups):
    nseq, tr, _ = cur_ref.shape
    width = y_ref.shape[1]
    nh = width // head
    c = chunk
    per_seq = tr // c
    gsz = nseq // n_groups
    steps = int(math.log2(c))
    hs = lambda h: slice(h * head, (h + 1) * head)
    ri = lax.broadcasted_iota(jnp.int32, (c, c), 0)
    ci = lax.broadcasted_iota(jnp.int32, (c, c), 1)
    strict = ri > ci
    incl = ri >= ci
    row = lax.broadcasted_iota(jnp.int32, cur_ref.shape[1:], 0)
    tok = {}
    fac, ar, sc, pw, nm, m_rb, lv, mv, vk, s0, am, ub, dlt, stats = ({} for _ in range(14))

    def seqs(gi):
        return range(gi * gsz, (gi + 1) * gsz)

    def chunks(gi):
        return [(q * per_seq + t, q, slice(((q - gi * gsz) * per_seq + t) * c, ((q - gi * gsz) * per_seq + t + 1) * c))
                for q in seqs(gi) for t in range(per_seq)]

    def pairs(gi):
        return [(j, q, h) for j, q, _ in chunks(gi) for h in range(nh)]

    def grows(gi):
        return slice(gi * gsz * tr, (gi + 1) * gsz * tr)

    def pre(gi):
        curs, prevs = [], []
        for q in seqs(gi):
            cq = cur_ref[q]
            prevs.append(jnp.where(row == 0, carry_scr[q], pltpu.roll(cq, 1, axis=0)))
            carry_scr[q] = cq[tr - 1:tr, :]
            curs.append(cq)
        tok[gi] = _pre_lora(jnp.concatenate(curs, axis=0), jnp.concatenate(prevs, axis=0), p, width)

    def gates(gi):
        _pre_gates(tok[gi], p, ones_bd)

    def keys(gi):
        t = tok[gi]
        _pre_keys(t, p)
        for j, _, rs in chunks(gi):
            lds = t["ld"][rs]
            hi, mid, lo = _split3(lds)
            fac[j] = dict(lds=lds, cum=_dot(tril, hi) + _dot(tril, mid) + _dot(tril, lo))

    def decay(gi):
        t = tok[gi]
        for j, _, rs in chunks(gi):
            f = fac[j]
            cum = f["cum"]
            g_inv = jnp.exp(-cum)
            cum_end = cum[c - 1:c, :]
            g_rel = jnp.exp(cum_end - cum)
            f.update(a_t=t["av"][rs] * jnp.exp(cum - f["lds"]), r_t=t["r"][rs] * jnp.exp(cum),
                     b_t=t["bv"][rs] * g_inv, k_t=t["k2"][rs] * g_inv,
                     b_e=(t["bv"][rs] * g_rel).astype(BF16), k_e=(t["k2"][rs] * g_rel).astype(BF16),
                     g_end=jnp.exp(cum_end), v=t["v"][rs].astype(BF16))

    def scores(gi):
        for j, _, h in pairs(gi):
            f = fac[j]
            ar[j, h] = jnp.concatenate([f["a_t"][:, hs(h)], f["r_t"][:, hs(h)]], axis=0).astype(BF16)
            bk = jnp.concatenate([f["b_t"][:, hs(h)], f["k_t"][:, hs(h)]], axis=0).astype(BF16)
            sc[j, h] = _dot_nt(ar[j, h], bk)

    def lowers(gi):
        for j, _, h in pairs(gi):
            x = sc[j, h]
            vh = fac[j]["v"][:, hs(h)]
            pw[j, h] = jnp.where(strict, x[:c, :c], 0.0)
            nm[j, h] = pw[j, h]
            m_rb[j, h] = jnp.where(incl, x[c:, :c], 0.0).astype(BF16)
            lv[j, h] = _dot(jnp.where(strict, x[:c, c:], 0.0).astype(BF16), vh)
            mv[j, h] = _dot(jnp.where(incl, x[c:, c:], 0.0).astype(BF16), vh)
            vk[j, h] = _dot_tn(vh, fac[j]["k_e"][:, hs(h)])

    def square(gi):
        for j, _, h in pairs(gi):
            pb = pw[j, h].astype(BF16)
            pw[j, h] = _dot(pb, pb)

    def extend(gi):
        for j, _, h in pairs(gi):
            nm[j, h] = nm[j, h] + pw[j, h] + _dot(nm[j, h].astype(BF16), pw[j, h].astype(BF16))

    def chain(t, part):
        def project(gi):
            for j, q, h in sel(gi):
                s0[j, h] = s_ref[q, h]
                am[j, h] = _dot_nt(ar[j, h], s0[j, h].astype(BF16))

        def solve(gi):
            for j, q, h in sel(gi):
                w = am[j, h][:c] + lv[j, h]
                ub[j, h] = (w + _dot(nm[j, h].astype(BF16), w.astype(BF16))).astype(BF16)

        def update(gi):
            for j, q, h in sel(gi):
                f = fac[j]
                r0 = (q * per_seq + t) * c
                y_ref[r0:r0 + c, hs(h)] = am[j, h][c:] + _dot(m_rb[j, h], ub[j, h]) + mv[j, h]
                s_ref[q, h] = (s0[j, h] * f["g_end"][:, hs(h)] + _dot_tn(ub[j, h], f["b_e"][:, hs(h)])
                               + vk[j, h])

        def sel(gi):
            return [(j, q, h) for j, q, h in pairs(gi) if j == q * per_seq + t]

        return dict(project=project, solve=solve, update=update)[part]

    def post_mean(gi):
        dlt[gi] = _post_mean(y_ref[grows(gi), :], ones_bd, head)

    def post_stats(gi):
        stats[gi] = _post_stats(dlt[gi], tok[gi], p, ones_bd, head)

    def post_out(gi):
        out = _post_out(dlt[gi], *stats[gi], tok[gi], p, proj_ref[...])
        for i, q in enumerate(seqs(gi)):
            out_ref[q] = out[i * tr:(i + 1) * tr]

    front = [pre, gates, keys, decay, scores, lowers]
    middle = [st for _ in range(steps - 1) for st in (square, extend)]
    middle += [chain(t, part) for t in range(per_seq) for part in ("project", "solve", "update")]
    middle += [post_mean, post_stats, post_out]
    order = [(st, 0) for st in front]
    for gi in range(n_groups):
        nxt = [(st, gi + 1) for st in front] if gi + 1 < n_groups else []
        order += _interleave([(st, gi) for st in middle], nxt)
    for st, gi in order:
        st(gi)


_RWKV_VEC_NAMES = ("w0", "a0", "k_k", "k_a", "r_k", "ln_w", "ln_b")
_RWKV_CONSTS = ("mu", "vecs", "w2p", "a2p", "g2")


def _rwkv_params(mu_ref, vecs_ref, w2p_ref, a2p_ref, g2_ref):
    p = {n: vecs_ref[i:i + 1, :] for i, n in enumerate(_RWKV_VEC_NAMES)}
    p.update(mu=mu_ref[...], w2p=w2p_ref[...], a2p=a2p_ref[...], g2=g2_ref[...])
    return p


def _rwkv_seq_kernel(cur_ref, shift0_ref, s0_ref, *rest, chunk, head):
    np_ = len(_RWKV_CONSTS)
    p = _rwkv_params(*rest[:np_])
    proj_ref, ones_ref, tril_ref = rest[np_:np_ + 3]
    out_ref, st_ref = rest[np_ + 3:np_ + 5]
    carry_scr, y_scr = rest[np_ + 5:]

    @pl.when(pl.program_id(1) == 0)
    def _():
        carry_scr[...] = shift0_ref[...]
        st_ref[...] = s0_ref[...]

    _rwkv_seq_block(cur_ref, carry_scr, p, ones_ref[...], tril_ref[...], proj_ref, st_ref, y_scr, out_ref,
                    head, chunk, WKV_GROUPS)


def _rwkv_rows_pre_kernel(cur_ref, prev_ref, mu_ref, vecs_ref, w2p_ref, a2p_ref, g2_ref, ones_ref,
                          r_ref, k_ref, v_ref, g_ref, t_ref, *, width):
    p = _rwkv_params(mu_ref, vecs_ref, w2p_ref, a2p_ref, g2_ref)
    r, ld, k2, v, a, b, g = _rwkv_pre(cur_ref[...], prev_ref[...], p, ones_ref[...], width)
    r_ref[...] = r
    k_ref[...] = k2
    v_ref[...] = v
    g_ref[...] = g
    for i, val in enumerate((r, ld, k2, v, a, b)):
        t_ref[i] = val.T


def _wkv_step_kernel(t_ref, s_ref, y_ref, so_ref):
    r = t_ref[0]
    dec = jnp.exp(t_ref[1])
    k = t_ref[2]
    a = t_ref[4]
    b = t_ref[5]

    def body(i, carry):
        s = s_ref[i]
        sa = jnp.sum(s * a, axis=0, keepdims=True)
        s2 = s * dec + sa * b + t_ref[3, pl.ds(i, 1), :] * k
        so_ref[i] = s2
        y_ref[pl.ds(i, 1), :] = jnp.sum(s2 * r, axis=0, keepdims=True)
        return carry

    lax.fori_loop(0, s_ref.shape[0], body, 0, unroll=V7X_SUBLANES)


def _rwkv_rows_post_kernel(yt_ref, r_ref, k_ref, v_ref, g_ref, mu_ref, vecs_ref, w2p_ref, a2p_ref, g2_ref,
                           proj_ref, ones_ref, o_ref, *, head):
    p = _rwkv_params(mu_ref, vecs_ref, w2p_ref, a2p_ref, g2_ref)
    o_ref[...] = _rwkv_post(yt_ref[...].T, r_ref[...], k_ref[...], v_ref[...], g_ref[...], p,
                            ones_ref[...], proj_ref[...], head)


def _rwkv_param_list(rw):
    return [rw[n] for n in _RWKV_CONSTS]


def _rwkv_seq_call(cur3, shift0, s0, rw, *, tr, nseq):
    nb, length, n_shift = cur3.shape
    nh, head = s0.shape[1], s0.shape[2]
    width = nh * head
    d = rw["proj"].shape[1]
    params = _rwkv_param_list(rw)
    chunk = min(WKV_CHUNK, tr)
    tril = (jnp.arange(chunk)[:, None] >= jnp.arange(chunk)[None, :]).astype(BF16)
    consts = params + [rw["proj"], rw["ones_bd"], tril]
    kern = functools.partial(_rwkv_seq_kernel, chunk=chunk, head=head)
    return pl.pallas_call(
        kern,
        out_shape=(jax.ShapeDtypeStruct((nb, length, d), F32),
                   jax.ShapeDtypeStruct((nb, nh, head, head), F32)),
        grid=(nb // nseq, length // tr),
        in_specs=[pl.BlockSpec((nseq, tr, n_shift), lambda q, i: (q, i, 0)),
                  pl.BlockSpec((nseq, 1, n_shift), lambda q, i: (q, 0, 0)),
                  pl.BlockSpec((nseq, nh, head, head), lambda q, i: (q, 0, 0, 0))]
                 + [pl.BlockSpec(c.shape, lambda q, i, nd=c.ndim: (0,) * nd) for c in consts],
        out_specs=(pl.BlockSpec((nseq, tr, d), lambda q, i: (q, i, 0)),
                   pl.BlockSpec((nseq, nh, head, head), lambda q, i: (q, 0, 0, 0))),
        scratch_shapes=[pltpu.VMEM((nseq, 1, n_shift), F32), pltpu.VMEM((nseq * tr, width), F32)],
        compiler_params=_cparams(("arbitrary", "arbitrary"), 40 * 2**20),
        name="rwkv_seq",
    )(cur3, shift0.reshape(nb, 1, n_shift), s0, *consts)


def _rwkv_rows_call(cur, prev, s0_t, rw):
    rows, n_shift = cur.shape
    nh, head = s0_t.shape[0], s0_t.shape[1]
    width = nh * head
    d = rw["proj"].shape[1]
    params = _rwkv_param_list(rw)
    full = lambda shape: pl.BlockSpec(shape, lambda i: (0,) * len(shape))
    vec = jax.ShapeDtypeStruct((rows, width), F32)
    n_t = 6

    pre_consts = params + [rw["ones_bd"]]
    r, k2, v, g, t = pl.pallas_call(
        functools.partial(_rwkv_rows_pre_kernel, width=width),
        out_shape=(vec,) * 4 + (jax.ShapeDtypeStruct((n_t, width, rows), F32),),
        grid=(1,),
        in_specs=[full(cur.shape), full(prev.shape)] + [full(c.shape) for c in pre_consts],
        out_specs=(full((rows, width)),) * 4 + (full((n_t, width, rows)),),
        compiler_params=_cparams(("arbitrary",), 32 * 2**20),
        name="rwkv_rows_pre",
    )(cur, prev, *pre_consts)

    sblk = pl.BlockSpec((None, head, head, rows), lambda h: (h, 0, 0, 0))
    yt, s1_t = pl.pallas_call(
        _wkv_step_kernel,
        out_shape=(jax.ShapeDtypeStruct((width, rows), F32), jax.ShapeDtypeStruct(s0_t.shape, F32)),
        grid=(nh,),
        in_specs=[pl.BlockSpec((n_t, head, rows), lambda h: (0, h, 0)), sblk],
        out_specs=(pl.BlockSpec((head, rows), lambda h: (h, 0)), sblk),
        compiler_params=_cparams(("parallel",), 32 * 2**20),
        name="wkv_step",
    )(t, s0_t)

    post_consts = params + [rw["proj"], rw["ones_bd"]]
    out = pl.pallas_call(
        functools.partial(_rwkv_rows_post_kernel, head=head),
        out_shape=jax.ShapeDtypeStruct((rows, d), F32),
        grid=(1,),
        in_specs=[full((width, rows))] + [full((rows, width))] * 4 + [full(c.shape) for c in post_consts],
        out_specs=full((rows, d)),
        compiler_params=_cparams(("arbitrary",), 32 * 2**20),
        name="rwkv_rows_post",
    )(yt, r, k2, v, g, *post_consts)
    return out, s1_t


def _merge_kernel(h_ref, gates_ref, ys5_ref, yrw_ref, gt2_ref, wout_ref,
                  sh_ref, sc_ref, gt_ref, g_ref, w1_ref, w3_ref, w2_ref, gfin_ref, o_ref, *, n_chunks,
                  tiles_per_seq):
    row = functools.partial(_mod_row, tiles_per_seq=tiles_per_seq)
    d = h_ref.shape[1]
    gt2, sh, sc, gt = row(gt2_ref), row(sh_ref), row(sc_ref), row(gt_ref)
    parts = _row_parts(h_ref.shape[0], ROW_PARTS)
    mixed = []
    for part in parts:
        m = (_sigmoid(gates_ref[part, :d]) * ys5_ref[part, :]
             + _sigmoid(gates_ref[part, d:]) * yrw_ref[part, :])
        mixed.append(_dot(m.astype(BF16), wout_ref[...]))
    for part, mo in zip(parts, mixed):
        h2 = h_ref[part, :] + _mod_part(gt2, part) * mo
        h3 = _ffn_core(h2, _mod_part(sh, part), _mod_part(sc, part), _mod_part(gt, part), g_ref[...],
                       w1_ref, w3_ref, w2_ref, n_chunks)
        o_ref[part, :] = _rms(h3, gfin_ref[...])


def _merge_call(h, gates, ys5, yrw, mods, ks, w_out, g, w1, w3, w2, g_fin, *, tm):
    rows, d = h.shape
    d_ff = w1.shape[1]
    kern = functools.partial(_merge_kernel, n_chunks=_ffn_chunks(d_ff), tiles_per_seq=mods.tiles_per_seq)
    row_spec = lambda w: pl.BlockSpec((tm, w), lambda i: (i, 0))
    weights = (3 * d * d_ff + d * d) * 2
    tiles = 2 * tm * 6 * d * 4 + 4 * tm * d_ff * 4 + 6 * tm * d * 4
    ys5_spec = row_spec(d) if mods.tiles_per_seq is None else _time_major(rows, d, tm, mods.tiles_per_seq)[1]
    return pl.pallas_call(
        kern,
        out_shape=jax.ShapeDtypeStruct((rows, d), F32),
        grid=(rows // tm,),
        in_specs=[row_spec(d), row_spec(2 * d), ys5_spec, row_spec(d),
                  mods.spec(ks[0]), _const_spec(w_out.shape),
                  mods.spec(ks[1]), mods.spec(ks[2]), mods.spec(ks[3]),
                  _const_spec((1, d)), _const_spec(w1.shape), _const_spec(w3.shape), _const_spec(w2.shape),
                  _const_spec((1, d))],
        out_specs=row_spec(d),
        compiler_params=_cparams(("parallel",), weights + tiles + 8 * 2**20),
        name="merge_ffn2",
    )(h, gates, ys5, yrw, mods.mod, w_out, mods.mod, mods.mod, mods.mod, g.reshape(1, d), w1, w3, w2,
      g_fin.reshape(1, d))


_LATE_WEIGHTS = ("ffn2_w1", "ffn2_w3", "ffn2_w2", "w_in", "w_out", "glu_v", "glu_g", "proj", "g2", "w2p", "a2p")


def _pick_tile(n, target):
    t = min(n, target)
    while n % t:
        t -= V7X_SUBLANES
    return t


def _layer(x, mod, mod_row0, s5_re0, s5_im0, wkv0, shift0, w, *, sequential):
    nb, length, d = x.shape
    rows = nb * length
    g, n = s5_re0.shape[1], s5_re0.shape[2]
    ns = g * n
    n_shift = shift0.shape[1]
    s5_width = w["s5"]["d"].shape[1]
    widths = (s5_width, n_shift, 2 * d)

    tm = _pick_tile(length if sequential else rows, 512)
    tm2 = _pick_tile(tm, 512)
    mods = _ModRows(mod, mod_row0, nb, tm, length // tm if sequential else None)
    mods2 = _ModRows(mod, mod_row0, nb, tm2, length // tm2 if sequential else None)

    x2 = x.reshape(rows, d)
    late = w["late"]
    to_cast = [n for n in _LATE_WEIGHTS if late[n].dtype != BF16]
    h1, cast = _ffn_call(x2, mods, (0, 1, 2), w["g_ffn1"], *w["ffn1"], tm=tm, casts=[late[n] for n in to_cast])
    late = dict(late, **dict(zip(to_cast, cast)))
    w = dict(w, late=late, w_in=late["w_in"], w_out=late["w_out"],
             ffn2=(late["ffn2_w1"], late["ffn2_w3"], late["ffn2_w2"]),
             s5=dict(w["s5"], glu_v=late["glu_v"], glu_g=late["glu_g"]),
             rwkv=dict(w["rwkv"], proj=late["proj"], g2=late["g2"], w2p=late["w2p"], a2p=late["a2p"]))
    s5_in, cur, gates = _proj_call(h1, mods, (3, 4), w["g_mix"], w["w_in"], widths, tm=tm)

    h0re = s5_re0.reshape(nb, ns)
    h0im = s5_im0.reshape(nb, ns)
    if sequential:
        ts = _pick_tile(length, S5_STEP_ROWS // nb)
        y_s5, hre, him = _s5_seq_call(s5_in, h0re, h0im, w["a_re"], w["a_im"], w["s5"], ts=ts)
        nseq = WKV_SEQS if nb % WKV_SEQS == 0 else 1
        tr = _pick_tile(length, WKV_BLOCK_CHUNKS * WKV_CHUNK // nseq)
        y_rw, wkv1 = _rwkv_seq_call(cur.reshape(nb, length, n_shift), shift0, wkv0, w["rwkv"], tr=tr, nseq=nseq)
        y_rw = y_rw.reshape(rows, d)
    else:
        y_s5, hre, him = _s5_rows_call(s5_in, h0re, h0im, w["a_re"], w["a_im"], w["s5"])
        y_rw, wkv1 = _rwkv_rows_call(cur, shift0, wkv0, w["rwkv"])

    y = _merge_call(h1, gates, y_s5, y_rw, mods2, (5, 6, 7, 8), w["w_out"],
                    w["g_ffn2"], *w["ffn2"], w["g_final"], tm=tm2)

    shift1 = cur.reshape(nb, length, n_shift)[:, -1]
    return (y.reshape(nb, length, d), hre.reshape(nb, g, n), him.reshape(nb, g, n), wkv1, shift1), w


def kernel(x_prompt, x_sample, c_prompt, c_sample, state_s5_re, state_s5_im, state_wkv, state_shift, w_ada, b_ada, g_ffn1, g_mix, g_ffn2, g_final, ffn1_w1, ffn1_w3, ffn1_w2, ffn2_w1, ffn2_w3, ffn2_w2, w_in, mu_shift, s5_lam_re, s5_lam_im, s5_log_dt, s5_b_re, s5_b_im, s5_c_re, s5_c_im, s5_d, s5_glu_v, s5_glu_g, rwkv_w0, rwkv_w2, rwkv_a0, rwkv_a2, rwkv_g2, rwkv_k_k, rwkv_k_a, rwkv_r_k, rwkv_ln_w, rwkv_ln_b, rwkv_proj, w_out):
    bp, _, d = x_prompt.shape
    bs = x_sample.shape[0]
    g, n = s5_lam_re.shape
    nh, head = rwkv_r_k.shape
    width = nh * head
    n_shift = mu_shift.shape[0]
    assert bs % bp == 0, "sample rows come first in the modulation array; prompt rows must stay block-aligned"
    bf = lambda t: t.astype(BF16)

    a_re, a_im, wb, wc_re, wc_im = _s5_prep_call(
        s5_lam_re, s5_lam_im, s5_log_dt, jnp.swapaxes(s5_b_re, 1, 2), jnp.swapaxes(s5_b_im, 1, 2),
        s5_c_re, s5_c_im, bp)
    w_lora = rwkv_w2.shape[0]
    a_lora = rwkv_a2.shape[0]
    w2p = jnp.concatenate([rwkv_w2, jnp.zeros((a_lora, width), F32)], axis=0)
    a2p = jnp.concatenate([jnp.zeros((w_lora, width), F32), rwkv_a2], axis=0)
    vec_rows = [rwkv_w0, rwkv_a0, rwkv_k_k, rwkv_k_a, rwkv_r_k.reshape(width), rwkv_ln_w, rwkv_ln_b]
    vecs = jnp.stack(vec_rows + [jnp.zeros_like(rwkv_w0)] * (V7X_SUBLANES - len(vec_rows)))
    rw = dict(mu=mu_shift.reshape(1, n_shift), vecs=vecs,
              ones_bd=jnp.kron(jnp.eye(nh, dtype=F32), jnp.ones((head, head), F32)).astype(BF16))
    late = dict(ffn2_w1=ffn2_w1, ffn2_w3=ffn2_w3, ffn2_w2=ffn2_w2, w_in=w_in, w_out=w_out,
                glu_v=s5_glu_v, glu_g=s5_glu_g, proj=rwkv_proj, g2=rwkv_g2, w2p=w2p, a2p=a2p)
    w = dict(
        g_ffn1=g_ffn1, g_mix=g_mix, g_ffn2=g_ffn2, g_final=g_final,
        ffn1=(bf(ffn1_w1), bf(ffn1_w3), bf(ffn1_w2)),
        a_re=a_re, a_im=a_im,
        s5=dict(wb=wb, wc_re=wc_re, wc_im=wc_im, d=s5_d.reshape(1, -1)),
        rwkv=rw, late=late,
    )

    mod = _mod_call(jnp.concatenate([c_sample, c_prompt], axis=0), w_ada, b_ada)

    z_s5 = jnp.zeros((bp, g, n), state_s5_re.dtype)
    z_wkv = jnp.zeros((bp, nh, head, head), state_wkv.dtype)
    z_shift = jnp.zeros((bp, n_shift), state_shift.dtype)
    (y_p, s5re_p, s5im_p, wkv_p, shift_p), w = _layer(x_prompt, mod, bs, z_s5, z_s5, z_wkv, z_shift, w,
                                                      sequential=True)
    (y_s, s5re_s, s5im_s, wkv_s_t, shift_s), _ = _layer(x_sample, mod, 0, state_s5_re, state_s5_im,
                                                        jnp.transpose(state_wkv, (1, 2, 3, 0)), state_shift, w,
                                                        sequential=False)
    wkv_s = jnp.transpose(wkv_s_t, (3, 0, 1, 2))
    return (y_p, y_s, s5re_p, s5im_p, wkv_p, shift_p, s5re_s, s5im_s, wkv_s, shift_s)
```

```python
import functools
import math

import jax
import jax.numpy as jnp
from jax import lax
from jax.experimental import pallas as pl
from jax.experimental.pallas import tpu as pltpu

F32 = jnp.float32
BF16 = jnp.bfloat16

NORM_EPS = 1e-6
GN_EPS = 64e-5
N_MOD = 9

V7X_LANES = 128
V7X_SUBLANES = 8
V7X_MXU_DIM = 256
V7X_VMEM_BYTES = 64 * 1024 * 1024
VMEM_CAP_BYTES = V7X_VMEM_BYTES - 8 * 1024 * 1024

ROW_PARTS = 2
S5_STEP_ROWS = 1024
WKV_CHUNK = 64
WKV_STEP_ROWS = 512


def _cparams(semantics, vmem_bytes):
    return pltpu.CompilerParams(
        dimension_semantics=semantics,
        vmem_limit_bytes=int(min(vmem_bytes, VMEM_CAP_BYTES)),
    )


def _const_spec(shape):
    nd = len(shape)
    return pl.BlockSpec(shape, lambda *_: (0,) * nd, pipeline_mode=pl.Buffered(1))


def _dot(a, b):
    return jnp.dot(a, b, preferred_element_type=F32)


def _dot_nt(a, b):
    return lax.dot_general(a, b, (((1,), (1,)), ((), ())), preferred_element_type=F32)


def _dot_tn(a, b):
    return lax.dot_general(a, b, (((0,), (0,)), ((), ())), preferred_element_type=F32)


def _split3(x):
    hi = x.astype(BF16)
    r1 = x - hi.astype(F32)
    mid = r1.astype(BF16)
    lo = (r1 - mid.astype(F32)).astype(BF16)
    return hi, mid, lo


def _rms(x, g):
    ms = jnp.mean(x * x, axis=-1, keepdims=True)
    return x * lax.rsqrt(ms + NORM_EPS) * g


def _softplus(x):
    return jnp.maximum(x, 0.0) + jnp.log(1.0 + jnp.exp(-jnp.abs(x)))


def _mod_kernel(c_ref, w_ref, b_ref, o_ref):
    c = c_ref[...]
    s = (c * jax.nn.sigmoid(c)).astype(BF16)
    o_ref[...] = _dot(s, w_ref[...].astype(BF16)) + b_ref[...]


def _mod_call(c, w_ada, b_ada):
    rows, d = c.shape
    n = w_ada.shape[1]
    return pl.pallas_call(
        _mod_kernel,
        out_shape=jax.ShapeDtypeStruct((n // d, rows, d), F32),
        grid=(n // d,),
        in_specs=[
            pl.BlockSpec((rows, d), lambda j: (0, 0)),
            pl.BlockSpec((d, d), lambda j: (0, j)),
            pl.BlockSpec((1, d), lambda j: (0, j)),
        ],
        out_specs=pl.BlockSpec((None, rows, d), lambda j: (j, 0, 0)),
        compiler_params=_cparams(("parallel",), 32 * 2**20),
        name="adaln_mod",
    )(c, w_ada, b_ada.reshape(1, n))


class _ModRows:
    def __init__(self, mod, row0, nrows, tm, tiles_per_seq):
        self.mod, self.row0, self.nrows, self.tm, self.tiles_per_seq = mod, row0, nrows, tm, tiles_per_seq

    def spec(self, k):
        d = self.mod.shape[2]
        if self.tiles_per_seq is None:
            blk0 = self.row0 // self.tm
            return pl.BlockSpec((None, self.tm, d), lambda i: (k, blk0 + i, 0))
        blk0 = self.row0 // self.nrows
        return pl.BlockSpec((None, self.nrows, d), lambda i: (k, blk0, 0))


def _mod_row(ref, tiles_per_seq):
    if tiles_per_seq is None:
        return ref[...]
    return ref[pl.ds(pl.program_id(0) // tiles_per_seq, 1), :]


def _ffn_core(x, sh, sc, gt, g, w1_ref, w3_ref, w2_ref, n_chunks):
    bounds = _ffn_bounds(w1_ref.shape[1], n_chunks)
    u = (_rms(x, g) * (1.0 + sc) + sh).astype(BF16)
    acc = None
    for lo, hi in zip(bounds[:-1], bounds[1:]):
        a = _dot(u, w1_ref[:, lo:hi])
        b = _dot(u, w3_ref[:, lo:hi])
        hm = (a * jax.nn.sigmoid(a) * b).astype(BF16)
        part = _dot(hm, w2_ref[lo:hi, :])
        acc = part if acc is None else acc + part
    return x + (0.5 * gt) * acc


def _row_parts(rows, n_parts):
    if rows % (n_parts * 2 * V7X_SUBLANES):
        return [slice(0, rows)]
    step = rows // n_parts
    return [slice(r * step, (r + 1) * step) for r in range(n_parts)]


def _mod_part(mod_rows, part):
    return mod_rows if mod_rows.shape[0] == 1 else mod_rows[part]


def _ffn_bounds(d_ff, n_chunks):
    tiles = -(-d_ff // V7X_MXU_DIM)
    return [min(d_ff, ((tiles * c + n_chunks - 1) // n_chunks) * V7X_MXU_DIM) for c in range(n_chunks + 1)]


def _ffn_kernel(x_ref, sh_ref, sc_ref, gt_ref, g_ref, w1_ref, w3_ref, w2_ref, *rest, n_chunks, tiles_per_seq):
    n_cast = len(rest) // 2
    o_ref = rest[n_cast]
    row = functools.partial(_mod_row, tiles_per_seq=tiles_per_seq)
    sh, sc, gt = row(sh_ref), row(sc_ref), row(gt_ref)
    for part in _row_parts(x_ref.shape[0], ROW_PARTS):
        o_ref[part, :] = _ffn_core(x_ref[part, :], _mod_part(sh, part), _mod_part(sc, part), _mod_part(gt, part),
                                   g_ref[...], w1_ref, w3_ref, w2_ref, n_chunks)
    for src, dst in zip(rest[:n_cast], rest[n_cast + 1:]):
        dst[...] = src[...].astype(BF16)


def _cast_blocks(arr, n_steps):
    bf16_rows = 2 * V7X_SUBLANES
    nblk = n_steps
    while arr.shape[0] % nblk or (arr.shape[0] // nblk) % bf16_rows:
        nblk //= 2
    per = n_steps // nblk
    return pl.BlockSpec((arr.shape[0] // nblk, arr.shape[1]), lambda i: (i // per, 0))


def _ffn_chunks(d_ff):
    return 2 if d_ff % (2 * V7X_LANES) == 0 else 1


def _ffn_call(x2, mods, ks, g, w1, w3, w2, *, tm, casts=()):
    rows, d = x2.shape
    d_ff = w1.shape[1]
    n_steps = rows // tm
    kern = functools.partial(_ffn_kernel, n_chunks=_ffn_chunks(d_ff), tiles_per_seq=mods.tiles_per_seq)
    weights = 3 * d * d_ff * 2
    tiles = 4 * tm * d * 4 + 4 * tm * d_ff * 4
    cast_specs = [_cast_blocks(a, n_steps) for a in casts]
    cast_bytes = sum(2 * 6 * s.block_shape[0] * s.block_shape[1] for s in cast_specs)
    out = pl.pallas_call(
        kern,
        out_shape=(jax.ShapeDtypeStruct((rows, d), F32),) + tuple(jax.ShapeDtypeStruct(a.shape, BF16) for a in casts),
        grid=(n_steps,),
        in_specs=[
            pl.BlockSpec((tm, d), lambda i: (i, 0)),
            mods.spec(ks[0]), mods.spec(ks[1]), mods.spec(ks[2]),
            _const_spec((1, d)),
            _const_spec(w1.shape), _const_spec(w3.shape), _const_spec(w2.shape),
        ] + cast_specs,
        out_specs=(pl.BlockSpec((tm, d), lambda i: (i, 0)),) + tuple(cast_specs),
        compiler_params=_cparams(("arbitrary",), weights + tiles + cast_bytes + 8 * 2**20),
        name="ffn1",
    )(x2, mods.mod, mods.mod, mods.mod, g.reshape(1, d), w1, w3, w2, *casts)
    return out[0], list(out[1:])


def _proj_kernel(h_ref, sh_ref, sc_ref, g_ref, win_ref, s5_ref, cur_ref, gates_ref, *, tiles_per_seq):
    row = functools.partial(_mod_row, tiles_per_seq=tiles_per_seq)
    u = (_rms(h_ref[...], g_ref[...]) * (1.0 + row(sc_ref)) + row(sh_ref)).astype(BF16)
    n0 = s5_ref.shape[1]
    n1 = n0 + cur_ref.shape[1]
    s5_ref[...] = _dot(u, win_ref[:, :n0])
    cur_ref[...] = _dot(u, win_ref[:, n0:n1])
    gates_ref[...] = _dot(u, win_ref[:, n1:])


def _time_major(rows, width, tm, tiles_per_seq):
    length = tiles_per_seq * tm
    spec = pl.BlockSpec((tm, width), lambda i: (i % tiles_per_seq, i // tiles_per_seq))
    return (length, rows // length * width), spec


def _proj_call(h, mods, ks, g, w_in, widths, *, tm):
    rows, d = h.shape
    n_in = w_in.shape[1]
    shapes = [(rows, w) for w in widths]
    specs = [pl.BlockSpec((tm, w), lambda i: (i, 0)) for w in widths]
    if mods.tiles_per_seq is not None:
        shapes[0], specs[0] = _time_major(rows, widths[0], tm, mods.tiles_per_seq)
    return pl.pallas_call(
        functools.partial(_proj_kernel, tiles_per_seq=mods.tiles_per_seq),
        out_shape=tuple(jax.ShapeDtypeStruct(s, F32) for s in shapes),
        grid=(rows // tm,),
        in_specs=[
            pl.BlockSpec((tm, d), lambda i: (i, 0)),
            mods.spec(ks[0]), mods.spec(ks[1]),
            _const_spec((1, d)),
            _const_spec(w_in.shape),
        ],
        out_specs=tuple(specs),
        compiler_params=_cparams(("parallel",), d * n_in * 2 + 3 * tm * (d + n_in) * 4 + 8 * 2**20),
        name="mix_proj",
    )(h, mods.mod, mods.mod, g.reshape(1, d), w_in)


def _s5_prep_kernel(lre_ref, lim_ref, ldt_ref, bre_ref, bim_ref, cre_ref, cim_ref,
                    are_ref, aim_ref, wb_ref, wcre_ref, wcim_ref, *, gpb):
    lre = lre_ref[...]
    lim = lim_ref[...]
    dt = jnp.exp(ldt_ref[...])
    mag = jnp.exp(lre * dt)
    ang = lim * dt
    a_re = mag * jnp.cos(ang)
    a_im = mag * jnp.sin(ang)
    den = lre * lre + lim * lim
    num_re = a_re - 1.0
    k_re = (num_re * lre + a_im * lim) / den
    k_im = (a_im * lre - num_re * lim) / den
    b_re = bre_ref[...]
    b_im = bim_ref[...]
    bb_re = (k_re * b_re - k_im * b_im).astype(BF16)
    bb_im = (k_re * b_im + k_im * b_re).astype(BF16)
    c_re = cre_ref[...].astype(BF16)
    c_im = (-cim_ref[...]).astype(BF16)
    g, c, n = b_re.shape
    nb = are_ref.shape[0]
    s = gpb * n
    wb_ref[...] = jnp.zeros(wb_ref.shape, BF16)
    wcre_ref[...] = jnp.zeros(wcre_ref.shape, BF16)
    wcim_ref[...] = jnp.zeros(wcim_ref.shape, BF16)
    for gi in range(g):
        k, gl = divmod(gi, gpb)
        rows = slice(gl * c, (gl + 1) * c)
        cols = slice(gl * n, (gl + 1) * n)
        are_ref[:, gi * n:(gi + 1) * n] = jnp.broadcast_to(a_re[gi], (nb, n))
        aim_ref[:, gi * n:(gi + 1) * n] = jnp.broadcast_to(a_im[gi], (nb, n))
        wb_ref[k, rows, cols] = bb_re[gi]
        wb_ref[k, rows, s + gl * n:s + (gl + 1) * n] = bb_im[gi]
        wcre_ref[k, rows, cols] = c_re[gi]
        wcim_ref[k, rows, cols] = c_im[gi]


def _s5_prep_call(lam_re, lam_im, log_dt, b_re_t, b_im_t, c_re, c_im, nb):
    g, n = lam_re.shape
    c = b_re_t.shape[1]
    gpb = max(1, min(g, V7X_MXU_DIM // c))
    nk = g // gpb
    return pl.pallas_call(
        functools.partial(_s5_prep_kernel, gpb=gpb),
        out_shape=(jax.ShapeDtypeStruct((nb, g * n), F32), jax.ShapeDtypeStruct((nb, g * n), F32),
                   jax.ShapeDtypeStruct((nk, gpb * c, 2 * gpb * n), BF16),
                   jax.ShapeDtypeStruct((nk, gpb * c, gpb * n), BF16),
                   jax.ShapeDtypeStruct((nk, gpb * c, gpb * n), BF16)),
        name="s5_discretise",
    )(lam_re.reshape(g, 1, n), lam_im.reshape(g, 1, n), log_dt.reshape(g, 1, 1), b_re_t, b_im_t, c_re, c_im)


def _s5_input_matmul(ub, wb_ref, bure_ref, buim_ref):
    nk, kin, two_s = wb_ref.shape
    s = two_s // 2
    for k in range(nk):
        res = _dot(ub[:, k * kin:(k + 1) * kin], wb_ref[k])
        bure_ref[:, k * s:(k + 1) * s] = res[:, :s]
        buim_ref[:, k * s:(k + 1) * s] = res[:, s:]


def _s5_output(xre, xim, u, wcre_ref, wcim_ref, d_row):
    nk, kout, s = wcre_ref.shape
    xre_b = xre.astype(BF16)
    xim_b = xim.astype(BF16)
    ys = []
    for k in range(nk):
        ys.append(_dot_nt(xre_b[:, k * s:(k + 1) * s], wcre_ref[k])
                  + _dot_nt(xim_b[:, k * s:(k + 1) * s], wcim_ref[k]))
    y = jnp.concatenate(ys, axis=-1) + d_row * u
    return jax.nn.gelu(y)


def _s5_glu(zb, gv_ref, gg_ref):
    return _dot(zb, gv_ref[...]) * jax.nn.sigmoid(_dot(zb, gg_ref[...]))


def _s5_seq_kernel(u_ref, h0re_ref, h0im_ref, are_ref, aim_ref, wb_ref, wcre_ref, wcim_ref, d_ref,
                   gv_ref, gg_ref,
                   o_ref, hre_ref, him_ref,
                   bure_scr, buim_scr, xre_scr, xim_scr, uslab_scr, oslab_scr, *, lane_chunk):
    nb = hre_ref.shape[0]
    ts = u_ref.shape[0]
    din = u_ref.shape[1] // nb
    dout = o_ref.shape[1] // nb

    @pl.when(pl.program_id(0) == 0)
    def _():
        hre_ref[...] = h0re_ref[...]
        him_ref[...] = h0im_ref[...]

    for b in range(nb):
        for j in range(din // V7X_LANES):
            col = b * din + j * V7X_LANES
            uslab_scr[j, pl.ds(b, ts, stride=nb), :] = u_ref[:, col:col + V7X_LANES]
    u_tm = jnp.concatenate([uslab_scr[j] for j in range(din // V7X_LANES)], axis=1)
    _s5_input_matmul(u_tm.astype(BF16), wb_ref, bure_scr, buim_scr)

    ns = are_ref.shape[1]
    for lc in range(ns // lane_chunk):
        sl = slice(lc * lane_chunk, (lc + 1) * lane_chunk)
        ar = are_ref[:, sl]
        ai = aim_ref[:, sl]
        sr = hre_ref[:, sl]
        si = him_ref[:, sl]
        for t in range(ts):
            rs = slice(t * nb, (t + 1) * nb)
            nr = ar * sr - ai * si + bure_scr[rs, sl]
            ni = ar * si + ai * sr + buim_scr[rs, sl]
            xre_scr[rs, sl] = nr
            xim_scr[rs, sl] = ni
            sr, si = nr, ni
        hre_ref[:, sl] = sr
        him_ref[:, sl] = si

    z_tm = _s5_output(xre_scr[...], xim_scr[...], u_tm, wcre_ref, wcim_ref, d_ref[...])
    out = _s5_glu(z_tm.astype(BF16), gv_ref, gg_ref)
    for j in range(dout // V7X_LANES):
        oslab_scr[j] = out[:, j * V7X_LANES:(j + 1) * V7X_LANES]
    for b in range(nb):
        for j in range(dout // V7X_LANES):
            col = b * dout + j * V7X_LANES
            o_ref[:, col:col + V7X_LANES] = oslab_scr[j, pl.ds(b, ts, stride=nb), :]


def _s5_rows_kernel(u_ref, h0re_ref, h0im_ref, are_ref, aim_ref, wb_ref, wcre_ref, wcim_ref, d_ref,
                    gv_ref, gg_ref, o_ref, hre_ref, him_ref, bure_scr, buim_scr):
    u = u_ref[...]
    _s5_input_matmul(u.astype(BF16), wb_ref, bure_scr, buim_scr)
    ar = are_ref[0:1, :]
    ai = aim_ref[0:1, :]
    sr = h0re_ref[...]
    si = h0im_ref[...]
    nr = ar * sr - ai * si + bure_scr[...]
    ni = ar * si + ai * sr + buim_scr[...]
    hre_ref[...] = nr
    him_ref[...] = ni
    z = _s5_output(nr, ni, u, wcre_ref, wcim_ref, d_ref[...])
    o_ref[...] = _s5_glu(z.astype(BF16), gv_ref, gg_ref)


def _s5_weight_specs(ws):
    return [_const_spec(w.shape) for w in ws]


def _s5_seq_call(s5_in, h0re, h0im, a_re, a_im, sw, *, ts):
    nb, ns = a_re.shape
    length = s5_in.shape[0]
    din = s5_in.shape[1] // nb
    d = sw["glu_v"].shape[1]
    rows = nb * ts
    ws = [sw["wb"], sw["wc_re"], sw["wc_im"], sw["d"], sw["glu_v"], sw["glu_g"]]
    kern = functools.partial(_s5_seq_kernel, lane_chunk=4 * V7X_LANES)
    return pl.pallas_call(
        kern,
        out_shape=(jax.ShapeDtypeStruct((length, nb * d), F32),
                   jax.ShapeDtypeStruct((nb, ns), F32), jax.ShapeDtypeStruct((nb, ns), F32)),
        grid=(length // ts,),
        in_specs=[pl.BlockSpec((ts, nb * din), lambda c: (c, 0)),
                  _const_spec((nb, ns)), _const_spec((nb, ns)), _const_spec((nb, ns)), _const_spec((nb, ns))]
                 + _s5_weight_specs(ws),
        out_specs=(pl.BlockSpec((ts, nb * d), lambda c: (c, 0)),
                   pl.BlockSpec((nb, ns), lambda c: (0, 0)), pl.BlockSpec((nb, ns), lambda c: (0, 0))),
        scratch_shapes=[pltpu.VMEM((rows, ns), F32) for _ in range(4)]
                       + [pltpu.VMEM((din // V7X_LANES, rows, V7X_LANES), F32),
                          pltpu.VMEM((d // V7X_LANES, rows, V7X_LANES), F32)],
        compiler_params=_cparams(("arbitrary",), 48 * 2**20),
        name="s5_seq",
    )(s5_in, h0re, h0im, a_re, a_im, *ws)


def _s5_rows_call(s5_in, h0re, h0im, a_re, a_im, sw):
    rows, din = s5_in.shape
    ns = a_re.shape[1]
    d = sw["glu_v"].shape[1]
    ws = [sw["wb"], sw["wc_re"], sw["wc_im"], sw["d"], sw["glu_v"], sw["glu_g"]]
    full = lambda shape: pl.BlockSpec(shape, lambda i: (0,) * len(shape))
    return pl.pallas_call(
        _s5_rows_kernel,
        out_shape=(jax.ShapeDtypeStruct((rows, d), F32),
                   jax.ShapeDtypeStruct((rows, ns), F32), jax.ShapeDtypeStruct((rows, ns), F32)),
        grid=(1,),
        in_specs=[full((rows, din)), full((rows, ns)), full((rows, ns)), full(a_re.shape), full(a_im.shape)]
                 + [full(w.shape) for w in ws],
        out_specs=(full((rows, d)), full((rows, ns)), full((rows, ns))),
        scratch_shapes=[pltpu.VMEM((rows, ns), F32) for _ in range(2)],
        compiler_params=_cparams(("arbitrary",), 40 * 2**20),
        name="s5_rows",
    )(s5_in, h0re, h0im, a_re, a_im, *ws)


def _headsum(x, ones_bd):
    hi = x.astype(BF16)
    lo = (x - hi.astype(F32)).astype(BF16)
    return _dot(hi, ones_bd) + _dot(lo, ones_bd)


def _pre_lora(cur, prev, p, width):
    mixed = cur + p["mu"] * (prev - cur)
    nl = p["w2p"].shape[0]
    wa = mixed[:, 3 * width:3 * width + nl]
    gd = mixed[:, 3 * width + nl:]
    return dict(
        r=mixed[:, 0:width], k=mixed[:, width:2 * width], v=mixed[:, 2 * width:3 * width],
        lw=_dot(jnp.tanh(wa).astype(BF16), p["w2p"]),
        la=_dot(wa.astype(BF16), p["a2p"]),
        g=_dot(jax.nn.sigmoid(gd).astype(BF16), p["g2"]))


def _pre_gates(t, p, ones_bd):
    w = -_softplus(-(p["w0"] + t["lw"])) - 0.5
    t["ld"] = -jnp.exp(w)
    t["a"] = jax.nn.sigmoid(p["a0"] + t["la"])
    t["kk"] = t["k"] * p["k_k"]
    t["ss"] = _headsum(t["kk"] * t["kk"], ones_bd)


def _pre_keys(t, p):
    kk = t["kk"] * jnp.minimum(lax.rsqrt(t["ss"]), 1e12)
    t["k2"] = t["k"] * (1.0 + (t["a"] - 1.0) * p["k_a"])
    t["av"] = -kk
    t["bv"] = kk * t["a"]


def _rwkv_pre(cur, prev, p, ones_bd, width):
    t = _pre_lora(cur, prev, p, width)
    _pre_gates(t, p, ones_bd)
    _pre_keys(t, p)
    return t


def _rwkv_post(y, t, p, ones_bd, proj, head):
    inv = 1.0 / head
    dlt = y - _headsum(y, ones_bd) * inv
    var = _headsum(dlt * dlt, ones_bd) * inv
    bonus = _headsum(t["r"] * t["k2"] * p["r_k"], ones_bd) * t["v"]
    yn = dlt * lax.rsqrt(var + GN_EPS) * p["ln_w"] + p["ln_b"]
    return _dot(((yn + bonus) * t["g"]).astype(BF16), proj)


def _wkv_chunks(t, s_ref, y_ref, tril, head, chunk, nseq):
    rows, width = t["r"].shape
    nh = width // head
    c = chunk
    per_seq = rows // (nseq * c)
    chunks = [(q * per_seq + i, q) for q in range(nseq) for i in range(per_seq)]
    pairs = [(j, q, h) for j, q in chunks for h in range(nh)]
    hs = lambda h: slice(h * head, (h + 1) * head)
    ri = lax.broadcasted_iota(jnp.int32, (c, c), 0)
    ci = lax.broadcasted_iota(jnp.int32, (c, c), 1)
    strict = ri > ci
    incl = ri >= ci

    fac = {}
    for j, _ in chunks:
        rs = slice(j * c, (j + 1) * c)
        lds = t["ld"][rs]
        hi, mid, lo = _split3(lds)
        fac[j] = dict(lds=lds, rs=rs, cum=_dot(tril, hi) + _dot(tril, mid) + _dot(tril, lo))
    for j, _ in chunks:
        f = fac[j]
        cum, rs = f["cum"], f["rs"]
        g_inv = jnp.exp(-cum)
        cum_end = cum[c - 1:c, :]
        g_rel = jnp.exp(cum_end - cum)
        f.update(a_t=t["av"][rs] * jnp.exp(cum - f["lds"]), r_t=t["r"][rs] * jnp.exp(cum),
                 b_t=t["bv"][rs] * g_inv, k_t=t["k2"][rs] * g_inv,
                 b_e=(t["bv"][rs] * g_rel).astype(BF16), k_e=(t["k2"][rs] * g_rel).astype(BF16),
                 g_end=jnp.exp(cum_end), v=t["v"][rs].astype(BF16))

    ar, sc, pw, nm, m_rb, lv, mv, vk = ({} for _ in range(8))
    for j, _, h in pairs:
        f = fac[j]
        ar[j, h] = jnp.concatenate([f["a_t"][:, hs(h)], f["r_t"][:, hs(h)]], axis=0).astype(BF16)
        bk = jnp.concatenate([f["b_t"][:, hs(h)], f["k_t"][:, hs(h)]], axis=0).astype(BF16)
        sc[j, h] = _dot_nt(ar[j, h], bk)
    for j, _, h in pairs:
        x = sc[j, h]
        vh = fac[j]["v"][:, hs(h)]
        pw[j, h] = jnp.where(strict, x[:c, :c], 0.0)
        nm[j, h] = pw[j, h]
        m_rb[j, h] = jnp.where(incl, x[c:, :c], 0.0).astype(BF16)
        lv[j, h] = _dot(jnp.where(strict, x[:c, c:], 0.0).astype(BF16), vh)
        mv[j, h] = _dot(jnp.where(incl, x[c:, c:], 0.0).astype(BF16), vh)
        vk[j, h] = _dot_tn(vh, fac[j]["k_e"][:, hs(h)])
    for _ in range(int(math.log2(c)) - 1):
        for j, _, h in pairs:
            pb = pw[j, h].astype(BF16)
            pw[j, h] = _dot(pb, pb)
        for j, _, h in pairs:
            nm[j, h] = nm[j, h] + pw[j, h] + _dot(nm[j, h].astype(BF16), pw[j, h].astype(BF16))

    for i in range(per_seq):
        now = [(j, q, h) for j, q, h in pairs if j == q * per_seq + i]
        s0, am, ub = {}, {}, {}
        for j, q, h in now:
            s0[j, h] = s_ref[q, h]
            am[j, h] = _dot_nt(ar[j, h], s0[j, h].astype(BF16))
        for j, q, h in now:
            w = am[j, h][:c] + lv[j, h]
            ub[j, h] = (w + _dot(nm[j, h].astype(BF16), w.astype(BF16))).astype(BF16)
        for j, q, h in now:
            f = fac[j]
            y_ref[f["rs"], hs(h)] = am[j, h][c:] + _dot(m_rb[j, h], ub[j, h]) + mv[j, h]
            s_ref[q, h] = (s0[j, h] * f["g_end"][:, hs(h)] + _dot_tn(ub[j, h], f["b_e"][:, hs(h)])
                           + vk[j, h])


_RWKV_VEC_NAMES = ("w0", "a0", "k_k", "k_a", "r_k", "ln_w", "ln_b")
_RWKV_CONSTS = ("mu", "vecs", "w2p", "a2p", "g2")


def _rwkv_params(mu_ref, vecs_ref, w2p_ref, a2p_ref, g2_ref):
    p = {n: vecs_ref[i:i + 1, :] for i, n in enumerate(_RWKV_VEC_NAMES)}
    p.update(mu=mu_ref[...], w2p=w2p_ref[...], a2p=a2p_ref[...], g2=g2_ref[...])
    return p


def _rwkv_seq_kernel(cur_ref, shift0_ref, s0_ref, *rest, chunk, head):
    np_ = len(_RWKV_CONSTS)
    p = _rwkv_params(*rest[:np_])
    proj_ref, ones_ref, tril_ref = rest[np_:np_ + 3]
    out_ref, st_ref = rest[np_ + 3:np_ + 5]
    carry_scr, y_scr = rest[np_ + 5:]

    @pl.when(pl.program_id(0) == 0)
    def _():
        carry_scr[...] = shift0_ref[...]
        st_ref[...] = s0_ref[...]

    nseq, tr, _ = cur_ref.shape
    width = y_scr.shape[1]
    row = lax.broadcasted_iota(jnp.int32, cur_ref.shape[1:], 0)
    curs, prevs = [], []
    for q in range(nseq):
        cq = cur_ref[q]
        prevs.append(jnp.where(row == 0, carry_scr[q], pltpu.roll(cq, 1, axis=0)))
        carry_scr[q] = cq[tr - 1:tr, :]
        curs.append(cq)
    ones_bd = ones_ref[...]
    t = _rwkv_pre(jnp.concatenate(curs, axis=0), jnp.concatenate(prevs, axis=0), p, ones_bd, width)
    _wkv_chunks(t, st_ref, y_scr, tril_ref[...], head, chunk, nseq)
    out = _rwkv_post(y_scr[...], t, p, ones_bd, proj_ref[...], head)
    for q in range(nseq):
        out_ref[q] = out[q * tr:(q + 1) * tr]


def _rwkv_rows_pre_kernel(cur_ref, prev_ref, mu_ref, vecs_ref, w2p_ref, a2p_ref, g2_ref, ones_ref,
                          r_ref, k_ref, v_ref, g_ref, t_ref, *, width):
    p = _rwkv_params(mu_ref, vecs_ref, w2p_ref, a2p_ref, g2_ref)
    t = _rwkv_pre(cur_ref[...], prev_ref[...], p, ones_ref[...], width)
    r_ref[...] = t["r"]
    k_ref[...] = t["k2"]
    v_ref[...] = t["v"]
    g_ref[...] = t["g"]
    for i, n in enumerate(("r", "ld", "k2", "v", "av", "bv")):
        t_ref[i] = t[n].T


def _wkv_step_kernel(t_ref, s_ref, y_ref, so_ref):
    r = t_ref[0]
    dec = jnp.exp(t_ref[1])
    k = t_ref[2]
    a = t_ref[4]
    b = t_ref[5]

    def body(i, carry):
        s = s_ref[i]
        sa = jnp.sum(s * a, axis=0, keepdims=True)
        s2 = s * dec + sa * b + t_ref[3, pl.ds(i, 1), :] * k
        so_ref[i] = s2
        y_ref[pl.ds(i, 1), :] = jnp.sum(s2 * r, axis=0, keepdims=True)
        return carry

    lax.fori_loop(0, s_ref.shape[0], body, 0, unroll=V7X_SUBLANES)


def _rwkv_rows_post_kernel(yt_ref, r_ref, k_ref, v_ref, g_ref, mu_ref, vecs_ref, w2p_ref, a2p_ref, g2_ref,
                           proj_ref, ones_ref, o_ref, *, head):
    p = _rwkv_params(mu_ref, vecs_ref, w2p_ref, a2p_ref, g2_ref)
    t = dict(r=r_ref[...], k2=k_ref[...], v=v_ref[...], g=g_ref[...])
    o_ref[...] = _rwkv_post(yt_ref[...].T, t, p, ones_ref[...], proj_ref[...], head)


def _rwkv_param_list(rw):
    return [rw[n] for n in _RWKV_CONSTS]


def _rwkv_seq_call(cur3, shift0, s0, rw, *, tr):
    nb, length, n_shift = cur3.shape
    nh, head = s0.shape[1], s0.shape[2]
    width = nh * head
    d = rw["proj"].shape[1]
    chunk = min(WKV_CHUNK, tr)
    tril = (jnp.arange(chunk)[:, None] >= jnp.arange(chunk)[None, :]).astype(BF16)
    consts = _rwkv_param_list(rw) + [rw["proj"], rw["ones_bd"], tril]
    whole = lambda c: pl.BlockSpec(c.shape, lambda i, nd=c.ndim: (0,) * nd)
    shift3 = shift0.reshape(nb, 1, n_shift)
    kern = functools.partial(_rwkv_seq_kernel, chunk=chunk, head=head)
    return pl.pallas_call(
        kern,
        out_shape=(jax.ShapeDtypeStruct((nb, length, d), F32),
                   jax.ShapeDtypeStruct((nb, nh, head, head), F32)),
        grid=(length // tr,),
        in_specs=[pl.BlockSpec((nb, tr, n_shift), lambda i: (0, i, 0)), whole(shift3), whole(s0)]
                 + [whole(c) for c in consts],
        out_specs=(pl.BlockSpec((nb, tr, d), lambda i: (0, i, 0)), whole(s0)),
        scratch_shapes=[pltpu.VMEM((nb, 1, n_shift), F32), pltpu.VMEM((nb * tr, width), F32)],
        compiler_params=_cparams(("arbitrary",), 40 * 2**20),
        name="rwkv_seq",
    )(cur3, shift3, s0, *consts)


def _rwkv_rows_call(cur, prev, s0_t, rw):
    rows, n_shift = cur.shape
    nh, head = s0_t.shape[0], s0_t.shape[1]
    width = nh * head
    d = rw["proj"].shape[1]
    params = _rwkv_param_list(rw)
    full = lambda shape: pl.BlockSpec(shape, lambda i: (0,) * len(shape))
    vec = jax.ShapeDtypeStruct((rows, width), F32)
    n_t = 6

    pre_consts = params + [rw["ones_bd"]]
    r, k2, v, g, t = pl.pallas_call(
        functools.partial(_rwkv_rows_pre_kernel, width=width),
        out_shape=(vec,) * 4 + (jax.ShapeDtypeStruct((n_t, width, rows), F32),),
        grid=(1,),
        in_specs=[full(cur.shape), full(prev.shape)] + [full(c.shape) for c in pre_consts],
        out_specs=(full((rows, width)),) * 4 + (full((n_t, width, rows)),),
        compiler_params=_cparams(("arbitrary",), 32 * 2**20),
        name="rwkv_rows_pre",
    )(cur, prev, *pre_consts)

    sblk = pl.BlockSpec((None, head, head, rows), lambda h: (h, 0, 0, 0))
    yt, s1_t = pl.pallas_call(
        _wkv_step_kernel,
        out_shape=(jax.ShapeDtypeStruct((width, rows), F32), jax.ShapeDtypeStruct(s0_t.shape, F32)),
        grid=(nh,),
        in_specs=[pl.BlockSpec((n_t, head, rows), lambda h: (0, h, 0)), sblk],
        out_specs=(pl.BlockSpec((head, rows), lambda h: (h, 0)), sblk),
        compiler_params=_cparams(("parallel",), 32 * 2**20),
        name="wkv_step",
    )(t, s0_t)

    post_consts = params + [rw["proj"], rw["ones_bd"]]
    out = pl.pallas_call(
        functools.partial(_rwkv_rows_post_kernel, head=head),
        out_shape=jax.ShapeDtypeStruct((rows, d), F32),
        grid=(1,),
        in_specs=[full((width, rows))] + [full((rows, width))] * 4 + [full(c.shape) for c in post_consts],
        out_specs=full((rows, d)),
        compiler_params=_cparams(("arbitrary",), 32 * 2**20),
        name="rwkv_rows_post",
    )(yt, r, k2, v, g, *post_consts)
    return out, s1_t


def _merge_kernel(h_ref, gates_ref, ys5_ref, yrw_ref, gt2_ref, wout_ref,
                  sh_ref, sc_ref, gt_ref, g_ref, w1_ref, w3_ref, w2_ref, gfin_ref, o_ref, *, n_chunks,
                  tiles_per_seq):
    row = functools.partial(_mod_row, tiles_per_seq=tiles_per_seq)
    d = h_ref.shape[1]
    gt2, sh, sc, gt = row(gt2_ref), row(sh_ref), row(sc_ref), row(gt_ref)
    parts = _row_parts(h_ref.shape[0], ROW_PARTS)
    mixed = []
    for part in parts:
        m = (jax.nn.sigmoid(gates_ref[part, :d]) * ys5_ref[part, :]
             + jax.nn.sigmoid(gates_ref[part, d:]) * yrw_ref[part, :])
        mixed.append(_dot(m.astype(BF16), wout_ref[...]))
    for part, mo in zip(parts, mixed):
        h2 = h_ref[part, :] + _mod_part(gt2, part) * mo
        h3 = _ffn_core(h2, _mod_part(sh, part), _mod_part(sc, part), _mod_part(gt, part), g_ref[...],
                       w1_ref, w3_ref, w2_ref, n_chunks)
        o_ref[part, :] = _rms(h3, gfin_ref[...])


def _merge_call(h, gates, ys5, yrw, mods, ks, w_out, g, w1, w3, w2, g_fin, *, tm):
    rows, d = h.shape
    d_ff = w1.shape[1]
    kern = functools.partial(_merge_kernel, n_chunks=_ffn_chunks(d_ff), tiles_per_seq=mods.tiles_per_seq)
    row_spec = lambda w: pl.BlockSpec((tm, w), lambda i: (i, 0))
    weights = (3 * d * d_ff + d * d) * 2
    tiles = 2 * tm * 6 * d * 4 + 4 * tm * d_ff * 4 + 6 * tm * d * 4
    ys5_spec = row_spec(d) if mods.tiles_per_seq is None else _time_major(rows, d, tm, mods.tiles_per_seq)[1]
    return pl.pallas_call(
        kern,
        out_shape=jax.ShapeDtypeStruct((rows, d), F32),
        grid=(rows // tm,),
        in_specs=[row_spec(d), row_spec(2 * d), ys5_spec, row_spec(d),
                  mods.spec(ks[0]), _const_spec(w_out.shape),
                  mods.spec(ks[1]), mods.spec(ks[2]), mods.spec(ks[3]),
                  _const_spec((1, d)), _const_spec(w1.shape), _const_spec(w3.shape), _const_spec(w2.shape),
                  _const_spec((1, d))],
        out_specs=row_spec(d),
        compiler_params=_cparams(("parallel",), weights + tiles + 8 * 2**20),
        name="merge_ffn2",
    )(h, gates, ys5, yrw, mods.mod, w_out, mods.mod, mods.mod, mods.mod, g.reshape(1, d), w1, w3, w2,
      g_fin.reshape(1, d))


_LATE_WEIGHTS = ("ffn2_w1", "ffn2_w3", "ffn2_w2", "w_in", "w_out", "glu_v", "glu_g", "proj", "g2", "w2p", "a2p")


def _pick_tile(n, target):
    t = min(n, target)
    while n % t:
        t -= V7X_SUBLANES
    return t


def _layer(x, mod, mod_row0, s5_re0, s5_im0, wkv0, shift0, w, *, sequential):
    nb, length, d = x.shape
    rows = nb * length
    g, n = s5_re0.shape[1], s5_re0.shape[2]
    ns = g * n
    n_shift = shift0.shape[1]
    s5_width = w["s5"]["d"].shape[1]
    widths = (s5_width, n_shift, 2 * d)

    tm = _pick_tile(length if sequential else rows, 512)
    mods = _ModRows(mod, mod_row0, nb, tm, length // tm if sequential else None)

    x2 = x.reshape(rows, d)
    late = w["late"]
    to_cast = [n for n in _LATE_WEIGHTS if late[n].dtype != BF16]
    h1, cast = _ffn_call(x2, mods, (0, 1, 2), w["g_ffn1"], *w["ffn1"], tm=tm, casts=[late[n] for n in to_cast])
    late = dict(late, **dict(zip(to_cast, cast)))
    w = dict(w, late=late, w_in=late["w_in"], w_out=late["w_out"],
             ffn2=(late["ffn2_w1"], late["ffn2_w3"], late["ffn2_w2"]),
             s5=dict(w["s5"], glu_v=late["glu_v"], glu_g=late["glu_g"]),
             rwkv=dict(w["rwkv"], proj=late["proj"], g2=late["g2"], w2p=late["w2p"], a2p=late["a2p"]))
    s5_in, cur, gates = _proj_call(h1, mods, (3, 4), w["g_mix"], w["w_in"], widths, tm=tm)

    h0re = s5_re0.reshape(nb, ns)
    h0im = s5_im0.reshape(nb, ns)
    if sequential:
        ts = _pick_tile(length, S5_STEP_ROWS // nb)
        y_s5, hre, him = _s5_seq_call(s5_in, h0re, h0im, w["a_re"], w["a_im"], w["s5"], ts=ts)
        tr = _pick_tile(length, max(WKV_CHUNK, WKV_STEP_ROWS // nb))
        y_rw, wkv1 = _rwkv_seq_call(cur.reshape(nb, length, n_shift), shift0, wkv0, w["rwkv"], tr=tr)
        y_rw = y_rw.reshape(rows, d)
    else:
        y_s5, hre, him = _s5_rows_call(s5_in, h0re, h0im, w["a_re"], w["a_im"], w["s5"])
        y_rw, wkv1 = _rwkv_rows_call(cur, shift0, wkv0, w["rwkv"])

    y = _merge_call(h1, gates, y_s5, y_rw, mods, (5, 6, 7, 8), w["w_out"],
                    w["g_ffn2"], *w["ffn2"], w["g_final"], tm=tm)

    shift1 = cur.reshape(nb, length, n_shift)[:, -1]
    return (y.reshape(nb, length, d), hre.reshape(nb, g, n), him.reshape(nb, g, n), wkv1, shift1), w


def kernel(x_prompt, x_sample, c_prompt, c_sample, state_s5_re, state_s5_im, state_wkv, state_shift, w_ada, b_ada, g_ffn1, g_mix, g_ffn2, g_final, ffn1_w1, ffn1_w3, ffn1_w2, ffn2_w1, ffn2_w3, ffn2_w2, w_in, mu_shift, s5_lam_re, s5_lam_im, s5_log_dt, s5_b_re, s5_b_im, s5_c_re, s5_c_im, s5_d, s5_glu_v, s5_glu_g, rwkv_w0, rwkv_w2, rwkv_a0, rwkv_a2, rwkv_g2, rwkv_k_k, rwkv_k_a, rwkv_r_k, rwkv_ln_w, rwkv_ln_b, rwkv_proj, w_out):
    bp, _, d = x_prompt.shape
    bs = x_sample.shape[0]
    g, n = s5_lam_re.shape
    nh, head = rwkv_r_k.shape
    width = nh * head
    n_shift = mu_shift.shape[0]
    assert bs % bp == 0, "sample rows come first in the modulation array; prompt rows must stay block-aligned"
    bf = lambda t: t.astype(BF16)

    a_re, a_im, wb, wc_re, wc_im = _s5_prep_call(
        s5_lam_re, s5_lam_im, s5_log_dt, jnp.swapaxes(s5_b_re, 1, 2), jnp.swapaxes(s5_b_im, 1, 2),
        s5_c_re, s5_c_im, bp)
    w_lora = rwkv_w2.shape[0]
    a_lora = rwkv_a2.shape[0]
    w2p = jnp.concatenate([rwkv_w2, jnp.zeros((a_lora, width), F32)], axis=0)
    a2p = jnp.concatenate([jnp.zeros((w_lora, width), F32), rwkv_a2], axis=0)
    vec_rows = [rwkv_w0, rwkv_a0, rwkv_k_k, rwkv_k_a, rwkv_r_k.reshape(width), rwkv_ln_w, rwkv_ln_b]
    vecs = jnp.stack(vec_rows + [jnp.zeros_like(rwkv_w0)] * (V7X_SUBLANES - len(vec_rows)))
    rw = dict(mu=mu_shift.reshape(1, n_shift), vecs=vecs,
              ones_bd=jnp.kron(jnp.eye(nh, dtype=F32), jnp.ones((head, head), F32)).astype(BF16))
    late = dict(ffn2_w1=ffn2_w1, ffn2_w3=ffn2_w3, ffn2_w2=ffn2_w2, w_in=w_in, w_out=w_out,
                glu_v=s5_glu_v, glu_g=s5_glu_g, proj=rwkv_proj, g2=rwkv_g2, w2p=w2p, a2p=a2p)
    w = dict(
        g_ffn1=g_ffn1, g_mix=g_mix, g_ffn2=g_ffn2, g_final=g_final,
        ffn1=(bf(ffn1_w1), bf(ffn1_w3), bf(ffn1_w2)),
        a_re=a_re, a_im=a_im,
        s5=dict(wb=wb, wc_re=wc_re, wc_im=wc_im, d=s5_d.reshape(1, -1)),
        rwkv=rw, late=late,
    )

    mod = _mod_call(jnp.concatenate([c_sample, c_prompt], axis=0), w_ada, b_ada)

    z_s5 = jnp.zeros((bp, g, n), state_s5_re.dtype)
    z_wkv = jnp.zeros((bp, nh, head, head), state_wkv.dtype)
    z_shift = jnp.zeros((bp, n_shift), state_shift.dtype)
    (y_p, s5re_p, s5im_p, wkv_p, shift_p), w = _layer(x_prompt, mod, bs, z_s5, z_s5, z_wkv, z_shift, w,
                                                      sequential=True)
    (y_s, s5re_s, s5im_s, wkv_s_t, shift_s), _ = _layer(x_sample, mod, 0, state_s5_re, state_s5_im,
                                                        jnp.transpose(state_wkv, (1, 2, 3, 0)), state_shift, w,
                                                        sequential=False)
    wkv_s = jnp.transpose(wkv_s_t, (3, 0, 1, 2))
    return (y_p, y_s, s5re_p, s5im_p, wkv_p, shift_p, s5re_s, s5im_s, wkv_s, shift_s)
```

```python
import functools
import math

import jax
import jax.numpy as jnp
from jax import lax
from jax.experimental import pallas as pl
from jax.experimental.pallas import tpu as pltpu

F32 = jnp.float32
BF16 = jnp.bfloat16

NORM_EPS = 1e-6
GN_EPS = 64e-5
N_MOD = 9

V7X_LANES = 128
V7X_SUBLANES = 8
V7X_MXU_DIM = 256
V7X_VMEM_BYTES = 64 * 1024 * 1024
VMEM_CAP_BYTES = V7X_VMEM_BYTES - 8 * 1024 * 1024

ROW_PARTS = 2
S5_STEP_ROWS = 1024
WKV_CHUNK = 64
WKV_STEP_ROWS = 512


def _cparams(semantics, vmem_bytes):
    return pltpu.CompilerParams(
        dimension_semantics=semantics,
        vmem_limit_bytes=int(min(vmem_bytes, VMEM_CAP_BYTES)),
    )


def _const_spec(shape):
    nd = len(shape)
    return pl.BlockSpec(shape, lambda *_: (0,) * nd, pipeline_mode=pl.Buffered(1))


def _dot(a, b):
    return jnp.dot(a, b, preferred_element_type=F32)


def _dot_nt(a, b):
    return lax.dot_general(a, b, (((1,), (1,)), ((), ())), preferred_element_type=F32)


def _dot_tn(a, b):
    return lax.dot_general(a, b, (((0,), (0,)), ((), ())), preferred_element_type=F32)


def _split3(x):
    hi = x.astype(BF16)
    r1 = x - hi.astype(F32)
    mid = r1.astype(BF16)
    lo = (r1 - mid.astype(F32)).astype(BF16)
    return hi, mid, lo


def _rms(x, g):
    ms = jnp.mean(x * x, axis=-1, keepdims=True)
    return x * lax.rsqrt(ms + NORM_EPS) * g


def _softplus(x):
    return jnp.maximum(x, 0.0) + jnp.log(1.0 + jnp.exp(-jnp.abs(x)))


def _mod_kernel(c_ref, w_ref, b_ref, o_ref):
    c = c_ref[...]
    s = (c * jax.nn.sigmoid(c)).astype(BF16)
    o_ref[...] = _dot(s, w_ref[...].astype(BF16)) + b_ref[...]


def _mod_call(c, w_ada, b_ada):
    rows, d = c.shape
    n = w_ada.shape[1]
    return pl.pallas_call(
        _mod_kernel,
        out_shape=jax.ShapeDtypeStruct((n // d, rows, d), F32),
        grid=(n // d,),
        in_specs=[
            pl.BlockSpec((rows, d), lambda j: (0, 0)),
            pl.BlockSpec((d, d), lambda j: (0, j)),
            pl.BlockSpec((1, d), lambda j: (0, j)),
        ],
        out_specs=pl.BlockSpec((None, rows, d), lambda j: (j, 0, 0)),
        compiler_params=_cparams(("parallel",), 32 * 2**20),
        name="adaln_mod",
    )(c, w_ada, b_ada.reshape(1, n))


class _ModRows:
    def __init__(self, mod, row0, nrows, tm, tiles_per_seq):
        self.mod, self.row0, self.nrows, self.tm, self.tiles_per_seq = mod, row0, nrows, tm, tiles_per_seq

    def spec(self, k):
        d = self.mod.shape[2]
        if self.tiles_per_seq is None:
            blk0 = self.row0 // self.tm
            return pl.BlockSpec((None, self.tm, d), lambda i: (k, blk0 + i, 0))
        blk0 = self.row0 // self.nrows
        return pl.BlockSpec((None, self.nrows, d), lambda i: (k, blk0, 0))


def _mod_row(ref, tiles_per_seq):
    if tiles_per_seq is None:
        return ref[...]
    return ref[pl.ds(pl.program_id(0) // tiles_per_seq, 1), :]


def _ffn_core(x, sh, sc, gt, g, w1_ref, w3_ref, w2_ref, n_chunks):
    bounds = _ffn_bounds(w1_ref.shape[1], n_chunks)
    u = (_rms(x, g) * (1.0 + sc) + sh).astype(BF16)
    acc = None
    for lo, hi in zip(bounds[:-1], bounds[1:]):
        a = _dot(u, w1_ref[:, lo:hi])
        b = _dot(u, w3_ref[:, lo:hi])
        hm = (a * jax.nn.sigmoid(a) * b).astype(BF16)
        part = _dot(hm, w2_ref[lo:hi, :])
        acc = part if acc is None else acc + part
    return x + (0.5 * gt) * acc


def _row_parts(rows, n_parts):
    if rows % (n_parts * 2 * V7X_SUBLANES):
        return [slice(0, rows)]
    step = rows // n_parts
    return [slice(r * step, (r + 1) * step) for r in range(n_parts)]


def _mod_part(mod_rows, part):
    return mod_rows if mod_rows.shape[0] == 1 else mod_rows[part]


def _ffn_bounds(d_ff, n_chunks):
    tiles = -(-d_ff // V7X_MXU_DIM)
    return [min(d_ff, ((tiles * c + n_chunks - 1) // n_chunks) * V7X_MXU_DIM) for c in range(n_chunks + 1)]


def _ffn_kernel(x_ref, sh_ref, sc_ref, gt_ref, g_ref, w1_ref, w3_ref, w2_ref, *rest, n_chunks, tiles_per_seq):
    n_cast = len(rest) // 2
    o_ref = rest[n_cast]
    row = functools.partial(_mod_row, tiles_per_seq=tiles_per_seq)
    sh, sc, gt = row(sh_ref), row(sc_ref), row(gt_ref)
    for part in _row_parts(x_ref.shape[0], ROW_PARTS):
        o_ref[part, :] = _ffn_core(x_ref[part, :], _mod_part(sh, part), _mod_part(sc, part), _mod_part(gt, part),
                                   g_ref[...], w1_ref, w3_ref, w2_ref, n_chunks)
    for src, dst in zip(rest[:n_cast], rest[n_cast + 1:]):
        dst[...] = src[...].astype(BF16)


def _cast_blocks(arr, n_steps):
    bf16_rows = 2 * V7X_SUBLANES
    nblk = n_steps
    while arr.shape[0] % nblk or (arr.shape[0] // nblk) % bf16_rows:
        nblk //= 2
    per = n_steps // nblk
    return pl.BlockSpec((arr.shape[0] // nblk, arr.shape[1]), lambda i: (i // per, 0))


def _ffn_chunks(d_ff):
    return 2 if d_ff % (2 * V7X_LANES) == 0 else 1


def _ffn_call(x2, mods, ks, g, w1, w3, w2, *, tm, casts=()):
    rows, d = x2.shape
    d_ff = w1.shape[1]
    n_steps = rows // tm
    kern = functools.partial(_ffn_kernel, n_chunks=_ffn_chunks(d_ff), tiles_per_seq=mods.tiles_per_seq)
    weights = 3 * d * d_ff * 2
    tiles = 4 * tm * d * 4 + 4 * tm * d_ff * 4
    cast_specs = [_cast_blocks(a, n_steps) for a in casts]
    cast_bytes = sum(2 * 6 * s.block_shape[0] * s.block_shape[1] for s in cast_specs)
    out = pl.pallas_call(
        kern,
        out_shape=(jax.ShapeDtypeStruct((rows, d), F32),) + tuple(jax.ShapeDtypeStruct(a.shape, BF16) for a in casts),
        grid=(n_steps,),
        in_specs=[
            pl.BlockSpec((tm, d), lambda i: (i, 0)),
            mods.spec(ks[0]), mods.spec(ks[1]), mods.spec(ks[2]),
            _const_spec((1, d)),
            _const_spec(w1.shape), _const_spec(w3.shape), _const_spec(w2.shape),
        ] + cast_specs,
        out_specs=(pl.BlockSpec((tm, d), lambda i: (i, 0)),) + tuple(cast_specs),
        compiler_params=_cparams(("arbitrary",), weights + tiles + cast_bytes + 8 * 2**20),
        name="ffn1",
    )(x2, mods.mod, mods.mod, mods.mod, g.reshape(1, d), w1, w3, w2, *casts)
    return out[0], list(out[1:])


def _proj_kernel(h_ref, sh_ref, sc_ref, g_ref, win_ref, s5_ref, cur_ref, gates_ref, *, tiles_per_seq):
    row = functools.partial(_mod_row, tiles_per_seq=tiles_per_seq)
    u = (_rms(h_ref[...], g_ref[...]) * (1.0 + row(sc_ref)) + row(sh_ref)).astype(BF16)
    n0 = s5_ref.shape[1]
    n1 = n0 + cur_ref.shape[1]
    s5_ref[...] = _dot(u, win_ref[:, :n0])
    cur_ref[...] = _dot(u, win_ref[:, n0:n1])
    gates_ref[...] = _dot(u, win_ref[:, n1:])


def _time_major(rows, width, tm, tiles_per_seq):
    length = tiles_per_seq * tm
    spec = pl.BlockSpec((tm, width), lambda i: (i % tiles_per_seq, i // tiles_per_seq))
    return (length, rows // length * width), spec


def _proj_call(h, mods, ks, g, w_in, widths, *, tm):
    rows, d = h.shape
    n_in = w_in.shape[1]
    shapes = [(rows, w) for w in widths]
    specs = [pl.BlockSpec((tm, w), lambda i: (i, 0)) for w in widths]
    if mods.tiles_per_seq is not None:
        shapes[0], specs[0] = _time_major(rows, widths[0], tm, mods.tiles_per_seq)
    return pl.pallas_call(
        functools.partial(_proj_kernel, tiles_per_seq=mods.tiles_per_seq),
        out_shape=tuple(jax.ShapeDtypeStruct(s, F32) for s in shapes),
        grid=(rows // tm,),
        in_specs=[
            pl.BlockSpec((tm, d), lambda i: (i, 0)),
            mods.spec(ks[0]), mods.spec(ks[1]),
            _const_spec((1, d)),
            _const_spec(w_in.shape),
        ],
        out_specs=tuple(specs),
        compiler_params=_cparams(("parallel",), d * n_in * 2 + 3 * tm * (d + n_in) * 4 + 8 * 2**20),
        name="mix_proj",
    )(h, mods.mod, mods.mod, g.reshape(1, d), w_in)


def _s5_prep_kernel(lre_ref, lim_ref, ldt_ref, bre_ref, bim_ref, cre_ref, cim_ref,
                    are_ref, aim_ref, wb_ref, wcre_ref, wcim_ref, *, gpb):
    lre = lre_ref[...]
    lim = lim_ref[...]
    dt = jnp.exp(ldt_ref[...])
    mag = jnp.exp(lre * dt)
    ang = lim * dt
    a_re = mag * jnp.cos(ang)
    a_im = mag * jnp.sin(ang)
    den = lre * lre + lim * lim
    num_re = a_re - 1.0
    k_re = (num_re * lre + a_im * lim) / den
    k_im = (a_im * lre - num_re * lim) / den
    b_re = bre_ref[...]
    b_im = bim_ref[...]
    bb_re = (k_re * b_re - k_im * b_im).astype(BF16)
    bb_im = (k_re * b_im + k_im * b_re).astype(BF16)
    c_re = cre_ref[...].astype(BF16)
    c_im = (-cim_ref[...]).astype(BF16)
    g, c, n = b_re.shape
    nb = are_ref.shape[0]
    s = gpb * n
    wb_ref[...] = jnp.zeros(wb_ref.shape, BF16)
    wcre_ref[...] = jnp.zeros(wcre_ref.shape, BF16)
    wcim_ref[...] = jnp.zeros(wcim_ref.shape, BF16)
    for gi in range(g):
        k, gl = divmod(gi, gpb)
        rows = slice(gl * c, (gl + 1) * c)
        cols = slice(gl * n, (gl + 1) * n)
        are_ref[:, gi * n:(gi + 1) * n] = jnp.broadcast_to(a_re[gi], (nb, n))
        aim_ref[:, gi * n:(gi + 1) * n] = jnp.broadcast_to(a_im[gi], (nb, n))
        wb_ref[k, rows, cols] = bb_re[gi]
        wb_ref[k, rows, s + gl * n:s + (gl + 1) * n] = bb_im[gi]
        wcre_ref[k, rows, cols] = c_re[gi]
        wcim_ref[k, rows, cols] = c_im[gi]


def _s5_prep_call(lam_re, lam_im, log_dt, b_re_t, b_im_t, c_re, c_im, nb):
    g, n = lam_re.shape
    c = b_re_t.shape[1]
    gpb = max(1, min(g, V7X_MXU_DIM // c))
    nk = g // gpb
    return pl.pallas_call(
        functools.partial(_s5_prep_kernel, gpb=gpb),
        out_shape=(jax.ShapeDtypeStruct((nb, g * n), F32), jax.ShapeDtypeStruct((nb, g * n), F32),
                   jax.ShapeDtypeStruct((nk, gpb * c, 2 * gpb * n), BF16),
                   jax.ShapeDtypeStruct((nk, gpb * c, gpb * n), BF16),
                   jax.ShapeDtypeStruct((nk, gpb * c, gpb * n), BF16)),
        name="s5_discretise",
    )(lam_re.reshape(g, 1, n), lam_im.reshape(g, 1, n), log_dt.reshape(g, 1, 1), b_re_t, b_im_t, c_re, c_im)


def _s5_input_matmul(ub, wb_ref, bure_ref, buim_ref):
    nk, kin, two_s = wb_ref.shape
    s = two_s // 2
    for k in range(nk):
        res = _dot(ub[:, k * kin:(k + 1) * kin], wb_ref[k])
        bure_ref[:, k * s:(k + 1) * s] = res[:, :s]
        buim_ref[:, k * s:(k + 1) * s] = res[:, s:]


def _s5_output(xre, xim, u, wcre_ref, wcim_ref, d_row):
    nk, kout, s = wcre_ref.shape
    xre_b = xre.astype(BF16)
    xim_b = xim.astype(BF16)
    ys = []
    for k in range(nk):
        ys.append(_dot_nt(xre_b[:, k * s:(k + 1) * s], wcre_ref[k])
                  + _dot_nt(xim_b[:, k * s:(k + 1) * s], wcim_ref[k]))
    y = jnp.concatenate(ys, axis=-1) + d_row * u
    return jax.nn.gelu(y)


def _s5_glu(zb, gv_ref, gg_ref):
    return _dot(zb, gv_ref[...]) * jax.nn.sigmoid(_dot(zb, gg_ref[...]))


def _s5_seq_kernel(u_ref, h0re_ref, h0im_ref, are_ref, aim_ref, wb_ref, wcre_ref, wcim_ref, d_ref,
                   gv_ref, gg_ref,
                   o_ref, hre_ref, him_ref,
                   bure_scr, buim_scr, xre_scr, xim_scr, uslab_scr, oslab_scr, *, lane_chunk):
    nb = hre_ref.shape[0]
    ts = u_ref.shape[0]
    din = u_ref.shape[1] // nb
    dout = o_ref.shape[1] // nb

    @pl.when(pl.program_id(0) == 0)
    def _():
        hre_ref[...] = h0re_ref[...]
        him_ref[...] = h0im_ref[...]

    for b in range(nb):
        for j in range(din // V7X_LANES):
            col = b * din + j * V7X_LANES
            uslab_scr[j, pl.ds(b, ts, stride=nb), :] = u_ref[:, col:col + V7X_LANES]
    u_tm = jnp.concatenate([uslab_scr[j] for j in range(din // V7X_LANES)], axis=1)
    _s5_input_matmul(u_tm.astype(BF16), wb_ref, bure_scr, buim_scr)

    ns = are_ref.shape[1]
    for lc in range(ns // lane_chunk):
        sl = slice(lc * lane_chunk, (lc + 1) * lane_chunk)
        ar = are_ref[:, sl]
        ai = aim_ref[:, sl]
        sr = hre_ref[:, sl]
        si = him_ref[:, sl]
        for t in range(ts):
            rs = slice(t * nb, (t + 1) * nb)
            nr = ar * sr - ai * si + bure_scr[rs, sl]
            ni = ar * si + ai * sr + buim_scr[rs, sl]
            xre_scr[rs, sl] = nr
            xim_scr[rs, sl] = ni
            sr, si = nr, ni
        hre_ref[:, sl] = sr
        him_ref[:, sl] = si

    z_tm = _s5_output(xre_scr[...], xim_scr[...], u_tm, wcre_ref, wcim_ref, d_ref[...])
    out = _s5_glu(z_tm.astype(BF16), gv_ref, gg_ref)
    for j in range(dout // V7X_LANES):
        oslab_scr[j] = out[:, j * V7X_LANES:(j + 1) * V7X_LANES]
    for b in range(nb):
        for j in range(dout // V7X_LANES):
            col = b * dout + j * V7X_LANES
            o_ref[:, col:col + V7X_LANES] = oslab_scr[j, pl.ds(b, ts, stride=nb), :]


def _s5_rows_kernel(u_ref, h0re_ref, h0im_ref, are_ref, aim_ref, wb_ref, wcre_ref, wcim_ref, d_ref,
                    gv_ref, gg_ref, o_ref, hre_ref, him_ref, bure_scr, buim_scr):
    u = u_ref[...]
    _s5_input_matmul(u.astype(BF16), wb_ref, bure_scr, buim_scr)
    ar = are_ref[0:1, :]
    ai = aim_ref[0:1, :]
    sr = h0re_ref[...]
    si = h0im_ref[...]
    nr = ar * sr - ai * si + bure_scr[...]
    ni = ar * si + ai * sr + buim_scr[...]
    hre_ref[...] = nr
    him_ref[...] = ni
    z = _s5_output(nr, ni, u, wcre_ref, wcim_ref, d_ref[...])
    o_ref[...] = _s5_glu(z.astype(BF16), gv_ref, gg_ref)


def _s5_weight_specs(ws):
    return [_const_spec(w.shape) for w in ws]


def _s5_seq_call(s5_in, h0re, h0im, a_re, a_im, sw, *, ts):
    nb, ns = a_re.shape
    length = s5_in.shape[0]
    din = s5_in.shape[1] // nb
    d = sw["glu_v"].shape[1]
    rows = nb * ts
    ws = [sw["wb"], sw["wc_re"], sw["wc_im"], sw["d"], sw["glu_v"], sw["glu_g"]]
    kern = functools.partial(_s5_seq_kernel, lane_chunk=4 * V7X_LANES)
    return pl.pallas_call(
        kern,
        out_shape=(jax.ShapeDtypeStruct((length, nb * d), F32),
                   jax.ShapeDtypeStruct((nb, ns), F32), jax.ShapeDtypeStruct((nb, ns), F32)),
        grid=(length // ts,),
        in_specs=[pl.BlockSpec((ts, nb * din), lambda c: (c, 0)),
                  _const_spec((nb, ns)), _const_spec((nb, ns)), _const_spec((nb, ns)), _const_spec((nb, ns))]
                 + _s5_weight_specs(ws),
        out_specs=(pl.BlockSpec((ts, nb * d), lambda c: (c, 0)),
                   pl.BlockSpec((nb, ns), lambda c: (0, 0)), pl.BlockSpec((nb, ns), lambda c: (0, 0))),
        scratch_shapes=[pltpu.VMEM((rows, ns), F32) for _ in range(4)]
                       + [pltpu.VMEM((din // V7X_LANES, rows, V7X_LANES), F32),
                          pltpu.VMEM((d // V7X_LANES, rows, V7X_LANES), F32)],
        compiler_params=_cparams(("arbitrary",), 48 * 2**20),
        name="s5_seq",
    )(s5_in, h0re, h0im, a_re, a_im, *ws)


def _s5_rows_call(s5_in, h0re, h0im, a_re, a_im, sw):
    rows, din = s5_in.shape
    ns = a_re.shape[1]
    d = sw["glu_v"].shape[1]
    ws = [sw["wb"], sw["wc_re"], sw["wc_im"], sw["d"], sw["glu_v"], sw["glu_g"]]
    full = lambda shape: pl.BlockSpec(shape, lambda i: (0,) * len(shape))
    return pl.pallas_call(
        _s5_rows_kernel,
        out_shape=(jax.ShapeDtypeStruct((rows, d), F32),
                   jax.ShapeDtypeStruct((rows, ns), F32), jax.ShapeDtypeStruct((rows, ns), F32)),
        grid=(1,),
        in_specs=[full((rows, din)), full((rows, ns)), full((rows, ns)), full(a_re.shape), full(a_im.shape)]
                 + [full(w.shape) for w in ws],
        out_specs=(full((rows, d)), full((rows, ns)), full((rows, ns))),
        scratch_shapes=[pltpu.VMEM((rows, ns), F32) for _ in range(2)],
        compiler_params=_cparams(("arbitrary",), 40 * 2**20),
        name="s5_rows",
    )(s5_in, h0re, h0im, a_re, a_im, *ws)


def _headsum(x, ones_bd):
    hi = x.astype(BF16)
    lo = (x - hi.astype(F32)).astype(BF16)
    gw = ones_bd.shape[0]
    return jnp.concatenate([_dot(hi[:, s:s + gw], ones_bd) + _dot(lo[:, s:s + gw], ones_bd)
                            for s in range(0, x.shape[1], gw)], axis=1)


def _pre_lora(cur, prev, p, width):
    mixed = cur + p["mu"] * (prev - cur)
    nl = p["w2p"].shape[0]
    wa = mixed[:, 3 * width:3 * width + nl]
    gd = mixed[:, 3 * width + nl:]
    return dict(
        r=mixed[:, 0:width], k=mixed[:, width:2 * width], v=mixed[:, 2 * width:3 * width],
        lw=_dot(jnp.tanh(wa).astype(BF16), p["w2p"]),
        la=_dot(wa.astype(BF16), p["a2p"]),
        g=_dot(jax.nn.sigmoid(gd).astype(BF16), p["g2"]))


def _pre_gates(t, p, ones_bd):
    w = -_softplus(-(p["w0"] + t["lw"])) - 0.5
    t["ld"] = -jnp.exp(w)
    t["a"] = jax.nn.sigmoid(p["a0"] + t["la"])
    t["kk"] = t["k"] * p["k_k"]
    t["ss"] = _headsum(t["kk"] * t["kk"], ones_bd)


def _pre_keys(t, p):
    kk = t["kk"] * jnp.minimum(lax.rsqrt(t["ss"]), 1e12)
    t["k2"] = t["k"] * (1.0 + (t["a"] - 1.0) * p["k_a"])
    t["av"] = -kk
    t["bv"] = kk * t["a"]


def _rwkv_pre(cur, prev, p, ones_bd, width):
    t = _pre_lora(cur, prev, p, width)
    _pre_gates(t, p, ones_bd)
    _pre_keys(t, p)
    return t


def _rwkv_post(y, t, p, ones_bd, proj, head):
    inv = 1.0 / head
    dlt = y - _headsum(y, ones_bd) * inv
    var = _headsum(dlt * dlt, ones_bd) * inv
    bonus = _headsum(t["r"] * t["k2"] * p["r_k"], ones_bd) * t["v"]
    yn = dlt * lax.rsqrt(var + GN_EPS) * p["ln_w"] + p["ln_b"]
    return _dot(((yn + bonus) * t["g"]).astype(BF16), proj)


def _head_blocks(x, bd_mask):
    return jnp.concatenate([x] * (bd_mask.shape[0] // x.shape[0]), axis=0) * bd_mask


def _wkv_chunks(t, s_scr, y_ref, tril, bd_mask, head, chunk, nseq):
    rows, width = t["r"].shape
    c = chunk
    gw = bd_mask.shape[0]
    assert c == head and width % gw == 0, "head packing needs chunk == head size and whole lane groups"
    per_seq = rows // (nseq * c)
    chunks = [(q * per_seq + i, q) for q in range(nseq) for i in range(per_seq)]
    groups = [(j, q, g) for j, q in chunks for g in range(width // gw)]
    gs = lambda g: slice(g * gw, (g + 1) * gw)
    ri = lax.broadcasted_iota(jnp.int32, (c, gw), 0)
    ci = lax.broadcasted_iota(jnp.int32, (c, gw), 1) % c
    strict = ri > ci
    incl = ri >= ci
    blocks = functools.partial(_head_blocks, bd_mask=bd_mask)
    on_diag = (lax.broadcasted_iota(jnp.int32, (gw, gw), 0) // head
               == lax.broadcasted_iota(jnp.int32, (gw, gw), 1) // head)

    fac = {}
    for j, _ in chunks:
        rs = slice(j * c, (j + 1) * c)
        lds = t["ld"][rs]
        hi, mid, lo = _split3(lds)
        fac[j] = dict(lds=lds, rs=rs, cum=_dot(tril, hi) + _dot(tril, mid) + _dot(tril, lo))
    for j, _ in chunks:
        f = fac[j]
        cum, rs = f["cum"], f["rs"]
        g_inv = jnp.exp(-cum)
        cum_end = cum[c - 1:c, :]
        g_rel = jnp.exp(cum_end - cum)
        f.update(ar=jnp.concatenate([t["av"][rs] * jnp.exp(cum - f["lds"]), t["r"][rs] * jnp.exp(cum)],
                                    axis=0).astype(BF16),
                 b_t=(t["bv"][rs] * g_inv).astype(BF16), k_t=(t["k2"][rs] * g_inv).astype(BF16),
                 bk_e=jnp.concatenate([t["bv"][rs] * g_rel, t["k2"][rs] * g_rel], axis=0).astype(BF16),
                 g_end=jnp.exp(cum_end), v=t["v"][rs].astype(BF16))

    xb, xk, pw, nm, m_rb, lvmv = ({} for _ in range(6))
    for j, _, g in groups:
        f = fac[j]
        xb[j, g] = _dot_nt(f["ar"][:, gs(g)], blocks(f["b_t"][:, gs(g)]))
        xk[j, g] = _dot_nt(f["ar"][:, gs(g)], blocks(f["k_t"][:, gs(g)]))
    for j, _, g in groups:
        pw[j, g] = jnp.where(strict, xb[j, g][:c], 0.0)
        m_rb[j, g] = jnp.where(incl, xb[j, g][c:], 0.0).astype(BF16)
        akrk = jnp.concatenate([jnp.where(strict, xk[j, g][:c], 0.0), jnp.where(incl, xk[j, g][c:], 0.0)], axis=0)
        lvmv[j, g] = _dot(akrk.astype(BF16), blocks(fac[j]["v"][:, gs(g)]))
    n_pow = int(math.log2(c))
    for i in range(n_pow):
        for j, _, g in groups:
            pb = pw[j, g].astype(BF16)
            lhs = [pb] if i < n_pow - 1 else []
            lhs += [nm[j, g].astype(BF16)] if i > 0 else []
            prod = _dot(jnp.concatenate(lhs, axis=0), blocks(pb))
            if i > 0:
                nm[j, g] = nm[j, g] + pw[j, g] + prod[-c:]
            else:
                nm[j, g] = pw[j, g]
            if i < n_pow - 1:
                pw[j, g] = prod[:c]

    for i in range(per_seq):
        now = [(j, q, g) for j, q, g in groups if j == q * per_seq + i]
        s0, am, ub = {}, {}, {}
        for j, q, g in now:
            s0[j, g] = s_scr[q, g]
            am[j, g] = _dot_nt(fac[j]["ar"][:, gs(g)], s0[j, g].astype(BF16))
        for j, q, g in now:
            w = am[j, g][:c] + lvmv[j, g][:c]
            ub[j, g] = (w + _dot(nm[j, g].astype(BF16), blocks(w.astype(BF16)))).astype(BF16)
        for j, q, g in now:
            f = fac[j]
            y_ref[f["rs"], gs(g)] = am[j, g][c:] + _dot(m_rb[j, g], blocks(ub[j, g])) + lvmv[j, g][c:]
            uv = jnp.concatenate([ub[j, g], f["v"][:, gs(g)]], axis=0)
            s_scr[q, g] = jnp.where(on_diag, s0[j, g] * f["g_end"][:, gs(g)] + _dot_tn(uv, f["bk_e"][:, gs(g)]), 0.0)


_RWKV_VEC_NAMES = ("w0", "a0", "k_k", "k_a", "r_k", "ln_w", "ln_b")
_RWKV_CONSTS = ("mu", "vecs", "w2p", "a2p", "g2")


def _rwkv_params(mu_ref, vecs_ref, w2p_ref, a2p_ref, g2_ref):
    p = {n: vecs_ref[i:i + 1, :] for i, n in enumerate(_RWKV_VEC_NAMES)}
    p.update(mu=mu_ref[...], w2p=w2p_ref[...], a2p=a2p_ref[...], g2=g2_ref[...])
    return p


def _rwkv_seq_kernel(cur_ref, shift0_ref, s0_ref, *rest, chunk, head):
    np_ = len(_RWKV_CONSTS)
    p = _rwkv_params(*rest[:np_])
    proj_ref, ones_ref, tril_ref = rest[np_:np_ + 3]
    out_ref, st_ref = rest[np_ + 3:np_ + 5]
    carry_scr, y_scr, s_scr = rest[np_ + 5:]
    nseq, tr, _ = cur_ref.shape
    width = y_scr.shape[1]
    gw = s_scr.shape[2]
    blocks = [(q, h, (h * head) // gw, slice((h * head) % gw, (h * head) % gw + head))
              for q in range(nseq) for h in range(width // head)]

    @pl.when(pl.program_id(0) == 0)
    def _():
        carry_scr[...] = shift0_ref[...]
        s_scr[...] = jnp.zeros(s_scr.shape, F32)
        for q, h, g, sl in blocks:
            s_scr[q, g, sl, sl] = s0_ref[q, h]

    row = lax.broadcasted_iota(jnp.int32, cur_ref.shape[1:], 0)
    curs, prevs = [], []
    for q in range(nseq):
        cq = cur_ref[q]
        prevs.append(jnp.where(row == 0, carry_scr[q], pltpu.roll(cq, 1, axis=0)))
        carry_scr[q] = cq[tr - 1:tr, :]
        curs.append(cq)
    ones_bd = ones_ref[...]
    t = _rwkv_pre(jnp.concatenate(curs, axis=0), jnp.concatenate(prevs, axis=0), p, ones_bd, width)
    _wkv_chunks(t, s_scr, y_scr, tril_ref[...], ones_bd, head, chunk, nseq)
    out = _rwkv_post(y_scr[...], t, p, ones_bd, proj_ref[...], head)
    for q in range(nseq):
        out_ref[q] = out[q * tr:(q + 1) * tr]

    @pl.when(pl.program_id(0) == pl.num_programs(0) - 1)
    def _():
        for q, h, g, sl in blocks:
            st_ref[q, h] = s_scr[q, g, sl, sl]


def _rwkv_rows_pre_kernel(cur_ref, prev_ref, mu_ref, vecs_ref, w2p_ref, a2p_ref, g2_ref, ones_ref,
                          r_ref, k_ref, v_ref, g_ref, t_ref, *, width):
    p = _rwkv_params(mu_ref, vecs_ref, w2p_ref, a2p_ref, g2_ref)
    t = _rwkv_pre(cur_ref[...], prev_ref[...], p, ones_ref[...], width)
    r_ref[...] = t["r"]
    k_ref[...] = t["k2"]
    v_ref[...] = t["v"]
    g_ref[...] = t["g"]
    for i, n in enumerate(("r", "ld", "k2", "v", "av", "bv")):
        t_ref[i] = t[n].T


def _wkv_step_kernel(t_ref, s_ref, y_ref, so_ref):
    r = t_ref[0]
    dec = jnp.exp(t_ref[1])
    k = t_ref[2]
    a = t_ref[4]
    b = t_ref[5]

    def body(i, carry):
        s = s_ref[i]
        sa = jnp.sum(s * a, axis=0, keepdims=True)
        s2 = s * dec + sa * b + t_ref[3, pl.ds(i, 1), :] * k
        so_ref[i] = s2
        y_ref[pl.ds(i, 1), :] = jnp.sum(s2 * r, axis=0, keepdims=True)
        return carry

    lax.fori_loop(0, s_ref.shape[0], body, 0, unroll=V7X_SUBLANES)


def _rwkv_rows_post_kernel(yt_ref, r_ref, k_ref, v_ref, g_ref, mu_ref, vecs_ref, w2p_ref, a2p_ref, g2_ref,
                           proj_ref, ones_ref, o_ref, *, head):
    p = _rwkv_params(mu_ref, vecs_ref, w2p_ref, a2p_ref, g2_ref)
    t = dict(r=r_ref[...], k2=k_ref[...], v=v_ref[...], g=g_ref[...])
    o_ref[...] = _rwkv_post(yt_ref[...].T, t, p, ones_ref[...], proj_ref[...], head)


def _rwkv_param_list(rw):
    return [rw[n] for n in _RWKV_CONSTS]


def _rwkv_seq_call(cur3, shift0, s0, rw, *, tr):
    nb, length, n_shift = cur3.shape
    nh, head = s0.shape[1], s0.shape[2]
    width = nh * head
    d = rw["proj"].shape[1]
    chunk = min(WKV_CHUNK, tr)
    gw = rw["ones_bd"].shape[0]
    tril =(jnp.arange(chunk)[:, None] >= jnp.arange(chunk)[None, :]).astype(BF16)
    consts = _rwkv_param_list(rw) + [rw["proj"], rw["ones_bd"], tril]
    whole = lambda c: pl.BlockSpec(c.shape, lambda i, nd=c.ndim: (0,) * nd)
    shift3 = shift0.reshape(nb, 1, n_shift)
    kern = functools.partial(_rwkv_seq_kernel, chunk=chunk, head=head)
    return pl.pallas_call(
        kern,
        out_shape=(jax.ShapeDtypeStruct((nb, length, d), F32),
                   jax.ShapeDtypeStruct((nb, nh, head, head), F32)),
        grid=(length // tr,),
        in_specs=[pl.BlockSpec((nb, tr, n_shift), lambda i: (0, i, 0)), whole(shift3), whole(s0)]
                 + [whole(c) for c in consts],
        out_specs=(pl.BlockSpec((nb, tr, d), lambda i: (0, i, 0)), whole(s0)),
        scratch_shapes=[pltpu.VMEM((nb, 1, n_shift), F32), pltpu.VMEM((nb * tr, width), F32),
                        pltpu.VMEM((nb, width // gw, gw, gw), F32)],
        compiler_params=_cparams(("arbitrary",), 40 * 2**20),
        name="rwkv_seq",
    )(cur3, shift3, s0, *consts)


def _rwkv_rows_call(cur, prev, s0_t, rw):
    rows, n_shift = cur.shape
    nh, head = s0_t.shape[0], s0_t.shape[1]
    width = nh * head
    d = rw["proj"].shape[1]
    params = _rwkv_param_list(rw)
    full = lambda shape: pl.BlockSpec(shape, lambda i: (0,) * len(shape))
    vec = jax.ShapeDtypeStruct((rows, width), F32)
    n_t = 6

    pre_consts = params + [rw["ones_bd"]]
    r, k2, v, g, t = pl.pallas_call(
        functools.partial(_rwkv_rows_pre_kernel, width=width),
        out_shape=(vec,) * 4 + (jax.ShapeDtypeStruct((n_t, width, rows), F32),),
        grid=(1,),
        in_specs=[full(cur.shape), full(prev.shape)] + [full(c.shape) for c in pre_consts],
        out_specs=(full((rows, width)),) * 4 + (full((n_t, width, rows)),),
        compiler_params=_cparams(("arbitrary",), 32 * 2**20),
        name="rwkv_rows_pre",
    )(cur, prev, *pre_consts)

    sblk = pl.BlockSpec((None, head, head, rows), lambda h: (h, 0, 0, 0))
    yt, s1_t = pl.pallas_call(
        _wkv_step_kernel,
        out_shape=(jax.ShapeDtypeStruct((width, rows), F32), jax.ShapeDtypeStruct(s0_t.shape, F32)),
        grid=(nh,),
        in_specs=[pl.BlockSpec((n_t, head, rows), lambda h: (0, h, 0)), sblk],
        out_specs=(pl.BlockSpec((head, rows), lambda h: (h, 0)), sblk),
        compiler_params=_cparams(("parallel",), 32 * 2**20),
        name="wkv_step",
    )(t, s0_t)

    post_consts = params + [rw["proj"], rw["ones_bd"]]
    out = pl.pallas_call(
        functools.partial(_rwkv_rows_post_kernel, head=head),
        out_shape=jax.ShapeDtypeStruct((rows, d), F32),
        grid=(1,),
        in_specs=[full((width, rows))] + [full((rows, width))] * 4 + [full(c.shape) for c in post_consts],
        out_specs=full((rows, d)),
        compiler_params=_cparams(("arbitrary",), 32 * 2**20),
        name="rwkv_rows_post",
    )(yt, r, k2, v, g, *post_consts)
    return out, s1_t


def _merge_kernel(h_ref, gates_ref, ys5_ref, yrw_ref, gt2_ref, wout_ref,
                  sh_ref, sc_ref, gt_ref, g_ref, w1_ref, w3_ref, w2_ref, gfin_ref, o_ref, *, n_chunks,
                  tiles_per_seq):
    row = functools.partial(_mod_row, tiles_per_seq=tiles_per_seq)
    d = h_ref.shape[1]
    gt2, sh, sc, gt = row(gt2_ref), row(sh_ref), row(sc_ref), row(gt_ref)
    parts = _row_parts(h_ref.shape[0], ROW_PARTS)
    mixed = []
    for part in parts:
        m = (jax.nn.sigmoid(gates_ref[part, :d]) * ys5_ref[part, :]
             + jax.nn.sigmoid(gates_ref[part, d:]) * yrw_ref[part, :])
        mixed.append(_dot(m.astype(BF16), wout_ref[...]))
    for part, mo in zip(parts, mixed):
        h2 = h_ref[part, :] + _mod_part(gt2, part) * mo
        h3 = _ffn_core(h2, _mod_part(sh, part), _mod_part(sc, part), _mod_part(gt, part), g_ref[...],
                       w1_ref, w3_ref, w2_ref, n_chunks)
        o_ref[part, :] = _rms(h3, gfin_ref[...])


def _merge_call(h, gates, ys5, yrw, mods, ks, w_out, g, w1, w3, w2, g_fin, *, tm):
    rows, d = h.shape
    d_ff = w1.shape[1]
    kern = functools.partial(_merge_kernel, n_chunks=_ffn_chunks(d_ff), tiles_per_seq=mods.tiles_per_seq)
    row_spec = lambda w: pl.BlockSpec((tm, w), lambda i: (i, 0))
    weights = (3 * d * d_ff + d * d) * 2
    tiles = 2 * tm * 6 * d * 4 + 4 * tm * d_ff * 4 + 6 * tm * d * 4
    ys5_spec = row_spec(d) if mods.tiles_per_seq is None else _time_major(rows, d, tm, mods.tiles_per_seq)[1]
    return pl.pallas_call(
        kern,
        out_shape=jax.ShapeDtypeStruct((rows, d), F32),
        grid=(rows // tm,),
        in_specs=[row_spec(d), row_spec(2 * d), ys5_spec, row_spec(d),
                  mods.spec(ks[0]), _const_spec(w_out.shape),
                  mods.spec(ks[1]), mods.spec(ks[2]), mods.spec(ks[3]),
                  _const_spec((1, d)), _const_spec(w1.shape), _const_spec(w3.shape), _const_spec(w2.shape),
                  _const_spec((1, d))],
        out_specs=row_spec(d),
        compiler_params=_cparams(("parallel",), weights + tiles + 8 * 2**20),
        name="merge_ffn2",
    )(h, gates, ys5, yrw, mods.mod, w_out, mods.mod, mods.mod, mods.mod, g.reshape(1, d), w1, w3, w2,
      g_fin.reshape(1, d))


_LATE_WEIGHTS = ("ffn2_w1", "ffn2_w3", "ffn2_w2", "w_in", "w_out", "glu_v", "glu_g", "proj", "g2", "w2p", "a2p")


def _pick_tile(n, target):
    t = min(n, target)
    while n % t:
        t -= V7X_SUBLANES
    return t


def _layer(x, mod, mod_row0, s5_re0, s5_im0, wkv0, shift0, w, *, sequential):
    nb, length, d = x.shape
    rows = nb * length
    g, n = s5_re0.shape[1], s5_re0.shape[2]
    ns = g * n
    n_shift = shift0.shape[1]
    s5_width = w["s5"]["d"].shape[1]
    widths = (s5_width, n_shift, 2 * d)

    tm = _pick_tile(length if sequential else rows, 512)
    mods = _ModRows(mod, mod_row0, nb, tm, length // tm if sequential else None)

    x2 = x.reshape(rows, d)
    late = w["late"]
    to_cast = [n for n in _LATE_WEIGHTS if late[n].dtype != BF16]
    h1, cast = _ffn_call(x2, mods, (0, 1, 2), w["g_ffn1"], *w["ffn1"], tm=tm, casts=[late[n] for n in to_cast])
    late = dict(late, **dict(zip(to_cast, cast)))
    w = dict(w, late=late, w_in=late["w_in"], w_out=late["w_out"],
             ffn2=(late["ffn2_w1"], late["ffn2_w3"], late["ffn2_w2"]),
             s5=dict(w["s5"], glu_v=late["glu_v"], glu_g=late["glu_g"]),
             rwkv=dict(w["rwkv"], proj=late["proj"], g2=late["g2"], w2p=late["w2p"], a2p=late["a2p"]))
    s5_in, cur, gates = _proj_call(h1, mods, (3, 4), w["g_mix"], w["w_in"], widths, tm=tm)

    h0re = s5_re0.reshape(nb, ns)
    h0im = s5_im0.reshape(nb, ns)
    if sequential:
        ts = _pick_tile(length, S5_STEP_ROWS // nb)
        y_s5, hre, him = _s5_seq_call(s5_in, h0re, h0im, w["a_re"], w["a_im"], w["s5"], ts=ts)
        tr = _pick_tile(length, max(WKV_CHUNK, WKV_STEP_ROWS // nb))
        y_rw, wkv1 = _rwkv_seq_call(cur.reshape(nb, length, n_shift), shift0, wkv0, w["rwkv"], tr=tr)
        y_rw = y_rw.reshape(rows, d)
    else:
        y_s5, hre, him = _s5_rows_call(s5_in, h0re, h0im, w["a_re"], w["a_im"], w["s5"])
        y_rw, wkv1 = _rwkv_rows_call(cur, shift0, wkv0, w["rwkv"])

    y = _merge_call(h1, gates, y_s5, y_rw, mods, (5, 6, 7, 8), w["w_out"],
                    w["g_ffn2"], *w["ffn2"], w["g_final"], tm=tm)

    shift1 = cur.reshape(nb, length, n_shift)[:, -1]
    return (y.reshape(nb, length, d), hre.reshape(nb, g, n), him.reshape(nb, g, n), wkv1, shift1), w


def kernel(x_prompt, x_sample, c_prompt, c_sample, state_s5_re, state_s5_im, state_wkv, state_shift, w_ada, b_ada, g_ffn1, g_mix, g_ffn2, g_final, ffn1_w1, ffn1_w3, ffn1_w2, ffn2_w1, ffn2_w3, ffn2_w2, w_in, mu_shift, s5_lam_re, s5_lam_im, s5_log_dt, s5_b_re, s5_b_im, s5_c_re, s5_c_im, s5_d, s5_glu_v, s5_glu_g, rwkv_w0, rwkv_w2, rwkv_a0, rwkv_a2, rwkv_g2, rwkv_k_k, rwkv_k_a, rwkv_r_k, rwkv_ln_w, rwkv_ln_b, rwkv_proj, w_out):
    bp, _, d = x_prompt.shape
    bs = x_sample.shape[0]
    g, n = s5_lam_re.shape
    nh, head = rwkv_r_k.shape
    width = nh * head
    n_shift = mu_shift.shape[0]
    assert bs % bp == 0, "sample rows come first in the modulation array; prompt rows must stay block-aligned"
    bf = lambda t: t.astype(BF16)

    a_re, a_im, wb, wc_re, wc_im = _s5_prep_call(
        s5_lam_re, s5_lam_im, s5_log_dt, jnp.swapaxes(s5_b_re, 1, 2), jnp.swapaxes(s5_b_im, 1, 2),
        s5_c_re, s5_c_im, bp)
    w_lora = rwkv_w2.shape[0]
    a_lora = rwkv_a2.shape[0]
    w2p = jnp.concatenate([rwkv_w2, jnp.zeros((a_lora, width), F32)], axis=0)
    a2p = jnp.concatenate([jnp.zeros((w_lora, width), F32), rwkv_a2], axis=0)
    vec_rows = [rwkv_w0, rwkv_a0, rwkv_k_k, rwkv_k_a, rwkv_r_k.reshape(width), rwkv_ln_w, rwkv_ln_b]
    vecs = jnp.stack(vec_rows + [jnp.zeros_like(rwkv_w0)] * (V7X_SUBLANES - len(vec_rows)))
    rw = dict(mu=mu_shift.reshape(1, n_shift), vecs=vecs,
              ones_bd=jnp.kron(jnp.eye(min(width, V7X_MXU_DIM) // head, dtype=F32),
                               jnp.ones((head, head), F32)).astype(BF16))
    late = dict(ffn2_w1=ffn2_w1, ffn2_w3=ffn2_w3, ffn2_w2=ffn2_w2, w_in=w_in, w_out=w_out,
                glu_v=s5_glu_v, glu_g=s5_glu_g, proj=rwkv_proj, g2=rwkv_g2, w2p=w2p, a2p=a2p)
    w = dict(
        g_ffn1=g_ffn1, g_mix=g_mix, g_ffn2=g_ffn2, g_final=g_final,
        ffn1=(bf(ffn1_w1), bf(ffn1_w3), bf(ffn1_w2)),
        a_re=a_re, a_im=a_im,
        s5=dict(wb=wb, wc_re=wc_re, wc_im=wc_im, d=s5_d.reshape(1, -1)),
        rwkv=rw, late=late,
    )

    mod = _mod_call(jnp.concatenate([c_sample, c_prompt], axis=0), w_ada, b_ada)

    z_s5 = jnp.zeros((bp, g, n), state_s5_re.dtype)
    z_wkv = jnp.zeros((bp, nh, head, head), state_wkv.dtype)
    z_shift = jnp.zeros((bp, n_shift), state_shift.dtype)
    (y_p, s5re_p, s5im_p, wkv_p, shift_p), w = _layer(x_prompt, mod, bs, z_s5, z_s5, z_wkv, z_shift, w,
                                                      sequential=True)
    (y_s, s5re_s, s5im_s, wkv_s_t, shift_s), _ = _layer(x_sample, mod, 0, state_s5_re, state_s5_im,
                                                        jnp.transpose(state_wkv, (1, 2, 3, 0)), state_shift, w,
                                                        sequential=False)
    wkv_s = jnp.transpose(wkv_s_t, (3, 0, 1, 2))
    return (y_p, y_s, s5re_p, s5im_p, wkv_p, shift_p, s5re_s, s5im_s, wkv_s, shift_s)
```

```python
import functools
import math

import jax
import jax.numpy as jnp
from jax import lax
from jax.experimental import pallas as pl
from jax.experimental.pallas import tpu as pltpu

F32 = jnp.float32
BF16 = jnp.bfloat16

NORM_EPS = 1e-6
GN_EPS = 64e-5
N_MOD = 9

V7X_LANES = 128
V7X_SUBLANES = 8
V7X_MXU_DIM = 256
V7X_VMEM_BYTES = 64 * 1024 * 1024
VMEM_CAP_BYTES = V7X_VMEM_BYTES - 8 * 1024 * 1024

ROW_PARTS = 2
S5_STEP_ROWS = 1024
WKV_CHUNK = 64
WKV_STEP_ROWS = 512


def _cparams(semantics, vmem_bytes):
    return pltpu.CompilerParams(
        dimension_semantics=semantics,
        vmem_limit_bytes=int(min(vmem_bytes, VMEM_CAP_BYTES)),
    )


def _const_spec(shape):
    nd = len(shape)
    return pl.BlockSpec(shape, lambda *_: (0,) * nd, pipeline_mode=pl.Buffered(1))


def _dot(a, b):
    return jnp.dot(a, b, preferred_element_type=F32)


def _dot_nt(a, b):
    return lax.dot_general(a, b, (((1,), (1,)), ((), ())), preferred_element_type=F32)


def _dot_tn(a, b):
    return lax.dot_general(a, b, (((0,), (0,)), ((), ())), preferred_element_type=F32)


def _split3(x):
    hi = x.astype(BF16)
    r1 = x - hi.astype(F32)
    mid = r1.astype(BF16)
    lo = (r1 - mid.astype(F32)).astype(BF16)
    return hi, mid, lo


def _rms(x, g):
    ms = jnp.mean(x * x, axis=-1, keepdims=True)
    return x * lax.rsqrt(ms + NORM_EPS) * g


def _softplus(x):
    return jnp.maximum(x, 0.0) + jnp.log(1.0 + jnp.exp(-jnp.abs(x)))


def _mod_kernel(c_ref, w_ref, b_ref, o_ref):
    c = c_ref[...]
    s = (c * jax.nn.sigmoid(c)).astype(BF16)
    o_ref[...] = _dot(s, w_ref[...].astype(BF16)) + b_ref[...]


def _mod_call(c, w_ada, b_ada):
    rows, d = c.shape
    n = w_ada.shape[1]
    return pl.pallas_call(
        _mod_kernel,
        out_shape=jax.ShapeDtypeStruct((n // d, rows, d), F32),
        grid=(n // d,),
        in_specs=[
            pl.BlockSpec((rows, d), lambda j: (0, 0)),
            pl.BlockSpec((d, d), lambda j: (0, j)),
            pl.BlockSpec((1, d), lambda j: (0, j)),
        ],
        out_specs=pl.BlockSpec((None, rows, d), lambda j: (j, 0, 0)),
        compiler_params=_cparams(("parallel",), 32 * 2**20),
        name="adaln_mod",
    )(c, w_ada, b_ada.reshape(1, n))


class _ModRows:
    def __init__(self, mod, row0, nrows, tm, tiles_per_seq):
        self.mod, self.row0, self.nrows, self.tm, self.tiles_per_seq = mod, row0, nrows, tm, tiles_per_seq

    def spec(self, k):
        d = self.mod.shape[2]
        if self.tiles_per_seq is None:
            blk0 = self.row0 // self.tm
            return pl.BlockSpec((None, self.tm, d), lambda i: (k, blk0 + i, 0))
        blk0 = self.row0 // self.nrows
        return pl.BlockSpec((None, self.nrows, d), lambda i: (k, blk0, 0))

    def fixed_spec(self, k):
        d = self.mod.shape[2]
        blk0 = self.row0 // self.tm
        return pl.BlockSpec((None, self.tm, d), lambda i: (k, blk0, 0))


def _mod_row(ref, tiles_per_seq):
    if tiles_per_seq is None:
        return ref[...]
    return ref[pl.ds(pl.program_id(0) // tiles_per_seq, 1), :]


def _ffn_core(x, sh, sc, gt, g, w1_ref, w3_ref, w2_ref, n_chunks):
    bounds = _ffn_bounds(w1_ref.shape[1], n_chunks)
    u = (_rms(x, g) * (1.0 + sc) + sh).astype(BF16)
    acc = None
    for lo, hi in zip(bounds[:-1], bounds[1:]):
        a = _dot(u, w1_ref[:, lo:hi])
        b = _dot(u, w3_ref[:, lo:hi])
        hm = (a * jax.nn.sigmoid(a) * b).astype(BF16)
        part = _dot(hm, w2_ref[lo:hi, :])
        acc = part if acc is None else acc + part
    return x + (0.5 * gt) * acc


def _row_parts(rows, n_parts):
    if rows % (n_parts * 2 * V7X_SUBLANES):
        return [slice(0, rows)]
    step = rows // n_parts
    return [slice(r * step, (r + 1) * step) for r in range(n_parts)]


def _mod_part(mod_rows, part):
    return mod_rows if mod_rows.shape[0] == 1 else mod_rows[part]


def _ffn_bounds(d_ff, n_chunks):
    tiles = -(-d_ff // V7X_MXU_DIM)
    return [min(d_ff, ((tiles * c + n_chunks - 1) // n_chunks) * V7X_MXU_DIM) for c in range(n_chunks + 1)]


def _with_rider(n_main, main, rider):
    if rider is None:
        main()
    else:
        pl.when(pl.program_id(0) < n_main)(main)
        pl.when(pl.program_id(0) == n_main)(rider)


def _ffn_tile(x_ref, sh, sc, gt, g_ref, w1_ref, w3_ref, w2_ref, o_ref, n_chunks):
    for part in _row_parts(x_ref.shape[0], ROW_PARTS):
        o_ref[part, :] = _ffn_core(x_ref[part, :], _mod_part(sh, part), _mod_part(sc, part), _mod_part(gt, part),
                                   g_ref[...], w1_ref, w3_ref, w2_ref, n_chunks)


def _ffn_kernel(x_ref, sh_ref, sc_ref, gt_ref, g_ref, w1_ref, w3_ref, w2_ref, *rest,
                n_chunks, tiles_per_seq, n_cast, n_main):
    has_rider = len(rest) > 2 * n_cast + 1
    rider_in, rest = (rest[:4], rest[4:]) if has_rider else ((), rest)
    weights = (g_ref, w1_ref, w3_ref, w2_ref)

    def main():
        row = functools.partial(_mod_row, tiles_per_seq=tiles_per_seq)
        _ffn_tile(x_ref, row(sh_ref), row(sc_ref), row(gt_ref), *weights, rest[n_cast], n_chunks)
        for src, dst in zip(rest[:n_cast], rest[n_cast + 1:2 * n_cast + 1]):
            dst[...] = src[...].astype(BF16)

    def rider():
        xs, shs, scs, gts = rider_in
        _ffn_tile(xs, shs[...], scs[...], gts[...], *weights, rest[-1], n_chunks)

    _with_rider(n_main, main, rider if has_rider else None)


def _cast_blocks(arr, n_steps):
    bf16_rows = 2 * V7X_SUBLANES
    nblk = n_steps
    while arr.shape[0] % nblk or (arr.shape[0] // nblk) % bf16_rows:
        nblk //= 2
    per = n_steps // nblk
    return pl.BlockSpec((arr.shape[0] // nblk, arr.shape[1]), lambda i: (jnp.minimum(i, n_steps - 1) // per, 0))


def _ffn_chunks(d_ff):
    return 2 if d_ff % (2 * V7X_LANES) == 0 else 1


def _rider_specs(arrays, mods_s, ks):
    return ([pl.BlockSpec(a.shape, lambda i: (0, 0)) for a in arrays] + [mods_s.fixed_spec(k) for k in ks],
            list(arrays) + [mods_s.mod] * len(ks))


def _ffn_call(x2, mods, ks, g, w1, w3, w2, *, tm, casts=(), rider=None):
    rows, d = x2.shape
    d_ff = w1.shape[1]
    n_steps = rows // tm
    assert rider is None or mods.tiles_per_seq is not None
    kern = functools.partial(_ffn_kernel, n_chunks=_ffn_chunks(d_ff), tiles_per_seq=mods.tiles_per_seq,
                             n_cast=len(casts), n_main=n_steps)
    weights = 3 * d * d_ff * 2
    tiles = 4 * tm * d * 4 + 4 * tm * d_ff * 4
    cast_specs = [_cast_blocks(a, n_steps) for a in casts]
    cast_bytes = sum(2 * 6 * s.block_shape[0] * s.block_shape[1] for s in cast_specs)
    tile_spec = pl.BlockSpec((tm, d), lambda i: (jnp.minimum(i, n_steps - 1), 0))
    r_specs, r_args = _rider_specs([rider[0]], rider[1], ks) if rider else ([], [])
    r_out = [pl.BlockSpec(rider[0].shape, lambda i: (0, 0))] if rider else []
    out = pl.pallas_call(
        kern,
        out_shape=(jax.ShapeDtypeStruct((rows, d), F32),) + tuple(jax.ShapeDtypeStruct(a.shape, BF16) for a in casts)
                  + tuple(jax.ShapeDtypeStruct(s.block_shape, F32) for s in r_out),
        grid=(n_steps + len(r_out),),
        in_specs=[
            tile_spec,
            mods.spec(ks[0]), mods.spec(ks[1]), mods.spec(ks[2]),
            _const_spec((1, d)),
            _const_spec(w1.shape), _const_spec(w3.shape), _const_spec(w2.shape),
        ] + r_specs + cast_specs,
        out_specs=(tile_spec,) + tuple(cast_specs) + tuple(r_out),
        compiler_params=_cparams(("arbitrary",), weights + tiles + cast_bytes + 8 * 2**20),
        name="ffn1",
    )(x2, mods.mod, mods.mod, mods.mod, g.reshape(1, d), w1, w3, w2, *r_args, *casts)
    n_cast = len(casts)
    return out[0], list(out[1:1 + n_cast]), (out[1 + n_cast] if rider else None)


def _proj_kernel(h_ref, sh_ref, sc_ref, g_ref, win_ref, s5_ref, cur_ref, gates_ref, *, tiles_per_seq):
    row = functools.partial(_mod_row, tiles_per_seq=tiles_per_seq)
    u = (_rms(h_ref[...], g_ref[...]) * (1.0 + row(sc_ref)) + row(sh_ref)).astype(BF16)
    n0 = s5_ref.shape[1]
    n1 = n0 + cur_ref.shape[1]
    s5_ref[...] = _dot(u, win_ref[:, :n0])
    cur_ref[...] = _dot(u, win_ref[:, n0:n1])
    gates_ref[...] = _dot(u, win_ref[:, n1:])


def _time_major(rows, width, tm, tiles_per_seq):
    length = tiles_per_seq * tm
    last = rows // tm - 1

    def index(i):
        i = jnp.minimum(i, last)
        return i % tiles_per_seq, i // tiles_per_seq

    return (length, rows // length * width), pl.BlockSpec((tm, width), index)


def _proj_call(h, mods, ks, g, w_in, widths, *, tm):
    rows, d = h.shape
    n_in = w_in.shape[1]
    shapes = [(rows, w) for w in widths]
    specs = [pl.BlockSpec((tm, w), lambda i: (i, 0)) for w in widths]
    if mods.tiles_per_seq is not None:
        shapes[0], specs[0] = _time_major(rows, widths[0], tm, mods.tiles_per_seq)
    return pl.pallas_call(
        functools.partial(_proj_kernel, tiles_per_seq=mods.tiles_per_seq),
        out_shape=tuple(jax.ShapeDtypeStruct(s, F32) for s in shapes),
        grid=(rows // tm,),
        in_specs=[
            pl.BlockSpec((tm, d), lambda i: (i, 0)),
            mods.spec(ks[0]), mods.spec(ks[1]),
            _const_spec((1, d)),
            _const_spec(w_in.shape),
        ],
        out_specs=tuple(specs),
        compiler_params=_cparams(("parallel",), d * n_in * 2 + 3 * tm * (d + n_in) * 4 + 8 * 2**20),
        name="mix_proj",
    )(h, mods.mod, mods.mod, g.reshape(1, d), w_in)


def _s5_prep_kernel(lre_ref, lim_ref, ldt_ref, bre_ref, bim_ref, cre_ref, cim_ref,
                    are_ref, aim_ref, wb_ref, wcre_ref, wcim_ref, *, gpb):
    lre = lre_ref[...]
    lim = lim_ref[...]
    dt = jnp.exp(ldt_ref[...])
    mag = jnp.exp(lre * dt)
    ang = lim * dt
    a_re = mag * jnp.cos(ang)
    a_im = mag * jnp.sin(ang)
    den = lre * lre + lim * lim
    num_re = a_re - 1.0
    k_re = (num_re * lre + a_im * lim) / den
    k_im = (a_im * lre - num_re * lim) / den
    b_re = bre_ref[...]
    b_im = bim_ref[...]
    bb_re = (k_re * b_re - k_im * b_im).astype(BF16)
    bb_im = (k_re * b_im + k_im * b_re).astype(BF16)
    c_re = cre_ref[...].astype(BF16)
    c_im = (-cim_ref[...]).astype(BF16)
    g, c, n = b_re.shape
    nb = are_ref.shape[0]
    s = gpb * n
    wb_ref[...] = jnp.zeros(wb_ref.shape, BF16)
    wcre_ref[...] = jnp.zeros(wcre_ref.shape, BF16)
    wcim_ref[...] = jnp.zeros(wcim_ref.shape, BF16)
    for gi in range(g):
        k, gl = divmod(gi, gpb)
        rows = slice(gl * c, (gl + 1) * c)
        cols = slice(gl * n, (gl + 1) * n)
        are_ref[:, gi * n:(gi + 1) * n] = jnp.broadcast_to(a_re[gi], (nb, n))
        aim_ref[:, gi * n:(gi + 1) * n] = jnp.broadcast_to(a_im[gi], (nb, n))
        wb_ref[k, rows, cols] = bb_re[gi]
        wb_ref[k, rows, s + gl * n:s + (gl + 1) * n] = bb_im[gi]
        wcre_ref[k, rows, cols] = c_re[gi]
        wcim_ref[k, rows, cols] = c_im[gi]


def _s5_prep_call(lam_re, lam_im, log_dt, b_re_t, b_im_t, c_re, c_im, nb):
    g, n = lam_re.shape
    c = b_re_t.shape[1]
    gpb = max(1, min(g, V7X_MXU_DIM // c))
    nk = g // gpb
    return pl.pallas_call(
        functools.partial(_s5_prep_kernel, gpb=gpb),
        out_shape=(jax.ShapeDtypeStruct((nb, g * n), F32), jax.ShapeDtypeStruct((nb, g * n), F32),
                   jax.ShapeDtypeStruct((nk, gpb * c, 2 * gpb * n), BF16),
                   jax.ShapeDtypeStruct((nk, gpb * c, gpb * n), BF16),
                   jax.ShapeDtypeStruct((nk, gpb * c, gpb * n), BF16)),
        name="s5_discretise",
    )(lam_re.reshape(g, 1, n), lam_im.reshape(g, 1, n), log_dt.reshape(g, 1, 1), b_re_t, b_im_t, c_re, c_im)


def _s5_input_matmul(ub, wb_ref, bure_ref, buim_ref):
    nk, kin, two_s = wb_ref.shape
    s = two_s // 2
    for k in range(nk):
        res = _dot(ub[:, k * kin:(k + 1) * kin], wb_ref[k])
        bure_ref[:, k * s:(k + 1) * s] = res[:, :s]
        buim_ref[:, k * s:(k + 1) * s] = res[:, s:]


def _s5_output(xre, xim, u, wcre_ref, wcim_ref, d_row):
    nk, kout, s = wcre_ref.shape
    xre_b = xre.astype(BF16)
    xim_b = xim.astype(BF16)
    ys = []
    for k in range(nk):
        ys.append(_dot_nt(xre_b[:, k * s:(k + 1) * s], wcre_ref[k])
                  + _dot_nt(xim_b[:, k * s:(k + 1) * s], wcim_ref[k]))
    y = jnp.concatenate(ys, axis=-1) + d_row * u
    return jax.nn.gelu(y)


def _s5_glu(zb, gv_ref, gg_ref):
    return _dot(zb, gv_ref[...]) * jax.nn.sigmoid(_dot(zb, gg_ref[...]))


def _s5_seq_kernel(u_ref, h0re_ref, h0im_ref, are_ref, aim_ref, wb_ref, wcre_ref, wcim_ref, d_ref,
                   gv_ref, gg_ref,
                   o_ref, hre_ref, him_ref,
                   bure_scr, buim_scr, xre_scr, xim_scr, uslab_scr, oslab_scr, *, lane_chunk):
    nb = hre_ref.shape[0]
    ts = u_ref.shape[0]
    din = u_ref.shape[1] // nb
    dout = o_ref.shape[1] // nb

    @pl.when(pl.program_id(0) == 0)
    def _():
        hre_ref[...] = h0re_ref[...]
        him_ref[...] = h0im_ref[...]

    for b in range(nb):
        for j in range(din // V7X_LANES):
            col = b * din + j * V7X_LANES
            uslab_scr[j, pl.ds(b, ts, stride=nb), :] = u_ref[:, col:col + V7X_LANES]
    u_tm = jnp.concatenate([uslab_scr[j] for j in range(din // V7X_LANES)], axis=1)
    _s5_input_matmul(u_tm.astype(BF16), wb_ref, bure_scr, buim_scr)

    ns = are_ref.shape[1]
    for lc in range(ns // lane_chunk):
        sl = slice(lc * lane_chunk, (lc + 1) * lane_chunk)
        ar = are_ref[:, sl]
        ai = aim_ref[:, sl]
        sr = hre_ref[:, sl]
        si = him_ref[:, sl]
        for t in range(ts):
            rs = slice(t * nb, (t + 1) * nb)
            nr = ar * sr - ai * si + bure_scr[rs, sl]
            ni = ar * si + ai * sr + buim_scr[rs, sl]
            xre_scr[rs, sl] = nr
            xim_scr[rs, sl] = ni
            sr, si = nr, ni
        hre_ref[:, sl] = sr
        him_ref[:, sl] = si

    z_tm = _s5_output(xre_scr[...], xim_scr[...], u_tm, wcre_ref, wcim_ref, d_ref[...])
    out = _s5_glu(z_tm.astype(BF16), gv_ref, gg_ref)
    for j in range(dout // V7X_LANES):
        oslab_scr[j] = out[:, j * V7X_LANES:(j + 1) * V7X_LANES]
    for b in range(nb):
        for j in range(dout // V7X_LANES):
            col = b * dout + j * V7X_LANES
            o_ref[:, col:col + V7X_LANES] = oslab_scr[j, pl.ds(b, ts, stride=nb), :]


def _s5_rows_kernel(u_ref, h0re_ref, h0im_ref, are_ref, aim_ref, wb_ref, wcre_ref, wcim_ref, d_ref,
                    gv_ref, gg_ref, o_ref, hre_ref, him_ref, bure_scr, buim_scr):
    u = u_ref[...]
    _s5_input_matmul(u.astype(BF16), wb_ref, bure_scr, buim_scr)
    ar = are_ref[0:1, :]
    ai = aim_ref[0:1, :]
    sr = h0re_ref[...]
    si = h0im_ref[...]
    nr = ar * sr - ai * si + bure_scr[...]
    ni = ar * si + ai * sr + buim_scr[...]
    hre_ref[...] = nr
    him_ref[...] = ni
    z = _s5_output(nr, ni, u, wcre_ref, wcim_ref, d_ref[...])
    o_ref[...] = _s5_glu(z.astype(BF16), gv_ref, gg_ref)


def _s5_weight_specs(ws):
    return [_const_spec(w.shape) for w in ws]


def _s5_seq_call(s5_in, h0re, h0im, a_re, a_im, sw, *, ts):
    nb, ns = a_re.shape
    length = s5_in.shape[0]
    din = s5_in.shape[1] // nb
    d = sw["glu_v"].shape[1]
    rows = nb * ts
    ws = [sw["wb"], sw["wc_re"], sw["wc_im"], sw["d"], sw["glu_v"], sw["glu_g"]]
    kern = functools.partial(_s5_seq_kernel, lane_chunk=4 * V7X_LANES)
    return pl.pallas_call(
        kern,
        out_shape=(jax.ShapeDtypeStruct((length, nb * d), F32),
                   jax.ShapeDtypeStruct((nb, ns), F32), jax.ShapeDtypeStruct((nb, ns), F32)),
        grid=(length // ts,),
        in_specs=[pl.BlockSpec((ts, nb * din), lambda c: (c, 0)),
                  _const_spec((nb, ns)), _const_spec((nb, ns)), _const_spec((nb, ns)), _const_spec((nb, ns))]
                 + _s5_weight_specs(ws),
        out_specs=(pl.BlockSpec((ts, nb * d), lambda c: (c, 0)),
                   pl.BlockSpec((nb, ns), lambda c: (0, 0)), pl.BlockSpec((nb, ns), lambda c: (0, 0))),
        scratch_shapes=[pltpu.VMEM((rows, ns), F32) for _ in range(4)]
                       + [pltpu.VMEM((din // V7X_LANES, rows, V7X_LANES), F32),
                          pltpu.VMEM((d // V7X_LANES, rows, V7X_LANES), F32)],
        compiler_params=_cparams(("arbitrary",), 48 * 2**20),
        name="s5_seq",
    )(s5_in, h0re, h0im, a_re, a_im, *ws)


def _s5_rows_call(s5_in, h0re, h0im, a_re, a_im, sw):
    rows, din = s5_in.shape
    ns = a_re.shape[1]
    d = sw["glu_v"].shape[1]
    ws = [sw["wb"], sw["wc_re"], sw["wc_im"], sw["d"], sw["glu_v"], sw["glu_g"]]
    full = lambda shape: pl.BlockSpec(shape, lambda i: (0,) * len(shape))
    return pl.pallas_call(
        _s5_rows_kernel,
        out_shape=(jax.ShapeDtypeStruct((rows, d), F32),
                   jax.ShapeDtypeStruct((rows, ns), F32), jax.ShapeDtypeStruct((rows, ns), F32)),
        grid=(1,),
        in_specs=[full((rows, din)), full((rows, ns)), full((rows, ns)), full(a_re.shape), full(a_im.shape)]
                 + [full(w.shape) for w in ws],
        out_specs=(full((rows, d)), full((rows, ns)), full((rows, ns))),
        scratch_shapes=[pltpu.VMEM((rows, ns), F32) for _ in range(2)],
        compiler_params=_cparams(("arbitrary",), 40 * 2**20),
        name="s5_rows",
    )(s5_in, h0re, h0im, a_re, a_im, *ws)


def _headsum(x, ones_bd):
    hi = x.astype(BF16)
    lo = (x - hi.astype(F32)).astype(BF16)
    gw = ones_bd.shape[0]
    return jnp.concatenate([_dot(hi[:, s:s + gw], ones_bd) + _dot(lo[:, s:s + gw], ones_bd)
                            for s in range(0, x.shape[1], gw)], axis=1)


def _pre_lora(cur, prev, p, width):
    mixed = cur + p["mu"] * (prev - cur)
    nl = p["w2p"].shape[0]
    wa = mixed[:, 3 * width:3 * width + nl]
    gd = mixed[:, 3 * width + nl:]
    return dict(
        r=mixed[:, 0:width], k=mixed[:, width:2 * width], v=mixed[:, 2 * width:3 * width],
        lw=_dot(jnp.tanh(wa).astype(BF16), p["w2p"]),
        la=_dot(wa.astype(BF16), p["a2p"]),
        g=_dot(jax.nn.sigmoid(gd).astype(BF16), p["g2"]))


def _pre_gates(t, p, ones_bd):
    w = -_softplus(-(p["w0"] + t["lw"])) - 0.5
    t["ld"] = -jnp.exp(w)
    t["a"] = jax.nn.sigmoid(p["a0"] + t["la"])
    t["kk"] = t["k"] * p["k_k"]
    t["ss"] = _headsum(t["kk"] * t["kk"], ones_bd)


def _pre_keys(t, p):
    kk = t["kk"] * jnp.minimum(lax.rsqrt(t["ss"]), 1e12)
    t["k2"] = t["k"] * (1.0 + (t["a"] - 1.0) * p["k_a"])
    t["av"] = -kk
    t["bv"] = kk * t["a"]


def _rwkv_pre(cur, prev, p, ones_bd, width):
    t = _pre_lora(cur, prev, p, width)
    _pre_gates(t, p, ones_bd)
    _pre_keys(t, p)
    return t


def _rwkv_post(y, t, p, ones_bd, proj, head):
    inv = 1.0 / head
    dlt = y - _headsum(y, ones_bd) * inv
    var = _headsum(dlt * dlt, ones_bd) * inv
    bonus = _headsum(t["r"] * t["k2"] * p["r_k"], ones_bd) * t["v"]
    yn = dlt * lax.rsqrt(var + GN_EPS) * p["ln_w"] + p["ln_b"]
    return _dot(((yn + bonus) * t["g"]).astype(BF16), proj)


def _head_blocks(x, bd_mask):
    return jnp.concatenate([x] * (bd_mask.shape[0] // x.shape[0]), axis=0) * bd_mask


def _wkv_chunks(t, s_scr, y_ref, tril, bd_mask, head, chunk, nseq):
    rows, width = t["r"].shape
    c = chunk
    gw = bd_mask.shape[0]
    assert c == head and width % gw == 0, "head packing needs chunk == head size and whole lane groups"
    per_seq = rows // (nseq * c)
    chunks = [(q * per_seq + i, q) for q in range(nseq) for i in range(per_seq)]
    groups = [(j, q, g) for j, q in chunks for g in range(width // gw)]
    gs = lambda g: slice(g * gw, (g + 1) * gw)
    ri = lax.broadcasted_iota(jnp.int32, (c, gw), 0)
    ci = lax.broadcasted_iota(jnp.int32, (c, gw), 1) % c
    strict = ri > ci
    incl = ri >= ci
    blocks = functools.partial(_head_blocks, bd_mask=bd_mask)
    on_diag = (lax.broadcasted_iota(jnp.int32, (gw, gw), 0) // head
               == lax.broadcasted_iota(jnp.int32, (gw, gw), 1) // head)

    fac = {}
    for j, _ in chunks:
        rs = slice(j * c, (j + 1) * c)
        lds = t["ld"][rs]
        hi, mid, lo = _split3(lds)
        fac[j] = dict(lds=lds, rs=rs, cum=_dot(tril, hi) + _dot(tril, mid) + _dot(tril, lo))
    for j, _ in chunks:
        f = fac[j]
        cum, rs = f["cum"], f["rs"]
        g_inv = jnp.exp(-cum)
        cum_end = cum[c - 1:c, :]
        g_rel = jnp.exp(cum_end - cum)
        f.update(ar=jnp.concatenate([t["av"][rs] * jnp.exp(cum - f["lds"]), t["r"][rs] * jnp.exp(cum)],
                                    axis=0).astype(BF16),
                 b_t=(t["bv"][rs] * g_inv).astype(BF16), k_t=(t["k2"][rs] * g_inv).astype(BF16),
                 bk_e=jnp.concatenate([t["bv"][rs] * g_rel, t["k2"][rs] * g_rel], axis=0).astype(BF16),
                 g_end=jnp.exp(cum_end), v=t["v"][rs].astype(BF16))

    xb, xk, pw, nm, m_rb, lvmv = ({} for _ in range(6))
    for j, _, g in groups:
        f = fac[j]
        xb[j, g] = _dot_nt(f["ar"][:, gs(g)], blocks(f["b_t"][:, gs(g)]))
        xk[j, g] = _dot_nt(f["ar"][:, gs(g)], blocks(f["k_t"][:, gs(g)]))
    for j, _, g in groups:
        pw[j, g] = jnp.where(strict, xb[j, g][:c], 0.0)
        m_rb[j, g] = jnp.where(incl, xb[j, g][c:], 0.0).astype(BF16)
        akrk = jnp.concatenate([jnp.where(strict, xk[j, g][:c], 0.0), jnp.where(incl, xk[j, g][c:], 0.0)], axis=0)
        lvmv[j, g] = _dot(akrk.astype(BF16), blocks(fac[j]["v"][:, gs(g)]))
    n_pow = int(math.log2(c))
    for i in range(n_pow):
        for j, _, g in groups:
            pb = pw[j, g].astype(BF16)
            lhs = [pb] if i < n_pow - 1 else []
            lhs += [nm[j, g].astype(BF16)] if i > 0 else []
            prod = _dot(jnp.concatenate(lhs, axis=0), blocks(pb))
            if i > 0:
                nm[j, g] = nm[j, g] + pw[j, g] + prod[-c:]
            else:
                nm[j, g] = pw[j, g]
            if i < n_pow - 1:
                pw[j, g] = prod[:c]

    for i in range(per_seq):
        now = [(j, q, g) for j, q, g in groups if j == q * per_seq + i]
        s0, am, ub = {}, {}, {}
        for j, q, g in now:
            s0[j, g] = s_scr[q, g]
            am[j, g] = _dot_nt(fac[j]["ar"][:, gs(g)], s0[j, g].astype(BF16))
        for j, q, g in now:
            w = am[j, g][:c] + lvmv[j, g][:c]
            ub[j, g] = (w + _dot(nm[j, g].astype(BF16), blocks(w.astype(BF16)))).astype(BF16)
        for j, q, g in now:
            f = fac[j]
            y_ref[f["rs"], gs(g)] = am[j, g][c:] + _dot(m_rb[j, g], blocks(ub[j, g])) + lvmv[j, g][c:]
            uv = jnp.concatenate([ub[j, g], f["v"][:, gs(g)]], axis=0)
            s_scr[q, g] = jnp.where(on_diag, s0[j, g] * f["g_end"][:, gs(g)] + _dot_tn(uv, f["bk_e"][:, gs(g)]), 0.0)


_RWKV_VEC_NAMES = ("w0", "a0", "k_k", "k_a", "r_k", "ln_w", "ln_b")
_RWKV_CONSTS = ("mu", "vecs", "w2p", "a2p", "g2")


def _rwkv_params(mu_ref, vecs_ref, w2p_ref, a2p_ref, g2_ref):
    p = {n: vecs_ref[i:i + 1, :] for i, n in enumerate(_RWKV_VEC_NAMES)}
    p.update(mu=mu_ref[...], w2p=w2p_ref[...], a2p=a2p_ref[...], g2=g2_ref[...])
    return p


def _rwkv_seq_kernel(cur_ref, shift0_ref, s0_ref, *rest, chunk, head):
    np_ = len(_RWKV_CONSTS)
    p = _rwkv_params(*rest[:np_])
    proj_ref, ones_ref, tril_ref = rest[np_:np_ + 3]
    out_ref, st_ref = rest[np_ + 3:np_ + 5]
    carry_scr, y_scr, s_scr = rest[np_ + 5:]
    nseq, tr, _ = cur_ref.shape
    width = y_scr.shape[1]
    gw = s_scr.shape[2]
    blocks = [(q, h, (h * head) // gw, slice((h * head) % gw, (h * head) % gw + head))
              for q in range(nseq) for h in range(width // head)]

    @pl.when(pl.program_id(0) == 0)
    def _():
        carry_scr[...] = shift0_ref[...]
        s_scr[...] = jnp.zeros(s_scr.shape, F32)
        for q, h, g, sl in blocks:
            s_scr[q, g, sl, sl] = s0_ref[q, h]

    row = lax.broadcasted_iota(jnp.int32, cur_ref.shape[1:], 0)
    curs, prevs = [], []
    for q in range(nseq):
        cq = cur_ref[q]
        prevs.append(jnp.where(row == 0, carry_scr[q], pltpu.roll(cq, 1, axis=0)))
        carry_scr[q] = cq[tr - 1:tr, :]
        curs.append(cq)
    ones_bd = ones_ref[...]
    t = _rwkv_pre(jnp.concatenate(curs, axis=0), jnp.concatenate(prevs, axis=0), p, ones_bd, width)
    _wkv_chunks(t, s_scr, y_scr, tril_ref[...], ones_bd, head, chunk, nseq)
    out = _rwkv_post(y_scr[...], t, p, ones_bd, proj_ref[...], head)
    for q in range(nseq):
        out_ref[q] = out[q * tr:(q + 1) * tr]

    @pl.when(pl.program_id(0) == pl.num_programs(0) - 1)
    def _():
        for q, h, g, sl in blocks:
            st_ref[q, h] = s_scr[q, g, sl, sl]


def _rwkv_rows_pre_kernel(cur_ref, prev_ref, mu_ref, vecs_ref, w2p_ref, a2p_ref, g2_ref, ones_ref,
                          r_ref, k_ref, v_ref, g_ref, t_ref, *, width):
    p = _rwkv_params(mu_ref, vecs_ref, w2p_ref, a2p_ref, g2_ref)
    t = _rwkv_pre(cur_ref[...], prev_ref[...], p, ones_ref[...], width)
    r_ref[...] = t["r"]
    k_ref[...] = t["k2"]
    v_ref[...] = t["v"]
    g_ref[...] = t["g"]
    for i, n in enumerate(("r", "ld", "k2", "v", "av", "bv")):
        t_ref[i] = t[n].T


def _wkv_step_kernel(t_ref, s_ref, y_ref, so_ref):
    r = t_ref[0]
    dec = jnp.exp(t_ref[1])
    k = t_ref[2]
    a = t_ref[4]
    b = t_ref[5]

    def body(i, carry):
        s = s_ref[i]
        sa = jnp.sum(s * a, axis=0, keepdims=True)
        s2 = s * dec + sa * b + t_ref[3, pl.ds(i, 1), :] * k
        so_ref[i] = s2
        y_ref[pl.ds(i, 1), :] = jnp.sum(s2 * r, axis=0, keepdims=True)
        return carry

    lax.fori_loop(0, s_ref.shape[0], body, 0, unroll=V7X_SUBLANES)


def _rwkv_rows_post_kernel(yt_ref, r_ref, k_ref, v_ref, g_ref, mu_ref, vecs_ref, w2p_ref, a2p_ref, g2_ref,
                           proj_ref, ones_ref, o_ref, *, head):
    p = _rwkv_params(mu_ref, vecs_ref, w2p_ref, a2p_ref, g2_ref)
    t = dict(r=r_ref[...], k2=k_ref[...], v=v_ref[...], g=g_ref[...])
    o_ref[...] = _rwkv_post(yt_ref[...].T, t, p, ones_ref[...], proj_ref[...], head)


def _rwkv_param_list(rw):
    return [rw[n] for n in _RWKV_CONSTS]


def _rwkv_seq_call(cur3, shift0, s0, rw, *, tr):
    nb, length, n_shift = cur3.shape
    nh, head = s0.shape[1], s0.shape[2]
    width = nh * head
    d = rw["proj"].shape[1]
    chunk = min(WKV_CHUNK, tr)
    gw = rw["ones_bd"].shape[0]
    tril =(jnp.arange(chunk)[:, None] >= jnp.arange(chunk)[None, :]).astype(BF16)
    consts = _rwkv_param_list(rw) + [rw["proj"], rw["ones_bd"], tril]
    whole = lambda c: pl.BlockSpec(c.shape, lambda i, nd=c.ndim: (0,) * nd)
    shift3 = shift0.reshape(nb, 1, n_shift)
    kern = functools.partial(_rwkv_seq_kernel, chunk=chunk, head=head)
    return pl.pallas_call(
        kern,
        out_shape=(jax.ShapeDtypeStruct((nb, length, d), F32),
                   jax.ShapeDtypeStruct((nb, nh, head, head), F32)),
        grid=(length // tr,),
        in_specs=[pl.BlockSpec((nb, tr, n_shift), lambda i: (0, i, 0)), whole(shift3), whole(s0)]
                 + [whole(c) for c in consts],
        out_specs=(pl.BlockSpec((nb, tr, d), lambda i: (0, i, 0)), whole(s0)),
        scratch_shapes=[pltpu.VMEM((nb, 1, n_shift), F32), pltpu.VMEM((nb * tr, width), F32),
                        pltpu.VMEM((nb, width // gw, gw, gw), F32)],
        compiler_params=_cparams(("arbitrary",), 40 * 2**20),
        name="rwkv_seq",
    )(cur3, shift3, s0, *consts)


def _rwkv_rows_call(cur, prev, s0_t, rw):
    rows, n_shift = cur.shape
    nh, head = s0_t.shape[0], s0_t.shape[1]
    width = nh * head
    d = rw["proj"].shape[1]
    params = _rwkv_param_list(rw)
    full = lambda shape: pl.BlockSpec(shape, lambda i: (0,) * len(shape))
    vec = jax.ShapeDtypeStruct((rows, width), F32)
    n_t = 6

    pre_consts = params + [rw["ones_bd"]]
    r, k2, v, g, t = pl.pallas_call(
        functools.partial(_rwkv_rows_pre_kernel, width=width),
        out_shape=(vec,) * 4 + (jax.ShapeDtypeStruct((n_t, width, rows), F32),),
        grid=(1,),
        in_specs=[full(cur.shape), full(prev.shape)] + [full(c.shape) for c in pre_consts],
        out_specs=(full((rows, width)),) * 4 + (full((n_t, width, rows)),),
        compiler_params=_cparams(("arbitrary",), 32 * 2**20),
        name="rwkv_rows_pre",
    )(cur, prev, *pre_consts)

    sblk = pl.BlockSpec((None, head, head, rows), lambda h: (h, 0, 0, 0))
    yt, s1_t = pl.pallas_call(
        _wkv_step_kernel,
        out_shape=(jax.ShapeDtypeStruct((width, rows), F32), jax.ShapeDtypeStruct(s0_t.shape, F32)),
        grid=(nh,),
        in_specs=[pl.BlockSpec((n_t, head, rows), lambda h: (0, h, 0)), sblk],
        out_specs=(pl.BlockSpec((head, rows), lambda h: (h, 0)), sblk),
        compiler_params=_cparams(("parallel",), 32 * 2**20),
        name="wkv_step",
    )(t, s0_t)

    post_consts = params + [rw["proj"], rw["ones_bd"]]
    out = pl.pallas_call(
        functools.partial(_rwkv_rows_post_kernel, head=head),
        out_shape=jax.ShapeDtypeStruct((rows, d), F32),
        grid=(1,),
        in_specs=[full((width, rows))] + [full((rows, width))] * 4 + [full(c.shape) for c in post_consts],
        out_specs=full((rows, d)),
        compiler_params=_cparams(("arbitrary",), 32 * 2**20),
        name="rwkv_rows_post",
    )(yt, r, k2, v, g, *post_consts)
    return out, s1_t


def _merge_tile(h_ref, gates_ref, ys5_ref, yrw_ref, gt2, sh, sc, gt, wout_ref, g_ref, w1_ref, w3_ref, w2_ref,
                gfin_ref, o_ref, n_chunks):
    d = h_ref.shape[1]
    parts = _row_parts(h_ref.shape[0], ROW_PARTS)
    mixed = []
    for part in parts:
        m = (jax.nn.sigmoid(gates_ref[part, :d]) * ys5_ref[part, :]
             + jax.nn.sigmoid(gates_ref[part, d:]) * yrw_ref[part, :])
        mixed.append(_dot(m.astype(BF16), wout_ref[...]))
    for part, mo in zip(parts, mixed):
        h2 = h_ref[part, :] + _mod_part(gt2, part) * mo
        h3 = _ffn_core(h2, _mod_part(sh, part), _mod_part(sc, part), _mod_part(gt, part), g_ref[...],
                       w1_ref, w3_ref, w2_ref, n_chunks)
        o_ref[part, :] = _rms(h3, gfin_ref[...])


def _merge_kernel(h_ref, gates_ref, ys5_ref, yrw_ref, gt2_ref, sh_ref, sc_ref, gt_ref,
                  wout_ref, g_ref, w1_ref, w3_ref, w2_ref, gfin_ref, *rest, n_chunks, tiles_per_seq, n_main):
    weights = (wout_ref, g_ref, w1_ref, w3_ref, w2_ref, gfin_ref)

    def main():
        row = functools.partial(_mod_row, tiles_per_seq=tiles_per_seq)
        _merge_tile(h_ref, gates_ref, ys5_ref, yrw_ref, row(gt2_ref), row(sh_ref), row(sc_ref), row(gt_ref),
                    *weights, rest[-2] if len(rest) > 1 else rest[0], n_chunks)

    def rider():
        hs, gs, ss, rs, gt2s, shs, scs, gts = rest[:8]
        _merge_tile(hs, gs, ss, rs, gt2s[...], shs[...], scs[...], gts[...], *weights, rest[-1], n_chunks)

    _with_rider(n_main, main, rider if len(rest) > 1 else None)


def _merge_call(h, gates, ys5, yrw, mods, ks, w_out, g, w1, w3, w2, g_fin, *, tm, rider=None):
    rows, d = h.shape
    d_ff = w1.shape[1]
    n_steps = rows // tm
    assert rider is None or mods.tiles_per_seq is not None
    kern = functools.partial(_merge_kernel, n_chunks=_ffn_chunks(d_ff), tiles_per_seq=mods.tiles_per_seq,
                             n_main=n_steps)
    row_spec = lambda w: pl.BlockSpec((tm, w), lambda i: (jnp.minimum(i, n_steps - 1), 0))
    weights = (3 * d * d_ff + d * d) * 2
    tiles = 2 * tm * 6 * d * 4 + 4 * tm * d_ff * 4 + 6 * tm * d * 4
    ys5_spec = row_spec(d) if mods.tiles_per_seq is None else _time_major(rows, d, tm, mods.tiles_per_seq)[1]
    r_specs, r_args = _rider_specs(rider[:4], rider[4], ks) if rider else ([], [])
    r_out = [pl.BlockSpec(rider[0].shape, lambda i: (0, 0))] if rider else []
    out = pl.pallas_call(
        kern,
        out_shape=(jax.ShapeDtypeStruct((rows, d), F32),)
                  + tuple(jax.ShapeDtypeStruct(s.block_shape, F32) for s in r_out),
        grid=(n_steps + len(r_out),),
        in_specs=[row_spec(d), row_spec(2 * d), ys5_spec, row_spec(d),
                  mods.spec(ks[0]), mods.spec(ks[1]), mods.spec(ks[2]), mods.spec(ks[3]),
                  _const_spec(w_out.shape),
                  _const_spec((1, d)), _const_spec(w1.shape), _const_spec(w3.shape), _const_spec(w2.shape),
                  _const_spec((1, d))] + r_specs,
        out_specs=(row_spec(d),) + tuple(r_out),
        compiler_params=_cparams(("arbitrary",), weights + tiles + 8 * 2**20),
        name="merge_ffn2",
    )(h, gates, ys5, yrw, mods.mod, mods.mod, mods.mod, mods.mod, w_out, g.reshape(1, d), w1, w3, w2,
      g_fin.reshape(1, d), *r_args)
    return out[0], (out[1] if rider else None)


_LATE_WEIGHTS = ("ffn2_w1", "ffn2_w3", "ffn2_w2", "w_in", "w_out", "glu_v", "glu_g", "proj", "g2", "w2p", "a2p")


def _pick_tile(n, target):
    t = min(n, target)
    while n % t:
        t -= V7X_SUBLANES
    return t


def _mixers(h1, mods, nb, length, s5_re0, s5_im0, wkv0, shift0, w, *, tm):
    rows, d = h1.shape
    g, n = s5_re0.shape[1], s5_re0.shape[2]
    n_shift = shift0.shape[1]
    widths = (w["s5"]["d"].shape[1], n_shift, 2 * d)
    s5_in, cur, gates = _proj_call(h1, mods, (3, 4), w["g_mix"], w["w_in"], widths, tm=tm)
    h0re = s5_re0.reshape(nb, g * n)
    h0im = s5_im0.reshape(nb, g * n)
    if mods.tiles_per_seq is not None:
        ts = _pick_tile(length, S5_STEP_ROWS // nb)
        y_s5, hre, him = _s5_seq_call(s5_in, h0re, h0im, w["a_re"], w["a_im"], w["s5"], ts=ts)
        tr = _pick_tile(length, max(WKV_CHUNK, WKV_STEP_ROWS // nb))
        y_rw, wkv1 = _rwkv_seq_call(cur.reshape(nb, length, n_shift), shift0, wkv0, w["rwkv"], tr=tr)
        y_rw = y_rw.reshape(rows, d)
    else:
        y_s5, hre, him = _s5_rows_call(s5_in, h0re, h0im, w["a_re"], w["a_im"], w["s5"])
        y_rw, wkv1 = _rwkv_rows_call(cur, shift0, wkv0, w["rwkv"])
    shift1 = cur.reshape(nb, length, n_shift)[:, -1]
    return (gates, y_s5, y_rw), (hre.reshape(nb, g, n), him.reshape(nb, g, n), wkv1, shift1)


def _layers(x_seq, x_row, mod, seq_states, row_states, w):
    nb, length, d = x_seq.shape
    n_row = x_row.shape[0]
    tm = _pick_tile(length, 512)
    mods = _ModRows(mod, n_row, nb, tm, length // tm)
    mods_row = _ModRows(mod, 0, n_row, n_row, None)

    late = w["late"]
    to_cast = list(_LATE_WEIGHTS)
    h_seq, cast, h_row = _ffn_call(x_seq.reshape(nb * length, d), mods, (0, 1, 2), w["g_ffn1"], *w["ffn1"], tm=tm,
                                   casts=[late[n] for n in to_cast], rider=(x_row.reshape(n_row, d), mods_row))
    late = dict(zip(to_cast, cast))
    w = dict(w, w_in=late["w_in"],
             s5=dict(w["s5"], glu_v=late["glu_v"], glu_g=late["glu_g"]),
             rwkv=dict(w["rwkv"], proj=late["proj"], g2=late["g2"], w2p=late["w2p"], a2p=late["a2p"]))
    mix_seq, st_seq = _mixers(h_seq, mods, nb, length, *seq_states, w, tm=tm)
    mix_row, st_row = _mixers(h_row, mods_row, n_row, 1, *row_states, w, tm=n_row)
    y_seq, y_row = _merge_call(h_seq, *mix_seq, mods, (5, 6, 7, 8), late["w_out"], w["g_ffn2"],
                               late["ffn2_w1"], late["ffn2_w3"], late["ffn2_w2"], w["g_final"], tm=tm,
                               rider=(h_row, *mix_row, mods_row))
    return (y_seq.reshape(nb, length, d),) + st_seq, (y_row.reshape(n_row, 1, d),) + st_row


def kernel(x_prompt, x_sample, c_prompt, c_sample, state_s5_re, state_s5_im, state_wkv, state_shift, w_ada, b_ada, g_ffn1, g_mix, g_ffn2, g_final, ffn1_w1, ffn1_w3, ffn1_w2, ffn2_w1, ffn2_w3, ffn2_w2, w_in, mu_shift, s5_lam_re, s5_lam_im, s5_log_dt, s5_b_re, s5_b_im, s5_c_re, s5_c_im, s5_d, s5_glu_v, s5_glu_g, rwkv_w0, rwkv_w2, rwkv_a0, rwkv_a2, rwkv_g2, rwkv_k_k, rwkv_k_a, rwkv_r_k, rwkv_ln_w, rwkv_ln_b, rwkv_proj, w_out):
    bp, _, d = x_prompt.shape
    bs = x_sample.shape[0]
    g, n = s5_lam_re.shape
    nh, head = rwkv_r_k.shape
    width = nh * head
    n_shift = mu_shift.shape[0]
    assert bs % bp == 0, "sample rows come first in the modulation array; prompt rows must stay block-aligned"
    bf = lambda t: t.astype(BF16)

    a_re, a_im, wb, wc_re, wc_im = _s5_prep_call(
        s5_lam_re, s5_lam_im, s5_log_dt, jnp.swapaxes(s5_b_re, 1, 2), jnp.swapaxes(s5_b_im, 1, 2),
        s5_c_re, s5_c_im, bp)
    w_lora = rwkv_w2.shape[0]
    a_lora = rwkv_a2.shape[0]
    w2p = jnp.concatenate([rwkv_w2, jnp.zeros((a_lora, width), F32)], axis=0)
    a2p = jnp.concatenate([jnp.zeros((w_lora, width), F32), rwkv_a2], axis=0)
    vec_rows = [rwkv_w0, rwkv_a0, rwkv_k_k, rwkv_k_a, rwkv_r_k.reshape(width), rwkv_ln_w, rwkv_ln_b]
    vecs = jnp.stack(vec_rows + [jnp.zeros_like(rwkv_w0)] * (V7X_SUBLANES - len(vec_rows)))
    rw = dict(mu=mu_shift.reshape(1, n_shift), vecs=vecs,
              ones_bd=jnp.kron(jnp.eye(min(width, V7X_MXU_DIM) // head, dtype=F32),
                               jnp.ones((head, head), F32)).astype(BF16))
    late = dict(ffn2_w1=ffn2_w1, ffn2_w3=ffn2_w3, ffn2_w2=ffn2_w2, w_in=w_in, w_out=w_out,
                glu_v=s5_glu_v, glu_g=s5_glu_g, proj=rwkv_proj, g2=rwkv_g2, w2p=w2p, a2p=a2p)
    w = dict(
        g_ffn1=g_ffn1, g_mix=g_mix, g_ffn2=g_ffn2, g_final=g_final,
        ffn1=(bf(ffn1_w1), bf(ffn1_w3), bf(ffn1_w2)),
        a_re=a_re, a_im=a_im,
        s5=dict(wb=wb, wc_re=wc_re, wc_im=wc_im, d=s5_d.reshape(1, -1)),
        rwkv=rw, late=late,
    )

    mod = _mod_call(jnp.concatenate([c_sample, c_prompt], axis=0), w_ada, b_ada)

    z_s5 = jnp.zeros((bp, g, n), state_s5_re.dtype)
    z_wkv = jnp.zeros((bp, nh, head, head), state_wkv.dtype)
    z_shift = jnp.zeros((bp, n_shift), state_shift.dtype)
    (y_p, s5re_p, s5im_p, wkv_p, shift_p), (y_s, s5re_s, s5im_s, wkv_s_t, shift_s) = _layers(
        x_prompt, x_sample, mod, (z_s5, z_s5, z_wkv, z_shift),
        (state_s5_re, state_s5_im, jnp.transpose(state_wkv, (1, 2, 3, 0)), state_shift), w)
    wkv_s = jnp.transpose(wkv_s_t, (3, 0, 1, 2))
    return (y_p, y_s, s5re_p, s5im_p, wkv_p, shift_p, s5re_s, s5im_s, wkv_s, shift_s)
```
